```python
import jax, jax.numpy as jnp
from jax import lax
import numpy as np

D_MODEL = 1024
BATCH = 4
SEQ = 4096
DEPTH = 2
DEC_BATCH = 4
DEC_SEQ = 8192
PAST_LEN = 128

N_MEM = 256
EPS = 1e-6
ROPE_THETA = 500000.0
Q_BLOCK = 128

A_HEADS = 8
A_HEAD_DIM = 64
A_WIDTH = A_HEADS * A_HEAD_DIM
A_ROPE_DIM = A_HEAD_DIM // 4
DILATED_BRANCHES = ((128, 1), (512, 4), (2048, 16))

B_HEADS = 8
B_NOPE_DIM = 64
B_ROPE_DIM = 32
B_QK_DIM = B_NOPE_DIM + B_ROPE_DIM
B_V_DIM = 64
B_WIDTH = B_HEADS * B_V_DIM
Q_LORA = 256
KV_LORA = 128

MIX_WIDTH = A_WIDTH + B_WIDTH
IN_SPLITS = (A_WIDTH, 2 * A_WIDTH, 3 * A_WIDTH, 3 * A_WIDTH + Q_LORA, 3 * A_WIDTH + Q_LORA + KV_LORA)
IN_COLS = 3 * A_WIDTH + Q_LORA + KV_LORA + B_ROPE_DIM

M_HEADS = 4
M_HEAD_DIM = 128
M_WIDTH = M_HEADS * M_HEAD_DIM

D_FF = 4 * D_MODEL

kernel_name = 'hybrid_dilated_mla_memory_encoder'


def rmsnorm(x, g):
    xf = x.astype(jnp.float32)
    y = xf * lax.rsqrt(jnp.mean(xf * xf, axis=-1, keepdims=True) + EPS)
    return (y * g.astype(jnp.float32)).astype(x.dtype)


def rope(x, pos):
    r = x.shape[-1]
    inv = ROPE_THETA ** (-jnp.arange(0, r, 2, dtype=jnp.float32) / r)
    ang = pos.astype(jnp.float32)[:, None] * inv[None, :]
    cos = jnp.cos(ang)[:, None, :]
    sin = jnp.sin(ang)[:, None, :]
    xf = x.astype(jnp.float32)
    x1, x2 = xf[..., : r // 2], xf[..., r // 2:]
    return jnp.concatenate([x1 * cos - x2 * sin, x2 * cos + x1 * sin], axis=-1).astype(x.dtype)


def partial_rope(x, pos, rdim):
    return jnp.concatenate([rope(x[..., :rdim], pos), x[..., rdim:]], axis=-1)


def dilated_mixture_attention(q, k, v):
    b, s, h, dh = q.shape
    nblk = s // Q_BLOCK
    q_blocks = q.reshape(b, nblk, Q_BLOCK, h, dh).transpose(1, 0, 2, 3, 4)
    starts = jnp.arange(nblk, dtype=jnp.int32) * Q_BLOCK
    local = jnp.arange(Q_BLOCK, dtype=jnp.int32)
    offsets = [jnp.asarray(np.arange(-(w // 2), w // 2 + 1, d), dtype=jnp.int32) for w, d in DILATED_BRANCHES]

    def one_block(args):
        q_blk, start = args
        qpos = start + local
        maxes, dens, nums = [], [], []
        for offs in offsets:
            idx = qpos[:, None] + offs[None, :]
            valid = (idx >= 0) & (idx < s)
            idx = jnp.clip(idx, 0, s - 1)
            k_g = k[:, idx]
            v_g = v[:, idx]
            sc = jnp.einsum('bqhd,bqkhd->bhqk', q_blk, k_g).astype(jnp.float32)
            sc = jnp.where(valid[None, None], sc, -jnp.inf)
            mx = jnp.max(sc, axis=-1)
            p = jnp.exp(sc - mx[..., None])
            maxes.append(mx)
            dens.append(jnp.sum(p, axis=-1))
            nums.append(jnp.einsum('bhqk,bqkhd->bhqd', p, v_g.astype(jnp.float32)))
        m_all = jnp.max(jnp.stack(maxes), axis=0)
        wts = [jnp.exp(mx - m_all) for mx in maxes]
        den = sum(wi * di for wi, di in zip(wts, dens))
        num = sum(wi[..., None] * ni for wi, ni in zip(wts, nums))
        out = num / den[..., None]
        return out.transpose(0, 2, 1, 3).astype(q.dtype)

    out = lax.map(one_block, (q_blocks, starts))
    return out.transpose(1, 0, 2, 3, 4).reshape(b, s, h, dh)


def blocked_dense_attention(q, k, v):
    b, s, h, dq = q.shape
    nblk = s // Q_BLOCK
    q_blocks = q.reshape(b, nblk, Q_BLOCK, h, dq).transpose(1, 0, 2, 3, 4)

    def one_block(q_blk):
        sc = jnp.einsum('bqhd,bkhd->bhqk', q_blk, k).astype(jnp.float32)
        p = jax.nn.softmax(sc, axis=-1)
        return jnp.einsum('bhqk,bkhd->bqhd', p.astype(v.dtype), v)

    out = lax.map(one_block, q_blocks)
    return out.transpose(1, 0, 2, 3, 4).reshape(b, s, h, v.shape[-1])


def encoder_trunk(x, mem, params):
    (norm_mix_g, w_in, a_q_norm_g, a_k_norm_g, b_cq_norm_g, b_ckv_norm_g, w_uq, w_ukv,
     b_q_norm_g, b_k_norm_g, a_out_norm_g, b_out_norm_g, w_out,
     norm_mem_g, mem_kv_norm_g, m_wq, m_wkv, m_q_norm_g, m_k_norm_g, m_wo,
     norm_ffn_g, w_ff1, w_ff2) = params
    bsz, s, _ = x.shape
    pos = jnp.arange(s, dtype=jnp.int32)
    for l in range(DEPTH):
        h = rmsnorm(x, norm_mix_g[l])
        proj = h @ w_in[l]
        qa, ka, va, cq, ckv, kr = jnp.split(proj, IN_SPLITS, axis=-1)

        qa = qa.reshape(bsz, s, A_HEADS, A_HEAD_DIM)
        ka = ka.reshape(bsz, s, A_HEADS, A_HEAD_DIM)
        va = va.reshape(bsz, s, A_HEADS, A_HEAD_DIM)
        qa = partial_rope(rmsnorm(qa, a_q_norm_g[l]), pos, A_ROPE_DIM) * (A_HEAD_DIM ** -0.5)
        ka = partial_rope(rmsnorm(ka, a_k_norm_g[l]), pos, A_ROPE_DIM)
        oa = dilated_mixture_attention(qa, ka, va).reshape(bsz, s, A_WIDTH)

        qb = (rmsnorm(cq, b_cq_norm_g[l]) @ w_uq[l]).reshape(bsz, s, B_HEADS, B_QK_DIM)
        kvb = (rmsnorm(ckv, b_ckv_norm_g[l]) @ w_ukv[l]).reshape(bsz, s, B_HEADS, B_NOPE_DIM + B_V_DIM)
        k_nope, vb = kvb[..., :B_NOPE_DIM], kvb[..., B_NOPE_DIM:]
        k_rope = jnp.broadcast_to(kr[:, :, None, :], (bsz, s, B_HEADS, B_ROPE_DIM))
        kb = jnp.concatenate([k_nope, k_rope], axis=-1)
        qb = rmsnorm(qb, b_q_norm_g[l])
        kb = rmsnorm(kb, b_k_norm_g[l])
        qb = jnp.concatenate([qb[..., :B_NOPE_DIM], rope(qb[..., B_NOPE_DIM:], pos)], axis=-1) * (B_QK_DIM ** -0.5)
        kb = jnp.concatenate([kb[..., :B_NOPE_DIM], rope(kb[..., B_NOPE_DIM:], pos)], axis=-1)
        ob = blocked_dense_attention(qb, kb, vb).reshape(bsz, s, B_WIDTH)

        mixed = jnp.concatenate([rmsnorm(oa, a_out_norm_g[l]), rmsnorm(ob, b_out_norm_g[l])], axis=-1)
        x = x + mixed @ w_out[l]

        h = rmsnorm(x, norm_mem_g[l])
        mm = rmsnorm(mem, mem_kv_norm_g[l])
        qm = (h @ m_wq[l]).reshape(bsz, s, M_HEADS, M_HEAD_DIM)
        kvm = (mm @ m_wkv[l]).reshape(bsz, N_MEM, 2, M_HEADS, M_HEAD_DIM)
        km, vm = kvm[:, :, 0], kvm[:, :, 1]
        qm = rmsnorm(qm, m_q_norm_g[l]) * (M_HEAD_DIM ** -0.5)
        km = rmsnorm(km, m_k_norm_g[l])
        sc = jnp.einsum('bqhd,bkhd->bhqk', qm, km).astype(jnp.float32)
        p = jax.nn.softmax(sc, axis=-1)
        om = jnp.einsum('bhqk,bkhd->bqhd', p.astype(vm.dtype), vm).reshape(bsz, s, M_WIDTH)
        x = x + om @ m_wo[l]

        h = rmsnorm(x, norm_ffn_g[l])
        x = x + jnp.square(jax.nn.relu(h @ w_ff1[l])) @ w_ff2[l]
    return x


def setup_inputs(seed: int = 0) -> dict:
    key = jax.random.key(seed)
    ks = jax.random.split(key, 32)

    def dense(k, shape, fan_in):
        return jax.random.normal(k, shape, jnp.float32) * (fan_in ** -0.5)

    def gain(k, shape):
        return 1.0 + 0.02 * jax.random.normal(k, shape, jnp.float32)

    L = DEPTH
    return {
        'x_prompt': jax.random.normal(ks[0], (BATCH, SEQ, D_MODEL), jnp.float32),
        'x_sample': jax.random.normal(ks[1], (DEC_BATCH, DEC_SEQ, D_MODEL), jnp.float32),
        'mem_prompt': jax.random.normal(ks[2], (BATCH, N_MEM, D_MODEL), jnp.float32),
        'mem_sample': jax.random.normal(ks[3], (DEC_BATCH, N_MEM, D_MODEL), jnp.float32),
        'norm_mix_g': gain(ks[4], (L, D_MODEL)),
        'w_in': dense(ks[5], (L, D_MODEL, IN_COLS), D_MODEL),
        'a_q_norm_g': gain(ks[6], (L, A_HEAD_DIM)),
        'a_k_norm_g': gain(ks[7], (L, A_HEAD_DIM)),
        'b_cq_norm_g': gain(ks[8], (L, Q_LORA)),
        'b_ckv_norm_g': gain(ks[9], (L, KV_LORA)),
        'w_uq': dense(ks[10], (L, Q_LORA, B_HEADS * B_QK_DIM), Q_LORA),
        'w_ukv': dense(ks[11], (L, KV_LORA, B_HEADS * (B_NOPE_DIM + B_V_DIM)), KV_LORA),
        'b_q_norm_g': gain(ks[12], (L, B_QK_DIM)),
        'b_k_norm_g': gain(ks[13], (L, B_QK_DIM)),
        'a_out_norm_g': gain(ks[14], (L, A_WIDTH)),
        'b_out_norm_g': gain(ks[15], (L, B_WIDTH)),
        'w_out': dense(ks[16], (L, MIX_WIDTH, D_MODEL), MIX_WIDTH),
        'norm_mem_g': gain(ks[17], (L, D_MODEL)),
        'mem_kv_norm_g': gain(ks[18], (L, D_MODEL)),
        'm_wq': dense(ks[19], (L, D_MODEL, M_WIDTH), D_MODEL),
        'm_wkv': dense(ks[20], (L, D_MODEL, 2 * M_WIDTH), D_MODEL),
        'm_q_norm_g': gain(ks[21], (L, M_HEAD_DIM)),
        'm_k_norm_g': gain(ks[22], (L, M_HEAD_DIM)),
        'm_wo': dense(ks[23], (L, M_WIDTH, D_MODEL), M_WIDTH),
        'norm_ffn_g': gain(ks[24], (L, D_MODEL)),
        'w_ff1': dense(ks[25], (L, D_MODEL, D_FF), D_MODEL),
        'w_ff2': dense(ks[26], (L, D_FF, D_MODEL), D_FF),
    }


def reference(x_prompt, x_sample, mem_prompt, mem_sample, norm_mix_g, w_in, a_q_norm_g, a_k_norm_g,
              b_cq_norm_g, b_ckv_norm_g, w_uq, w_ukv, b_q_norm_g, b_k_norm_g, a_out_norm_g, b_out_norm_g,
              w_out, norm_mem_g, mem_kv_norm_g, m_wq, m_wkv, m_q_norm_g, m_k_norm_g, m_wo,
              norm_ffn_g, w_ff1, w_ff2):
    params = (norm_mix_g, w_in, a_q_norm_g, a_k_norm_g, b_cq_norm_g, b_ckv_norm_g, w_uq, w_ukv,
              b_q_norm_g, b_k_norm_g, a_out_norm_g, b_out_norm_g, w_out,
              norm_mem_g, mem_kv_norm_g, m_wq, m_wkv, m_q_norm_g, m_k_norm_g, m_wo,
              norm_ffn_g, w_ff1, w_ff2)
    y_prompt = encoder_trunk(x_prompt, mem_prompt, params)
    y_sample = encoder_trunk(x_sample, mem_sample, params)
    return (y_prompt, y_sample)
```

```python
import functools

import numpy as np
import jax
import jax.numpy as jnp
from jax import lax
from jax.experimental import pallas as pl
from jax.experimental.pallas import tpu as pltpu

EPS = 1e-6
ROPE_THETA = 500000.0

LANES = 128
A_HEADS = 8
A_HEAD_DIM = 64
A_WIDTH = A_HEADS * A_HEAD_DIM
A_ROPE_DIM = 16
A_HALF_WINDOW = 64
DILATIONS = (1, 4, 16)

B_HEADS = 8
B_NOPE_DIM = 64
B_ROPE_DIM = 32
B_QK_DIM = B_NOPE_DIM + B_ROPE_DIM
B_V_DIM = 64
B_WIDTH = B_HEADS * B_V_DIM
B_QK_PAD = B_HEADS * LANES
Q_LORA = 256
KV_LORA = 128

M_HEADS = 4
M_HEAD_DIM = 128
M_WIDTH = M_HEADS * M_HEAD_DIM

VMEM_LIMIT = 56 * 1024 * 1024

BF16 = jnp.bfloat16
F32 = jnp.float32
NEG_BIG = -1e30


def _params(n_axes):
    return pltpu.CompilerParams(dimension_semantics=("arbitrary",) * n_axes,
                                vmem_limit_bytes=VMEM_LIMIT)


def _row_rms_scale(x, width):
    return lax.rsqrt(jnp.sum(x * x, axis=-1, keepdims=True) * (1.0 / width) + EPS)


def _dot(a, b):
    return jnp.dot(a, b, preferred_element_type=F32)


def _dot_nt(a, b):
    return lax.dot_general(a, b, (((1,), (1,)), ((), ())), preferred_element_type=F32)


def _rope_chunk(xc, cos, sin_lo, sin_hi, half):
    fwd = pltpu.roll(xc, LANES - half, axis=1)
    bwd = pltpu.roll(xc, half, axis=1)
    return xc * cos + fwd * sin_lo + bwd * sin_hi


def _proj_kernel(x_ref, gmix_ref, win_ref, gqa_ref, gka_ref, gcq_ref, gckv_ref,
                 wuq_ref, wuk_ref, wuv_ref, gqb_ref, gkb_ref, taba_ref, tabb_ref,
                 qa_ref, ka_ref, va_ref, qb_ref, kb_ref, vb_ref):
    x = x_ref[...]
    h = (x * _row_rms_scale(x, x.shape[-1]) * gmix_ref[...]).astype(BF16)
    proj = _dot(h, win_ref[...])

    lane = lax.broadcasted_iota(jnp.int32, (1, LANES), 1)
    low_head = lane < A_HEAD_DIM
    cos_a, sin_lo_a, sin_hi_a = taba_ref[0], taba_ref[1], taba_ref[2]
    cos_b, sin_lo_b, sin_hi_b = tabb_ref[0], tabb_ref[1], tabb_ref[2]

    def a_heads(col0, g_ref, out_ref):
        for c in range(A_WIDTH // LANES):
            xc = proj[:, col0 + c * LANES: col0 + (c + 1) * LANES]
            sq = xc * xc
            s_lo = jnp.sum(jnp.where(low_head, sq, 0.0), axis=-1, keepdims=True)
            s_hi = jnp.sum(jnp.where(low_head, 0.0, sq), axis=-1, keepdims=True)
            rs = jnp.where(low_head,
                           lax.rsqrt(s_lo * (1.0 / A_HEAD_DIM) + EPS),
                           lax.rsqrt(s_hi * (1.0 / A_HEAD_DIM) + EPS))
            y = xc * rs * g_ref[:, c * LANES:(c + 1) * LANES]
            y = _rope_chunk(y, cos_a, sin_lo_a, sin_hi_a, A_ROPE_DIM // 2)
            out_ref[:, c * LANES:(c + 1) * LANES] = y.astype(out_ref.dtype)

    a_heads(0, gqa_ref, qa_ref)
    a_heads(A_WIDTH, gka_ref, ka_ref)
    va_ref[...] = proj[:, 2 * A_WIDTH:3 * A_WIDTH].astype(va_ref.dtype)

    c0 = 3 * A_WIDTH
    cq = proj[:, c0:c0 + Q_LORA]
    ckv = proj[:, c0 + Q_LORA:c0 + Q_LORA + KV_LORA]
    kr = proj[:, c0 + Q_LORA + KV_LORA:c0 + Q_LORA + 2 * KV_LORA]
    cqn = (cq * _row_rms_scale(cq, Q_LORA) * gcq_ref[...]).astype(BF16)
    ckvn = (ckv * _row_rms_scale(ckv, KV_LORA) * gckv_ref[...]).astype(BF16)
    qb = _dot(cqn, wuq_ref[...])
    kn = _dot(ckvn, wuk_ref[...])
    vb_ref[...] = _dot(ckvn, wuv_ref[...]).astype(vb_ref.dtype)

    def b_heads(get_chunk, g_ref, out_ref):
        for c in range(B_HEADS):
            xc = get_chunk(c)
            rs = lax.rsqrt(jnp.sum(xc * xc, axis=-1, keepdims=True) * (1.0 / B_QK_DIM) + EPS)
            y = xc * rs * g_ref[:, c * LANES:(c + 1) * LANES]
            y = _rope_chunk(y, cos_b, sin_lo_b, sin_hi_b, B_ROPE_DIM // 2)
            out_ref[:, c * LANES:(c + 1) * LANES] = y.astype(out_ref.dtype)

    b_heads(lambda c: qb[:, c * LANES:(c + 1) * LANES], gqb_ref, qb_ref)
    b_heads(lambda c: kn[:, c * LANES:(c + 1) * LANES] + kr, gkb_ref, kb_ref)


def _proj(x2d, seq, w, tm):
    n, d = x2d.shape
    nblk_seq = seq // tm
    row = lambda i: (i, 0)
    const = lambda i: (0, 0)
    tab = lambda i: (0, i % nblk_seq, 0)
    full = lambda a: pl.BlockSpec(a.shape, const)
    outs = [(A_WIDTH, BF16), (A_WIDTH, BF16), (A_WIDTH, BF16), (B_QK_PAD, BF16), (B_QK_PAD, BF16), (B_WIDTH, BF16)]
    return pl.pallas_call(
        _proj_kernel,
        grid=(n // tm,),
        in_specs=[pl.BlockSpec((tm, d), row), full(w['gmix']), full(w['win']), full(w['gqa']), full(w['gka']),
                  full(w['gcq']), full(w['gckv']), full(w['wuq']), full(w['wuk']), full(w['wuv']),
                  full(w['gqb']), full(w['gkb']),
                  pl.BlockSpec((3, tm, LANES), tab), pl.BlockSpec((3, tm, LANES), tab)],
        out_specs=[pl.BlockSpec((tm, wd), row) for wd, _ in outs],
        out_shape=[jax.ShapeDtypeStruct((n, wd), dt) for wd, dt in outs],
        compiler_params=_params(1),
        name="proj",
    )(x2d, w['gmix'], w['win'], w['gqa'], w['gka'], w['gcq'], w['gckv'], w['wuq'], w['wuk'], w['wuv'],
      w['gqb'], w['gkb'], w['taba'], w['tabb'])


DIL_TQ = 128
DIL_WIN = DIL_TQ + 2 * A_HALF_WINDOW


def _dilated_kernel(q_ref, k_ref, v_ref, o_ref, lse_ref, *, sub_len):
    j = pl.program_id(2)
    q0 = j * DIL_TQ
    k0 = pl.multiple_of(jnp.clip(q0 - A_HALF_WINDOW, 0, sub_len - DIL_WIN), A_HALF_WINDOW)
    qpos = q0 + lax.broadcasted_iota(jnp.int32, (DIL_TQ, DIL_WIN), 0)
    kpos = k0 + lax.broadcasted_iota(jnp.int32, (DIL_TQ, DIL_WIN), 1)
    in_band = jnp.abs(kpos - qpos) <= A_HALF_WINDOW
    lane = lax.broadcasted_iota(jnp.int32, (1, LANES), 1)
    low_head = lane < A_HEAD_DIM
    for p in range(A_WIDTH // LANES):
        cols = slice(p * LANES, (p + 1) * LANES)
        qp = q_ref[0, :, cols]
        kw = k_ref[0, pl.ds(k0, DIL_WIN), cols]
        vw = v_ref[0, pl.ds(k0, DIL_WIN), cols]
        o_pair = None
        for own in (low_head, jnp.logical_not(low_head)):
            qh = jnp.where(own, qp, jnp.zeros_like(qp))
            s = jnp.where(in_band, _dot_nt(qh, kw), NEG_BIG)
            m = jnp.max(s, axis=-1, keepdims=True)
            pr = jnp.exp(s - m)
            den = jnp.sum(pr, axis=-1, keepdims=True)
            o = _dot(pr.astype(BF16), vw) / den
            lse = jnp.broadcast_to(m + jnp.log(den), o.shape)
            if o_pair is None:
                o_pair, lse_pair = o, lse
            else:
                o_pair = jnp.where(low_head, o_pair, o)
                lse_pair = jnp.where(low_head, lse_pair, lse)
        o_ref[0, :, cols] = o_pair
        lse_ref[0, :, cols] = lse_pair


def _dilated_branch(qa, ka, va, dil):
    b, s, w = qa.shape
    sub_len = s // dil
    view = lambda a: a.reshape(b, sub_len, dil * w)
    q_spec = pl.BlockSpec((1, DIL_TQ, w), lambda bi, r, j: (bi, j, r))
    kv_spec = pl.BlockSpec((1, sub_len, w), lambda bi, r, j: (bi, 0, r))
    o, lse = pl.pallas_call(
        functools.partial(_dilated_kernel, sub_len=sub_len),
        grid=(b, dil, sub_len // DIL_TQ),
        in_specs=[q_spec, kv_spec, kv_spec],
        out_specs=[q_spec, q_spec],
        out_shape=[jax.ShapeDtypeStruct((b, sub_len, dil * w), F32)] * 2,
        compiler_params=_params(3),
        name=f"dilated_d{dil}",
    )(view(qa), view(ka), view(va))
    return o.reshape(b * s, w), lse.reshape(b * s, w)


def _mla_kernel(q_ref, k_ref, v_ref, o_ref, *, tk):
    seq = k_ref.shape[1]
    tq = q_ref.shape[1]
    lane = lax.broadcasted_iota(jnp.int32, (1, LANES), 1)
    outs = []
    for hh in range(2):
        cols = slice(hh * LANES, (hh + 1) * LANES)
        qh = q_ref[0, :, cols]

        def body(t, carry, cols=cols, qh=qh):
            m, l, acc = carry
            start = pl.multiple_of(t * tk, tk)
            kt = k_ref[0, pl.ds(start, tk), cols]
            vt = v_ref[0, pl.ds(start, tk), :]
            s = _dot_nt(qh, kt)
            m_new = jnp.maximum(m, jnp.max(s, axis=-1, keepdims=True))
            alpha = jnp.exp(m - m_new)
            pr = jnp.exp(s - m_new)
            l = alpha * l + jnp.sum(pr, axis=-1, keepdims=True)
            acc = alpha * acc + _dot(pr.astype(BF16), vt)
            return m_new, l, acc

        init = (jnp.full((tq, 1), NEG_BIG, F32), jnp.zeros((tq, 1), F32), jnp.zeros((tq, LANES), F32))
        _, l, acc = lax.fori_loop(0, seq // tk, body, init)
        outs.append(acc / l)
    o_ref[0] = jnp.where(lane < B_V_DIM, outs[0], outs[1])


def _mla(qb, kb, vb, tq, tk):
    b, s, _ = qb.shape
    pairs = B_HEADS // 2
    out = pl.pallas_call(
        functools.partial(_mla_kernel, tk=tk),
        grid=(b, pairs, s // tq),
        in_specs=[pl.BlockSpec((1, tq, 2 * LANES), lambda bi, p, i: (bi, i, p)),
                  pl.BlockSpec((1, s, 2 * LANES), lambda bi, p, i: (bi, 0, p)),
                  pl.BlockSpec((1, s, LANES), lambda bi, p, i: (bi, 0, p))],
        out_specs=pl.BlockSpec((1, tq, LANES), lambda bi, p, i: (bi, i, p)),
        out_shape=jax.ShapeDtypeStruct((b, s, B_WIDTH), F32),
        compiler_params=_params(3),
        name="mla",
    )(qb, kb, vb)
    return out.reshape(b * s, B_WIDTH)


def _out_proj_kernel(x_ref, o1_ref, o4_ref, o16_ref, l1_ref, l4_ref, l16_ref, ob_ref,
                     ga_ref, gb_ref, wa_ref, wb_ref, y_ref):
    l1, l4, l16 = l1_ref[...], l4_ref[...], l16_ref[...]
    top = jnp.maximum(jnp.maximum(l1, l4), l16)
    w1, w4, w16 = jnp.exp(l1 - top), jnp.exp(l4 - top), jnp.exp(l16 - top)
    oa = (w1 * o1_ref[...] + w4 * o4_ref[...] + w16 * o16_ref[...]) / (w1 + w4 + w16)
    ob = ob_ref[...]
    na = (oa * _row_rms_scale(oa, A_WIDTH) * ga_ref[...]).astype(BF16)
    nb = (ob * _row_rms_scale(ob, B_WIDTH) * gb_ref[...]).astype(BF16)
    y_ref[...] = x_ref[...] + _dot(na, wa_ref[...]) + _dot(nb, wb_ref[...])


def _out_proj(x2d, o_branches, lse_branches, ob, w, tm):
    n, d = x2d.shape
    row = lambda i: (i, 0)
    const = lambda i: (0, 0)
    full = lambda a: pl.BlockSpec(a.shape, const)
    half = pl.BlockSpec((tm, A_WIDTH), row)
    return pl.pallas_call(
        _out_proj_kernel,
        grid=(n // tm,),
        in_specs=[pl.BlockSpec((tm, d), row)] + [half] * 7
                 + [full(w['ga_out']), full(w['gb_out']), full(w['wout_a']), full(w['wout_b'])],
        out_specs=pl.BlockSpec((tm, d), row),
        out_shape=jax.ShapeDtypeStruct((n, d), F32),
        compiler_params=_params(1),
        name="out_proj",
    )(x2d, *o_branches, *lse_branches, ob, w['ga_out'], w['gb_out'], w['wout_a'], w['wout_b'])


def _mem_kv_kernel(mem_ref, g_ref, wkv_ref, gk_ref, k_ref, v_ref):
    mem = mem_ref[0]
    mm = (mem * _row_rms_scale(mem, mem.shape[-1]) * g_ref[...]).astype(BF16)
    kv = _dot(mm, wkv_ref[...])
    for hd in range(M_HEADS):
        cols = slice(hd * M_HEAD_DIM, (hd + 1) * M_HEAD_DIM)
        kh = kv[:, cols]
        k_ref[0, :, cols] = (kh * _row_rms_scale(kh, M_HEAD_DIM) * gk_ref[...]).astype(k_ref.dtype)
    v_ref[0] = kv[:, M_WIDTH:].astype(v_ref.dtype)


def _mem_kv(mem, w):
    b, n_mem, d = mem.shape
    const = lambda bi: (0, 0)
    full = lambda a: pl.BlockSpec(a.shape, const)
    blk = pl.BlockSpec((1, n_mem, M_WIDTH), lambda bi: (bi, 0, 0))
    return pl.pallas_call(
        _mem_kv_kernel,
        grid=(b,),
        in_specs=[pl.BlockSpec((1, n_mem, d), lambda bi: (bi, 0, 0)),
                  full(w['gmem_kv']), full(w['wkv_m']), full(w['gk_m'])],
        out_specs=[blk, blk],
        out_shape=[jax.ShapeDtypeStruct((b, n_mem, M_WIDTH), BF16)] * 2,
        compiler_params=_params(1),
        name="mem_kv",
    )(mem, w['gmem_kv'], w['wkv_m'], w['gk_m'])


def _mem_attn_kernel(x_ref, g_ref, wq_ref, gq_ref, k_ref, v_ref, wo_ref, y_ref):
    x = x_ref[0]
    h = (x * _row_rms_scale(x, x.shape[-1]) * g_ref[...]).astype(BF16)
    q = _dot(h, wq_ref[...])
    heads = []
    for hd in range(M_HEADS):
        cols = slice(hd * M_HEAD_DIM, (hd + 1) * M_HEAD_DIM)
        qh = q[:, cols]
        qh = (qh * _row_rms_scale(qh, M_HEAD_DIM) * gq_ref[...]).astype(BF16)
        s = _dot_nt(qh, k_ref[0, :, cols])
        pr = jnp.exp(s - jnp.max(s, axis=-1, keepdims=True))
        pr = pr / jnp.sum(pr, axis=-1, keepdims=True)
        heads.append(_dot(pr.astype(BF16), v_ref[0, :, cols]).astype(BF16))
    om = jnp.concatenate(heads, axis=-1)
    y_ref[0] = x + _dot(om, wo_ref[...])


def _mem_attn(x3d, km, vm, w, tm):
    b, s, d = x3d.shape
    n_mem = km.shape[1]
    const = lambda bi, i: (0, 0)
    full = lambda a: pl.BlockSpec(a.shape, const)
    xblk = pl.BlockSpec((1, tm, d), lambda bi, i: (bi, i, 0))
    kvblk = pl.BlockSpec((1, n_mem, M_WIDTH), lambda bi, i: (bi, 0, 0))
    return pl.pallas_call(
        _mem_attn_kernel,
        grid=(b, s // tm),
        in_specs=[xblk, full(w['gmem']), full(w['wq_m']), full(w['gq_m']), kvblk, kvblk, full(w['wo_m'])],
        out_specs=xblk,
        out_shape=jax.ShapeDtypeStruct((b, s, d), F32),
        compiler_params=_params(2),
        name="mem_attn",
    )(x3d, w['gmem'], w['wq_m'], w['gq_m'], km, vm, w['wo_m'])


FFN_CHUNK = 1024


def _ffn_kernel(x_ref, g_ref, w1_ref, w2_ref, y_ref):
    x = x_ref[...]
    h = (x * _row_rms_scale(x, x.shape[-1]) * g_ref[...]).astype(BF16)
    acc = x
    for c in range(w1_ref.shape[1] // FFN_CHUNK):
        cols = slice(c * FFN_CHUNK, (c + 1) * FFN_CHUNK)
        a = jnp.maximum(_dot(h, w1_ref[:, cols]), 0.0)
        acc = acc + _dot((a * a).astype(BF16), w2_ref[cols, :])
    y_ref[...] = acc


def _ffn(x2d, w, tm):
    n, d = x2d.shape
    row = lambda i: (i, 0)
    const = lambda i: (0, 0)
    full = lambda a: pl.BlockSpec(a.shape, const)
    return pl.pallas_call(
        _ffn_kernel,
        grid=(n // tm,),
        in_specs=[pl.BlockSpec((tm, d), row), full(w['gffn']), full(w['w1']), full(w['w2'])],
        out_specs=pl.BlockSpec((tm, d), row),
        out_shape=jax.ShapeDtypeStruct((n, d), F32),
        compiler_params=_params(1),
        name="ffn",
    )(x2d, w['gffn'], w['w1'], w['w2'])


def _rope_tables(seq, rope_dim, first_lane, period):
    half = rope_dim // 2
    inv = ROPE_THETA ** (-jnp.arange(0, rope_dim, 2, dtype=F32) / rope_dim)
    ang = jnp.arange(seq, dtype=F32)[:, None] * inv[None, :]
    cos, sin = jnp.cos(ang), jnp.sin(ang)
    lane = np.arange(LANES) % period
    rel = lane - first_lane
    is_lo = (rel >= 0) & (rel < half)
    is_hi = (rel >= half) & (rel < rope_dim)
    idx = np.clip(np.where(is_hi, rel - half, rel), 0, half - 1)
    cos_t = jnp.where(is_lo | is_hi, cos[:, idx], 1.0)
    sin_lo = jnp.where(is_lo, -sin[:, idx], 0.0)
    sin_hi = jnp.where(is_hi, sin[:, idx], 0.0)
    return jnp.stack([cos_t, sin_lo, sin_hi]).astype(F32)


def _layer_weights(l, p):
    row = lambda v: v.reshape(1, -1).astype(F32)
    w_in = p['w_in'][l]
    d = w_in.shape[0]
    c_cq, c_ckv, c_kr = 3 * A_WIDTH, 3 * A_WIDTH + Q_LORA, 3 * A_WIDTH + Q_LORA + KV_LORA
    win = jnp.concatenate([
        w_in[:, :c_kr],
        jnp.zeros((d, B_NOPE_DIM), F32), w_in[:, c_kr:], jnp.zeros((d, LANES - B_QK_DIM), F32)], axis=1)
    pad_qk = lambda a: jnp.pad(a, ((0, 0), (0, 0), (0, LANES - a.shape[-1]))).reshape(a.shape[0], B_QK_PAD)
    wukv = p['w_ukv'][l].reshape(KV_LORA, B_HEADS, B_NOPE_DIM + B_V_DIM)
    pad_gain = lambda g: jnp.tile(jnp.pad(g, (0, LANES - B_QK_DIM)), B_HEADS)
    w_out = p['w_out'][l]
    return {
        'gmix': row(p['norm_mix_g'][l]),
        'win': win.astype(BF16),
        'gqa': row(jnp.tile(p['a_q_norm_g'][l], A_HEADS) * (A_HEAD_DIM ** -0.5)),
        'gka': row(jnp.tile(p['a_k_norm_g'][l], A_HEADS)),
        'gcq': row(p['b_cq_norm_g'][l]),
        'gckv': row(p['b_ckv_norm_g'][l]),
        'wuq': pad_qk(p['w_uq'][l].reshape(Q_LORA, B_HEADS, B_QK_DIM)).astype(BF16),
        'wuk': pad_qk(wukv[:, :, :B_NOPE_DIM]).astype(BF16),
        'wuv': wukv[:, :, B_NOPE_DIM:].reshape(KV_LORA, B_WIDTH).astype(BF16),
        'gqb': row(pad_gain(p['b_q_norm_g'][l]) * (B_QK_DIM ** -0.5)),
        'gkb': row(pad_gain(p['b_k_norm_g'][l])),
        'ga_out': row(p['a_out_norm_g'][l]),
        'gb_out': row(p['b_out_norm_g'][l]),
        'wout_a': w_out[:A_WIDTH].astype(BF16),
        'wout_b': w_out[A_WIDTH:].astype(BF16),
        'gmem': row(p['norm_mem_g'][l]),
        'gmem_kv': row(p['mem_kv_norm_g'][l]),
        'wq_m': p['m_wq'][l].astype(BF16),
        'wkv_m': p['m_wkv'][l].astype(BF16),
        'gq_m': row(p['m_q_norm_g'][l] * (M_HEAD_DIM ** -0.5)),
        'gk_m': row(p['m_k_norm_g'][l]),
        'wo_m': p['m_wo'][l].astype(BF16),
        'gffn': row(p['norm_ffn_g'][l]),
        'w1': p['w_ff1'][l].astype(BF16),
        'w2': p['w_ff2'][l].astype(BF16),
    }


def _tiles(seq):
    tm = 512 if seq % 512 == 0 else 256
    return dict(tm=tm, tq=256, tk=512)


def _trunk(x, mem, layers):
    b, s, d = x.shape
    assert s % (max(DILATIONS) * DIL_TQ) == 0 and s // max(DILATIONS) >= DIL_WIN
    t = _tiles(s)
    taba = _rope_tables(s, A_ROPE_DIM, 0, A_HEAD_DIM)
    tabb = _rope_tables(s, B_ROPE_DIM, B_NOPE_DIM, LANES)
    x2d = x.reshape(b * s, d)
    for w in layers:
        w = dict(w, taba=taba, tabb=tabb)
        qa, ka, va, qb, kb, vb = _proj(x2d, s, w, t['tm'])
        as3d = lambda a: a.reshape(b, s, a.shape[-1])
        branches = [_dilated_branch(as3d(qa), as3d(ka), as3d(va), dil) for dil in DILATIONS]
        ob = _mla(as3d(qb), as3d(kb), as3d(vb), t['tq'], t['tk'])
        x2d = _out_proj(x2d, [o for o, _ in branches], [m for _, m in branches], ob, w, t['tm'])
        km, vm = _mem_kv(mem, w)
        x2d = _mem_attn(x2d.reshape(b, s, d), km, vm, w, t['tm']).reshape(b * s, d)
        x2d = _ffn(x2d, w, t['tm'])
    return x2d.reshape(b, s, d)


def kernel(x_prompt, x_sample, mem_prompt, mem_sample, norm_mix_g, w_in, a_q_norm_g, a_k_norm_g, b_cq_norm_g, b_ckv_norm_g, w_uq, w_ukv, b_q_norm_g, b_k_norm_g, a_out_norm_g, b_out_norm_g, w_out, norm_mem_g, mem_kv_norm_g, m_wq, m_wkv, m_q_norm_g, m_k_norm_g, m_wo, norm_ffn_g, w_ff1, w_ff2):
    p = dict(norm_mix_g=norm_mix_g, w_in=w_in, a_q_norm_g=a_q_norm_g, a_k_norm_g=a_k_norm_g,
             b_cq_norm_g=b_cq_norm_g, b_ckv_norm_g=b_ckv_norm_g, w_uq=w_uq, w_ukv=w_ukv,
             b_q_norm_g=b_q_norm_g, b_k_norm_g=b_k_norm_g, a_out_norm_g=a_out_norm_g,
             b_out_norm_g=b_out_norm_g, w_out=w_out, norm_mem_g=norm_mem_g, mem_kv_norm_g=mem_kv_norm_g,
             m_wq=m_wq, m_wkv=m_wkv, m_q_norm_g=m_q_norm_g, m_k_norm_g=m_k_norm_g, m_wo=m_wo,
             norm_ffn_g=norm_ffn_g, w_ff1=w_ff1, w_ff2=w_ff2)
    layers = [_layer_weights(l, p) for l in range(w_in.shape[0])]
    return (_trunk(x_prompt, mem_prompt, layers), _trunk(x_sample, mem_sample, layers))
```

```python
import functools

import numpy as np
import jax
import jax.numpy as jnp
from jax import lax
from jax.experimental import pallas as pl
from jax.experimental.pallas import tpu as pltpu

EPS = 1e-6
ROPE_THETA = 500000.0

LANES = 128
A_HEADS = 8
A_HEAD_DIM = 64
A_WIDTH = A_HEADS * A_HEAD_DIM
A_ROPE_DIM = 16
A_HALF_WINDOW = 64
DILATIONS = (1, 4, 16)

B_HEADS = 8
B_NOPE_DIM = 64
B_ROPE_DIM = 32
B_QK_DIM = B_NOPE_DIM + B_ROPE_DIM
B_V_DIM = 64
B_WIDTH = B_HEADS * B_V_DIM
B_QK_PAD = B_HEADS * LANES
Q_LORA = 256
KV_LORA = 128

M_HEADS = 4
M_HEAD_DIM = 128
M_WIDTH = M_HEADS * M_HEAD_DIM

VMEM_LIMIT = 56 * 1024 * 1024

BF16 = jnp.bfloat16
F32 = jnp.float32
NEG_BIG = -1e30
LOG2_E = float(np.log2(np.e))


def _params(n_axes):
    return pltpu.CompilerParams(dimension_semantics=("arbitrary",) * n_axes,
                                vmem_limit_bytes=VMEM_LIMIT)


def _row_rms_scale(x, width):
    return lax.rsqrt(jnp.sum(x * x, axis=-1, keepdims=True) * (1.0 / width) + EPS)


def _dot(a, b):
    return jnp.dot(a, b, preferred_element_type=F32)


def _dot_nt(a, b):
    return lax.dot_general(a, b, (((1,), (1,)), ((), ())), preferred_element_type=F32)


def _rope_chunk(xc, cos, sin_lo, sin_hi, half):
    fwd = pltpu.roll(xc, LANES - half, axis=1)
    bwd = pltpu.roll(xc, half, axis=1)
    return xc * cos + fwd * sin_lo + bwd * sin_hi


def _proj_kernel(x_ref, gmix_ref, win_ref, gqa_ref, gka_ref, gcq_ref, gckv_ref,
                 wuq_ref, wuk_ref, wuv_ref, gqb_ref, gkb_ref, taba_ref, tabb_ref,
                 qa_ref, ka_ref, va_ref, qb_ref, kb_ref, vbt_ref):
    x = x_ref[...]
    h = (x * _row_rms_scale(x, x.shape[-1]) * gmix_ref[...]).astype(BF16)
    proj = _dot(h, win_ref[...])

    lane = lax.broadcasted_iota(jnp.int32, (1, LANES), 1)
    low_head = lane < A_HEAD_DIM
    cos_a, sin_lo_a, sin_hi_a = taba_ref[0], taba_ref[1], taba_ref[2]
    cos_b, sin_lo_b, sin_hi_b = tabb_ref[0], tabb_ref[1], tabb_ref[2]

    def a_heads(col0, g_ref, out_ref):
        for c in range(A_WIDTH // LANES):
            xc = proj[:, col0 + c * LANES: col0 + (c + 1) * LANES]
            sq = xc * xc
            s_lo = jnp.sum(jnp.where(low_head, sq, 0.0), axis=-1, keepdims=True)
            s_hi = jnp.sum(jnp.where(low_head, 0.0, sq), axis=-1, keepdims=True)
            rs = jnp.where(low_head,
                           lax.rsqrt(s_lo * (1.0 / A_HEAD_DIM) + EPS),
                           lax.rsqrt(s_hi * (1.0 / A_HEAD_DIM) + EPS))
            y = xc * rs * g_ref[:, c * LANES:(c + 1) * LANES]
            y = _rope_chunk(y, cos_a, sin_lo_a, sin_hi_a, A_ROPE_DIM // 2)
            out_ref[:, c * LANES:(c + 1) * LANES] = y.astype(out_ref.dtype)

    a_heads(0, gqa_ref, qa_ref)
    a_heads(A_WIDTH, gka_ref, ka_ref)
    va_ref[...] = proj[:, 2 * A_WIDTH:3 * A_WIDTH].astype(va_ref.dtype)

    c0 = 3 * A_WIDTH
    cq = proj[:, c0:c0 + Q_LORA]
    ckv = proj[:, c0 + Q_LORA:c0 + Q_LORA + KV_LORA]
    kr = proj[:, c0 + Q_LORA + KV_LORA:c0 + Q_LORA + 2 * KV_LORA]
    cqn = (cq * _row_rms_scale(cq, Q_LORA) * gcq_ref[...]).astype(BF16)
    ckvn = (ckv * _row_rms_scale(ckv, KV_LORA) * gckv_ref[...]).astype(BF16)
    qb = _dot(cqn, wuq_ref[...])
    kn = _dot(ckvn, wuk_ref[...])
    vbt_ref[...] = _dot(ckvn, wuv_ref[...]).T.astype(vbt_ref.dtype)

    def b_heads(get_chunk, g_ref, out_ref):
        for c in range(B_HEADS):
            xc = get_chunk(c)
            rs = lax.rsqrt(jnp.sum(xc * xc, axis=-1, keepdims=True) * (1.0 / B_QK_DIM) + EPS)
            y = xc * rs * g_ref[:, c * LANES:(c + 1) * LANES]
            y = _rope_chunk(y, cos_b, sin_lo_b, sin_hi_b, B_ROPE_DIM // 2)
            out_ref[:, c * LANES:(c + 1) * LANES] = y.astype(out_ref.dtype)

    b_heads(lambda c: qb[:, c * LANES:(c + 1) * LANES], gqb_ref, qb_ref)
    b_heads(lambda c: kn[:, c * LANES:(c + 1) * LANES] + kr, gkb_ref, kb_ref)


def _proj(x2d, seq, w, tm):
    n, d = x2d.shape
    nblk_seq = seq // tm
    row = lambda i: (i, 0)
    const = lambda i: (0, 0)
    tab = lambda i: (0, i % nblk_seq, 0)
    full = lambda a: pl.BlockSpec(a.shape, const)
    widths = [A_WIDTH, A_WIDTH, A_WIDTH, B_QK_PAD, B_QK_PAD]
    return pl.pallas_call(
        _proj_kernel,
        grid=(n // tm,),
        in_specs=[pl.BlockSpec((tm, d), row), full(w['gmix']), full(w['win']), full(w['gqa']), full(w['gka']),
                  full(w['gcq']), full(w['gckv']), full(w['wuq']), full(w['wuk']), full(w['wuv']),
                  full(w['gqb']), full(w['gkb']),
                  pl.BlockSpec((3, tm, LANES), tab), pl.BlockSpec((3, tm, LANES), tab)],
        out_specs=[pl.BlockSpec((tm, wd), row) for wd in widths]
                  + [pl.BlockSpec((B_WIDTH, tm), lambda i: (0, i))],
        out_shape=[jax.ShapeDtypeStruct((n, wd), BF16) for wd in widths]
                  + [jax.ShapeDtypeStruct((B_WIDTH, n), BF16)],
        compiler_params=_params(1),
        name="proj",
    )(x2d, w['gmix'], w['win'], w['gqa'], w['gka'], w['gcq'], w['gckv'], w['wuq'], w['wuk'], w['wuv'],
      w['gqb'], w['gkb'], w['taba'], w['tabb'])


DIL_TQ = 128
DIL_WIN = DIL_TQ + 2 * A_HALF_WINDOW


def _dilated_kernel(q_ref, k_ref, v_ref, o_ref, lse_ref, *, sub_len):
    j = pl.program_id(2)
    q0 = j * DIL_TQ
    k0 = pl.multiple_of(jnp.clip(q0 - A_HALF_WINDOW, 0, sub_len - DIL_WIN), A_HALF_WINDOW)
    qpos = q0 + lax.broadcasted_iota(jnp.int32, (DIL_TQ, DIL_WIN), 0)
    kpos = k0 + lax.broadcasted_iota(jnp.int32, (DIL_TQ, DIL_WIN), 1)
    in_band = jnp.abs(kpos - qpos) <= A_HALF_WINDOW
    lane = lax.broadcasted_iota(jnp.int32, (1, LANES), 1)
    low_head = lane < A_HEAD_DIM
    for p in range(A_WIDTH // LANES):
        cols = slice(p * LANES, (p + 1) * LANES)
        qp = q_ref[0, :, cols]
        kw = k_ref[0, pl.ds(k0, DIL_WIN), cols]
        vw = v_ref[0, pl.ds(k0, DIL_WIN), cols]
        o_pair = None
        for own in (low_head, jnp.logical_not(low_head)):
            qh = jnp.where(own, qp, jnp.zeros_like(qp))
            s = jnp.where(in_band, _dot_nt(qh, kw), NEG_BIG)
            m = jnp.max(s, axis=-1, keepdims=True)
            pr = jnp.exp(s - m)
            den = jnp.sum(pr, axis=-1, keepdims=True)
            o = _dot(pr.astype(BF16), vw) / den
            lse = jnp.broadcast_to(m + jnp.log(den), o.shape)
            if o_pair is None:
                o_pair, lse_pair = o, lse
            else:
                o_pair = jnp.where(low_head, o_pair, o)
                lse_pair = jnp.where(low_head, lse_pair, lse)
        o_ref[0, :, cols] = o_pair
        lse_ref[0, :, cols] = lse_pair


def _dilated_branch(qa, ka, va, dil):
    b, s, w = qa.shape
    sub_len = s // dil
    view = lambda a: a.reshape(b, sub_len, dil * w)
    q_spec = pl.BlockSpec((1, DIL_TQ, w), lambda bi, r, j: (bi, j, r))
    kv_spec = pl.BlockSpec((1, sub_len, w), lambda bi, r, j: (bi, 0, r))
    o, lse = pl.pallas_call(
        functools.partial(_dilated_kernel, sub_len=sub_len),
        grid=(b, dil, sub_len // DIL_TQ),
        in_specs=[q_spec, kv_spec, kv_spec],
        out_specs=[q_spec, q_spec],
        out_shape=[jax.ShapeDtypeStruct((b, sub_len, dil * w), F32)] * 2,
        compiler_params=_params(3),
        name=f"dilated_d{dil}",
    )(view(qa), view(ka), view(va))
    return o.reshape(b * s, w), lse.reshape(b * s, w)


MLA_DEN_ROWS = 16


def _mla_kernel(q_ref, k_ref, vt_ref, o_ref, s_scr, *, tk):
    seq = k_ref.shape[1]
    tq = q_ref.shape[1]
    heads = range(2)
    n_blocks = seq // tk
    assert seq % (2 * tk) == 0

    def scores_into(slot, blk):
        start = pl.multiple_of(blk * tk, tk)
        for hh in heads:
            cols = slice(hh * LANES, (hh + 1) * LANES)
            s_scr[slot, hh] = _dot_nt(k_ref[0, pl.ds(start, tk), cols], q_ref[0, :, cols])

    scores_into(0, 0)

    def body(t, carry):
        state = list(carry)
        for slot in range(2):
            blk = 2 * t + slot
            scores_into(1 - slot, jnp.where(blk + 1 == n_blocks, 0, blk + 1))
            start = pl.multiple_of(blk * tk, tk)
            for hh in heads:
                m, acc = state[hh]
                m_new = jnp.maximum(m, jnp.max(s_scr[slot, hh], axis=0, keepdims=True))
                alpha = jnp.exp2(m - m_new)
                pt = jnp.exp2(s_scr[slot, hh] - m_new).astype(BF16)
                vt1 = jnp.concatenate([vt_ref[hh * B_V_DIM:(hh + 1) * B_V_DIM, pl.ds(start, tk)],
                                       jnp.ones((MLA_DEN_ROWS, tk), BF16)], axis=0)
                acc = alpha * acc + _dot(vt1, pt)
                state[hh] = (m_new, acc)
        return tuple(state)

    init = tuple((jnp.full((1, tq), NEG_BIG, F32), jnp.zeros((B_V_DIM + MLA_DEN_ROWS, tq), F32)) for _ in heads)
    final = lax.fori_loop(0, n_blocks // 2, body, init)
    ot = jnp.concatenate([acc[:B_V_DIM] / acc[B_V_DIM:B_V_DIM + 1] for _, acc in final], axis=0)
    o_ref[0] = ot.T


def _mla(qb, kb, vbt, tq, tk):
    b, s, _ = qb.shape
    pairs = B_HEADS // 2
    out = pl.pallas_call(
        functools.partial(_mla_kernel, tk=tk),
        grid=(b, pairs, s // tq),
        in_specs=[pl.BlockSpec((1, tq, 2 * LANES), lambda bi, p, i: (bi, i, p)),
                  pl.BlockSpec((1, s, 2 * LANES), lambda bi, p, i: (bi, 0, p)),
                  pl.BlockSpec((LANES, s), lambda bi, p, i: (p, bi))],
        out_specs=pl.BlockSpec((1, tq, LANES), lambda bi, p, i: (bi, i, p)),
        out_shape=jax.ShapeDtypeStruct((b, s, B_WIDTH), F32),
        scratch_shapes=[pltpu.VMEM((2, 2, tk, tq), F32)],
        compiler_params=_params(3),
        name="mla",
    )(qb, kb, vbt)
    return out.reshape(b * s, B_WIDTH)


def _out_proj_kernel(x_ref, o1_ref, o4_ref, o16_ref, l1_ref, l4_ref, l16_ref, ob_ref,
                     ga_ref, gb_ref, wa_ref, wb_ref, y_ref):
    l1, l4, l16 = l1_ref[...], l4_ref[...], l16_ref[...]
    top = jnp.maximum(jnp.maximum(l1, l4), l16)
    w1, w4, w16 = jnp.exp(l1 - top), jnp.exp(l4 - top), jnp.exp(l16 - top)
    oa = (w1 * o1_ref[...] + w4 * o4_ref[...] + w16 * o16_ref[...]) / (w1 + w4 + w16)
    ob = ob_ref[...]
    na = (oa * _row_rms_scale(oa, A_WIDTH) * ga_ref[...]).astype(BF16)
    nb = (ob * _row_rms_scale(ob, B_WIDTH) * gb_ref[...]).astype(BF16)
    y_ref[...] = x_ref[...] + _dot(na, wa_ref[...]) + _dot(nb, wb_ref[...])


def _out_proj(x2d, o_branches, lse_branches, ob, w, tm):
    n, d = x2d.shape
    row = lambda i: (i, 0)
    const = lambda i: (0, 0)
    full = lambda a: pl.BlockSpec(a.shape, const)
    half = pl.BlockSpec((tm, A_WIDTH), row)
    return pl.pallas_call(
        _out_proj_kernel,
        grid=(n // tm,),
        in_specs=[pl.BlockSpec((tm, d), row)] + [half] * 7
                 + [full(w['ga_out']), full(w['gb_out']), full(w['wout_a']), full(w['wout_b'])],
        out_specs=pl.BlockSpec((tm, d), row),
        out_shape=jax.ShapeDtypeStruct((n, d), F32),
        compiler_params=_params(1),
        name="out_proj",
    )(x2d, *o_branches, *lse_branches, ob, w['ga_out'], w['gb_out'], w['wout_a'], w['wout_b'])


def _mem_kv_kernel(mem_ref, g_ref, wkv_ref, gk_ref, k_ref, v_ref):
    mem = mem_ref[0]
    mm = (mem * _row_rms_scale(mem, mem.shape[-1]) * g_ref[...]).astype(BF16)
    kv = _dot(mm, wkv_ref[...])
    for hd in range(M_HEADS):
        cols = slice(hd * M_HEAD_DIM, (hd + 1) * M_HEAD_DIM)
        kh = kv[:, cols]
        k_ref[0, :, cols] = (kh * _row_rms_scale(kh, M_HEAD_DIM) * gk_ref[...]).astype(k_ref.dtype)
    v_ref[0] = kv[:, M_WIDTH:].astype(v_ref.dtype)


def _mem_kv(mem, w):
    b, n_mem, d = mem.shape
    const = lambda bi: (0, 0)
    full = lambda a: pl.BlockSpec(a.shape, const)
    blk = pl.BlockSpec((1, n_mem, M_WIDTH), lambda bi: (bi, 0, 0))
    return pl.pallas_call(
        _mem_kv_kernel,
        grid=(b,),
        in_specs=[pl.BlockSpec((1, n_mem, d), lambda bi: (bi, 0, 0)),
                  full(w['gmem_kv']), full(w['wkv_m']), full(w['gk_m'])],
        out_specs=[blk, blk],
        out_shape=[jax.ShapeDtypeStruct((b, n_mem, M_WIDTH), BF16)] * 2,
        compiler_params=_params(1),
        name="mem_kv",
    )(mem, w['gmem_kv'], w['wkv_m'], w['gk_m'])


def _mem_attn_kernel(x_ref, g_ref, wq_ref, gq_ref, k_ref, v_ref, wo_ref, y_ref):
    x = x_ref[0]
    h = (x * _row_rms_scale(x, x.shape[-1]) * g_ref[...]).astype(BF16)
    q = _dot(h, wq_ref[...])
    heads = []
    for hd in range(M_HEADS):
        cols = slice(hd * M_HEAD_DIM, (hd + 1) * M_HEAD_DIM)
        qh = q[:, cols]
        qh = (qh * _row_rms_scale(qh, M_HEAD_DIM) * gq_ref[...]).astype(BF16)
        s = _dot_nt(qh, k_ref[0, :, cols])
        pr = jnp.exp(s - jnp.max(s, axis=-1, keepdims=True))
        pr = pr / jnp.sum(pr, axis=-1, keepdims=True)
        heads.append(_dot(pr.astype(BF16), v_ref[0, :, cols]).astype(BF16))
    om = jnp.concatenate(heads, axis=-1)
    y_ref[0] = x + _dot(om, wo_ref[...])


def _mem_attn(x3d, km, vm, w, tm):
    b, s, d = x3d.shape
    n_mem = km.shape[1]
    const = lambda bi, i: (0, 0)
    full = lambda a: pl.BlockSpec(a.shape, const)
    xblk = pl.BlockSpec((1, tm, d), lambda bi, i: (bi, i, 0))
    kvblk = pl.BlockSpec((1, n_mem, M_WIDTH), lambda bi, i: (bi, 0, 0))
    return pl.pallas_call(
        _mem_attn_kernel,
        grid=(b, s // tm),
        in_specs=[xblk, full(w['gmem']), full(w['wq_m']), full(w['gq_m']), kvblk, kvblk, full(w['wo_m'])],
        out_specs=xblk,
        out_shape=jax.ShapeDtypeStruct((b, s, d), F32),
        compiler_params=_params(2),
        name="mem_attn",
    )(x3d, w['gmem'], w['wq_m'], w['gq_m'], km, vm, w['wo_m'])


FFN_CHUNK = 1024


def _ffn_kernel(x_ref, g_ref, w1_ref, w2_ref, y_ref):
    x = x_ref[...]
    h = (x * _row_rms_scale(x, x.shape[-1]) * g_ref[...]).astype(BF16)
    acc = x
    for c in range(w1_ref.shape[1] // FFN_CHUNK):
        cols = slice(c * FFN_CHUNK, (c + 1) * FFN_CHUNK)
        a = jnp.maximum(_dot(h, w1_ref[:, cols]), 0.0)
        acc = acc + _dot((a * a).astype(BF16), w2_ref[cols, :])
    y_ref[...] = acc


def _ffn(x2d, w, tm):
    n, d = x2d.shape
    row = lambda i: (i, 0)
    const = lambda i: (0, 0)
    full = lambda a: pl.BlockSpec(a.shape, const)
    return pl.pallas_call(
        _ffn_kernel,
        grid=(n // tm,),
        in_specs=[pl.BlockSpec((tm, d), row), full(w['gffn']), full(w['w1']), full(w['w2'])],
        out_specs=pl.BlockSpec((tm, d), row),
        out_shape=jax.ShapeDtypeStruct((n, d), F32),
        compiler_params=_params(1),
        name="ffn",
    )(x2d, w['gffn'], w['w1'], w['w2'])


def _rope_tables(seq, rope_dim, first_lane, period):
    half = rope_dim // 2
    inv = ROPE_THETA ** (-jnp.arange(0, rope_dim, 2, dtype=F32) / rope_dim)
    ang = jnp.arange(seq, dtype=F32)[:, None] * inv[None, :]
    cos, sin = jnp.cos(ang), jnp.sin(ang)
    lane = np.arange(LANES) % period
    rel = lane - first_lane
    is_lo = (rel >= 0) & (rel < half)
    is_hi = (rel >= half) & (rel < rope_dim)
    idx = np.clip(np.where(is_hi, rel - half, rel), 0, half - 1)
    cos_t = jnp.where(is_lo | is_hi, cos[:, idx], 1.0)
    sin_lo = jnp.where(is_lo, -sin[:, idx], 0.0)
    sin_hi = jnp.where(is_hi, sin[:, idx], 0.0)
    return jnp.stack([cos_t, sin_lo, sin_hi]).astype(F32)


def _layer_weights(l, p):
    row = lambda v: v.reshape(1, -1).astype(F32)
    w_in = p['w_in'][l]
    d = w_in.shape[0]
    c_cq, c_ckv, c_kr = 3 * A_WIDTH, 3 * A_WIDTH + Q_LORA, 3 * A_WIDTH + Q_LORA + KV_LORA
    win = jnp.concatenate([
        w_in[:, :c_kr],
        jnp.zeros((d, B_NOPE_DIM), F32), w_in[:, c_kr:], jnp.zeros((d, LANES - B_QK_DIM), F32)], axis=1)
    pad_qk = lambda a: jnp.pad(a, ((0, 0), (0, 0), (0, LANES - a.shape[-1]))).reshape(a.shape[0], B_QK_PAD)
    wukv = p['w_ukv'][l].reshape(KV_LORA, B_HEADS, B_NOPE_DIM + B_V_DIM)
    pad_gain = lambda g: jnp.tile(jnp.pad(g, (0, LANES - B_QK_DIM)), B_HEADS)
    w_out = p['w_out'][l]
    return {
        'gmix': row(p['norm_mix_g'][l]),
        'win': win.astype(BF16),
        'gqa': row(jnp.tile(p['a_q_norm_g'][l], A_HEADS) * (A_HEAD_DIM ** -0.5)),
        'gka': row(jnp.tile(p['a_k_norm_g'][l], A_HEADS)),
        'gcq': row(p['b_cq_norm_g'][l]),
        'gckv': row(p['b_ckv_norm_g'][l]),
        'wuq': pad_qk(p['w_uq'][l].reshape(Q_LORA, B_HEADS, B_QK_DIM)).astype(BF16),
        'wuk': pad_qk(wukv[:, :, :B_NOPE_DIM]).astype(BF16),
        'wuv': wukv[:, :, B_NOPE_DIM:].reshape(KV_LORA, B_WIDTH).astype(BF16),
        'gqb': row(pad_gain(p['b_q_norm_g'][l]) * (B_QK_DIM ** -0.5 * LOG2_E)),
        'gkb': row(pad_gain(p['b_k_norm_g'][l])),
        'ga_out': row(p['a_out_norm_g'][l]),
        'gb_out': row(p['b_out_norm_g'][l]),
        'wout_a': w_out[:A_WIDTH].astype(BF16),
        'wout_b': w_out[A_WIDTH:].astype(BF16),
        'gmem': row(p['norm_mem_g'][l]),
        'gmem_kv': row(p['mem_kv_norm_g'][l]),
        'wq_m': p['m_wq'][l].astype(BF16),
        'wkv_m': p['m_wkv'][l].astype(BF16),
        'gq_m': row(p['m_q_norm_g'][l] * (M_HEAD_DIM ** -0.5)),
        'gk_m': row(p['m_k_norm_g'][l]),
        'wo_m': p['m_wo'][l].astype(BF16),
        'gffn': row(p['norm_ffn_g'][l]),
        'w1': p['w_ff1'][l].astype(BF16),
        'w2': p['w_ff2'][l].astype(BF16),
    }


def _tiles(seq):
    tm = 512 if seq % 512 == 0 else 256
    return dict(tm=tm, tq=256, tk=512)


def _trunk(x, mem, layers):
    b, s, d = x.shape
    assert s % (max(DILATIONS) * DIL_TQ) == 0 and s // max(DILATIONS) >= DIL_WIN
    t = _tiles(s)
    taba = _rope_tables(s, A_ROPE_DIM, 0, A_HEAD_DIM)
    tabb = _rope_tables(s, B_ROPE_DIM, B_NOPE_DIM, LANES)
    x2d = x.reshape(b * s, d)
    for w in layers:
        w = dict(w, taba=taba, tabb=tabb)
        qa, ka, va, qb, kb, vbt = _proj(x2d, s, w, t['tm'])
        as3d = lambda a: a.reshape(b, s, a.shape[-1])
        branches = [_dilated_branch(as3d(qa), as3d(ka), as3d(va), dil) for dil in DILATIONS]
        ob = _mla(as3d(qb), as3d(kb), vbt, t['tq'], t['tk'])
        x2d = _out_proj(x2d, [o for o, _ in branches], [m for _, m in branches], ob, w, t['tm'])
        km, vm = _mem_kv(mem, w)
        x2d = _mem_attn(x2d.reshape(b, s, d), km, vm, w, t['tm']).reshape(b * s, d)
        x2d = _ffn(x2d, w, t['tm'])
    return x2d.reshape(b, s, d)


def kernel(x_prompt, x_sample, mem_prompt, mem_sample, norm_mix_g, w_in, a_q_norm_g, a_k_norm_g, b_cq_norm_g, b_ckv_norm_g, w_uq, w_ukv, b_q_norm_g, b_k_norm_g, a_out_norm_g, b_out_norm_g, w_out, norm_mem_g, mem_kv_norm_g, m_wq, m_wkv, m_q_norm_g, m_k_norm_g, m_wo, norm_ffn_g, w_ff1, w_ff2):
    p = dict(norm_mix_g=norm_mix_g, w_in=w_in, a_q_norm_g=a_q_norm_g, a_k_norm_g=a_k_norm_g,
             b_cq_norm_g=b_cq_norm_g, b_ckv_norm_g=b_ckv_norm_g, w_uq=w_uq, w_ukv=w_ukv,
             b_q_norm_g=b_q_norm_g, b_k_norm_g=b_k_norm_g, a_out_norm_g=a_out_norm_g,
             b_out_norm_g=b_out_norm_g, w_out=w_out, norm_mem_g=norm_mem_g, mem_kv_norm_g=mem_kv_norm_g,
             m_wq=m_wq, m_wkv=m_wkv, m_q_norm_g=m_q_norm_g, m_k_norm_g=m_k_norm_g, m_wo=m_wo,
             norm_ffn_g=norm_ffn_g, w_ff1=w_ff1, w_ff2=w_ff2)
    layers = [_layer_weights(l, p) for l in range(w_in.shape[0])]
    return (_trunk(x_prompt, mem_prompt, layers), _trunk(x_sample, mem_sample, layers))
```

```python
import functools

import numpy as np
import jax
import jax.numpy as jnp
from jax import lax
from jax.experimental import pallas as pl
from jax.experimental.pallas import tpu as pltpu

EPS = 1e-6
ROPE_THETA = 500000.0

LANES = 128
A_HEADS = 8
A_HEAD_DIM = 64
A_WIDTH = A_HEADS * A_HEAD_DIM
A_ROPE_DIM = 16
A_HALF_WINDOW = 64
DILATIONS = (16, 4, 1)

B_HEADS = 8
B_NOPE_DIM = 64
B_ROPE_DIM = 32
B_QK_DIM = B_NOPE_DIM + B_ROPE_DIM
B_V_DIM = 64
B_WIDTH = B_HEADS * B_V_DIM
B_QK_PAD = B_HEADS * LANES
Q_LORA = 256
KV_LORA = 128

M_HEADS = 4
M_HEAD_DIM = 128
M_WIDTH = M_HEADS * M_HEAD_DIM

VMEM_LIMIT = 56 * 1024 * 1024

BF16 = jnp.bfloat16
F32 = jnp.float32
NEG_BIG = -1e30
LOG2_E = float(np.log2(np.e))


def _params(n_axes):
    return pltpu.CompilerParams(dimension_semantics=("arbitrary",) * n_axes,
                                vmem_limit_bytes=VMEM_LIMIT)


def _row_rms_scale(x, width):
    return lax.rsqrt(jnp.sum(x * x, axis=-1, keepdims=True) * (1.0 / width) + EPS)


def _dot(a, b):
    return jnp.dot(a, b, preferred_element_type=F32)


def _dot_nt(a, b):
    return lax.dot_general(a, b, (((1,), (1,)), ((), ())), preferred_element_type=F32)


def _rope_chunk(xc, cos, sin_lo, sin_hi, half):
    fwd = pltpu.roll(xc, LANES - half, axis=1)
    bwd = pltpu.roll(xc, half, axis=1)
    return xc * cos + fwd * sin_lo + bwd * sin_hi


def _proj_kernel(x_ref, gmix_ref, win_ref, gqa_ref, gka_ref, gcq_ref, gckv_ref,
                 wuq_ref, wuk_ref, wuv_ref, gqb_ref, gkb_ref, taba_ref, tabb_ref,
                 qa_ref, ka_ref, va_ref, qb_ref, kb_ref, vbt_ref):
    x = x_ref[...]
    h = (x * _row_rms_scale(x, x.shape[-1]) * gmix_ref[...]).astype(BF16)
    proj = _dot(h, win_ref[...])

    lane = lax.broadcasted_iota(jnp.int32, (1, LANES), 1)
    low_head = lane < A_HEAD_DIM
    cos_a, sin_lo_a, sin_hi_a = taba_ref[0], taba_ref[1], taba_ref[2]
    cos_b, sin_lo_b, sin_hi_b = tabb_ref[0], tabb_ref[1], tabb_ref[2]

    def a_heads(col0, g_ref, out_ref):
        for c in range(A_WIDTH // LANES):
            xc = proj[:, col0 + c * LANES: col0 + (c + 1) * LANES]
            sq = xc * xc
            s_lo = jnp.sum(jnp.where(low_head, sq, 0.0), axis=-1, keepdims=True)
            s_hi = jnp.sum(jnp.where(low_head, 0.0, sq), axis=-1, keepdims=True)
            rs = jnp.where(low_head,
                           lax.rsqrt(s_lo * (1.0 / A_HEAD_DIM) + EPS),
                           lax.rsqrt(s_hi * (1.0 / A_HEAD_DIM) + EPS))
            y = xc * rs * g_ref[:, c * LANES:(c + 1) * LANES]
            y = _rope_chunk(y, cos_a, sin_lo_a, sin_hi_a, A_ROPE_DIM // 2)
            out_ref[:, c * LANES:(c + 1) * LANES] = y.astype(out_ref.dtype)

    a_heads(0, gqa_ref, qa_ref)
    a_heads(A_WIDTH, gka_ref, ka_ref)
    va_ref[...] = proj[:, 2 * A_WIDTH:3 * A_WIDTH].astype(va_ref.dtype)

    c0 = 3 * A_WIDTH
    cq = proj[:, c0:c0 + Q_LORA]
    ckv = proj[:, c0 + Q_LORA:c0 + Q_LORA + KV_LORA]
    kr = proj[:, c0 + Q_LORA + KV_LORA:c0 + Q_LORA + 2 * KV_LORA]
    cqn = (cq * _row_rms_scale(cq, Q_LORA) * gcq_ref[...]).astype(BF16)
    ckvn = (ckv * _row_rms_scale(ckv, KV_LORA) * gckv_ref[...]).astype(BF16)
    qb = _dot(cqn, wuq_ref[...])
    kn = _dot(ckvn, wuk_ref[...])
    vbt_ref[...] = _dot(ckvn, wuv_ref[...]).T.astype(vbt_ref.dtype)

    def b_heads(get_chunk, g_ref, out_ref):
        for c in range(B_HEADS):
            xc = get_chunk(c)
            rs = lax.rsqrt(jnp.sum(xc * xc, axis=-1, keepdims=True) * (1.0 / B_QK_DIM) + EPS)
            y = xc * rs * g_ref[:, c * LANES:(c + 1) * LANES]
            y = _rope_chunk(y, cos_b, sin_lo_b, sin_hi_b, B_ROPE_DIM // 2)
            out_ref[:, c * LANES:(c + 1) * LANES] = y.astype(out_ref.dtype)

    b_heads(lambda c: qb[:, c * LANES:(c + 1) * LANES], gqb_ref, qb_ref)
    b_heads(lambda c: kn[:, c * LANES:(c + 1) * LANES] + kr, gkb_ref, kb_ref)


def _proj(x2d, seq, w, tm):
    n, d = x2d.shape
    nblk_seq = seq // tm
    row = lambda i: (i, 0)
    const = lambda i: (0, 0)
    tab = lambda i: (0, i % nblk_seq, 0)
    full = lambda a: pl.BlockSpec(a.shape, const)
    widths = [A_WIDTH, A_WIDTH, A_WIDTH, B_QK_PAD, B_QK_PAD]
    return pl.pallas_call(
        _proj_kernel,
        grid=(n // tm,),
        in_specs=[pl.BlockSpec((tm, d), row), full(w['gmix']), full(w['win']), full(w['gqa']), full(w['gka']),
                  full(w['gcq']), full(w['gckv']), full(w['wuq']), full(w['wuk']), full(w['wuv']),
                  full(w['gqb']), full(w['gkb']),
                  pl.BlockSpec((3, tm, LANES), tab), pl.BlockSpec((3, tm, LANES), tab)],
        out_specs=[pl.BlockSpec((tm, wd), row) for wd in widths]
                  + [pl.BlockSpec((B_WIDTH, tm), lambda i: (0, i))],
        out_shape=[jax.ShapeDtypeStruct((n, wd), F32 if wd == A_WIDTH else BF16) for wd in widths]
                  + [jax.ShapeDtypeStruct((B_WIDTH, n), BF16)],
        compiler_params=_params(1),
        name="proj",
    )(x2d, w['gmix'], w['win'], w['gqa'], w['gka'], w['gcq'], w['gckv'], w['wuq'], w['wuk'], w['wuv'],
      w['gqb'], w['gkb'], w['taba'], w['tabb'])


DIL_TQ = 128
DIL_WIN = DIL_TQ + 2 * A_HALF_WINDOW
DIL_SLOTS = 4


def _strided_rows(start, size, stride):
    return pl.ds(start, size) if stride == 1 else pl.ds(start, size, stride=stride)


def _dilated_kernel(q_ref, k_ref, v_ref, o_ref, lse_scr, s_scr, bias_scr):
    seq = q_ref.shape[1]
    lane = lax.broadcasted_iota(jnp.int32, (1, LANES), 1)
    low_head = lane < A_HEAD_DIM
    rel = (lax.broadcasted_iota(jnp.int32, (2 * DIL_TQ, DIL_WIN), 1)
           - lax.broadcasted_iota(jnp.int32, (2 * DIL_TQ, DIL_WIN), 0) % DIL_TQ)
    for place in range(3):
        bias_scr[place] = jnp.where(jnp.abs(rel - place * A_HALF_WINDOW) <= A_HALF_WINDOW, 0.0, NEG_BIG)

    for dil in DILATIONS:
        sub_len = seq // dil
        n_blk = sub_len // DIL_TQ
        n_total = dil * n_blk

        def coords(idx, dil=dil, sub_len=sub_len, n_blk=n_blk):
            r = idx // n_blk
            q0 = (idx % n_blk) * DIL_TQ
            k0 = jnp.clip(q0 - A_HALF_WINDOW, 0, sub_len - DIL_WIN)
            rows_q = _strided_rows(r + dil * q0, DIL_TQ, dil)
            rows_k = _strided_rows(r + dil * k0, DIL_WIN, dil)
            return rows_q, rows_k, (q0 - k0) // A_HALF_WINDOW

        def scores_into(slot, idx, coords=coords):
            rows_q, rows_k, place = coords(idx)
            qp = q_ref[0, rows_q, :]
            kw = k_ref[0, rows_k, :].astype(BF16)
            q2 = jnp.concatenate([jnp.where(low_head, qp, 0.0), jnp.where(low_head, 0.0, qp)], axis=0).astype(BF16)
            s_scr[slot] = _dot_nt(q2, kw) + bias_scr[place]

        def finish(slot, idx, dil=dil, coords=coords):
            rows_q, rows_k, _ = coords(idx)
            vw = v_ref[0, rows_k, :].astype(BF16)
            s = s_scr[slot]
            m = jnp.max(s, axis=-1, keepdims=True)
            pr = jnp.exp(s - m)
            den = jnp.sum(pr, axis=-1, keepdims=True)
            o2 = _dot(pr.astype(BF16), vw) / den
            lse2 = m + jnp.log(den)
            o = jnp.where(low_head, o2[:DIL_TQ], o2[DIL_TQ:])
            lse = jnp.where(low_head, lse2[:DIL_TQ], lse2[DIL_TQ:])
            if dil != DILATIONS[0]:
                o_prev, lse_prev = o_ref[0, rows_q, :], lse_scr[rows_q, :]
                top = jnp.maximum(lse_prev, lse)
                w_prev, w_new = jnp.exp(lse_prev - top), jnp.exp(lse - top)
                tot = w_prev + w_new
                o = (w_prev * o_prev + w_new * o) / tot
                lse = top + jnp.log(tot)
            o_ref[0, rows_q, :] = o
            if dil != DILATIONS[-1]:
                lse_scr[rows_q, :] = lse

        assert n_total % DIL_SLOTS == 0
        for slot in range(DIL_SLOTS - 1):
            scores_into(slot, slot)

        def step(i, carry, scores_into=scores_into, finish=finish, n_total=n_total):
            for slot in range(DIL_SLOTS):
                idx = DIL_SLOTS * i + slot
                ahead = idx + DIL_SLOTS - 1
                scores_into((slot + DIL_SLOTS - 1) % DIL_SLOTS, jnp.where(ahead >= n_total, ahead - n_total, ahead))
                finish(slot, idx)
            return carry

        lax.fori_loop(0, n_total // DIL_SLOTS, step, 0)


def _dilated(qa, ka, va):
    b, s, w = qa.shape
    spec = pl.BlockSpec((1, s, LANES), lambda bi, p: (bi, 0, p))
    out = pl.pallas_call(
        _dilated_kernel,
        grid=(b, w // LANES),
        in_specs=[spec, spec, spec],
        out_specs=spec,
        out_shape=jax.ShapeDtypeStruct((b, s, w), F32),
        scratch_shapes=[pltpu.VMEM((s, LANES), F32),
                        pltpu.VMEM((DIL_SLOTS, 2 * DIL_TQ, DIL_WIN), F32),
                        pltpu.VMEM((3, 2 * DIL_TQ, DIL_WIN), F32)],
        compiler_params=_params(2),
        name="dilated",
    )(qa, ka, va)
    return out.reshape(b * s, w)


MLA_DEN_ROWS = 16


def _mla_kernel(q_ref, k_ref, vt_ref, o_ref, s_scr, *, tk):
    seq = k_ref.shape[1]
    tq = q_ref.shape[1]
    heads = range(2)
    n_blocks = seq // tk
    assert seq % (2 * tk) == 0

    def scores_into(slot, blk):
        start = pl.multiple_of(blk * tk, tk)
        for hh in heads:
            cols = slice(hh * LANES, (hh + 1) * LANES)
            s_scr[slot, hh] = _dot_nt(k_ref[0, pl.ds(start, tk), cols], q_ref[0, :, cols])

    scores_into(0, 0)

    def body(t, carry):
        state = list(carry)
        for slot in range(2):
            blk = 2 * t + slot
            scores_into(1 - slot, jnp.where(blk + 1 == n_blocks, 0, blk + 1))
            start = pl.multiple_of(blk * tk, tk)
            for hh in heads:
                m, acc = state[hh]
                m_new = jnp.maximum(m, jnp.max(s_scr[slot, hh], axis=0, keepdims=True))
                alpha = jnp.exp2(m - m_new)
                pt = jnp.exp2(s_scr[slot, hh] - m_new).astype(BF16)
                vt1 = jnp.concatenate([vt_ref[hh * B_V_DIM:(hh + 1) * B_V_DIM, pl.ds(start, tk)],
                                       jnp.ones((MLA_DEN_ROWS, tk), BF16)], axis=0)
                acc = alpha * acc + _dot(vt1, pt)
                state[hh] = (m_new, acc)
        return tuple(state)

    init = tuple((jnp.full((1, tq), NEG_BIG, F32), jnp.zeros((B_V_DIM + MLA_DEN_ROWS, tq), F32)) for _ in heads)
    final = lax.fori_loop(0, n_blocks // 2, body, init)
    ot = jnp.concatenate([acc[:B_V_DIM] / acc[B_V_DIM:B_V_DIM + 1] for _, acc in final], axis=0)
    o_ref[0] = ot.T


def _mla(qb, kb, vbt, tq, tk):
    b, s, _ = qb.shape
    pairs = B_HEADS // 2
    out = pl.pallas_call(
        functools.partial(_mla_kernel, tk=tk),
        grid=(b, pairs, s // tq),
        in_specs=[pl.BlockSpec((1, tq, 2 * LANES), lambda bi, p, i: (bi, i, p)),
                  pl.BlockSpec((1, s, 2 * LANES), lambda bi, p, i: (bi, 0, p)),
                  pl.BlockSpec((LANES, s), lambda bi, p, i: (p, bi))],
        out_specs=pl.BlockSpec((1, tq, LANES), lambda bi, p, i: (bi, i, p)),
        out_shape=jax.ShapeDtypeStruct((b, s, B_WIDTH), F32),
        scratch_shapes=[pltpu.VMEM((2, 2, tk, tq), F32)],
        compiler_params=_params(3),
        name="mla",
    )(qb, kb, vbt)
    return out.reshape(b * s, B_WIDTH)


def _out_proj_kernel(x_ref, oa_ref, ob_ref, ga_ref, gb_ref, wa_ref, wb_ref, y_ref):
    oa, ob = oa_ref[...], ob_ref[...]
    na = (oa * _row_rms_scale(oa, A_WIDTH) * ga_ref[...]).astype(BF16)
    nb = (ob * _row_rms_scale(ob, B_WIDTH) * gb_ref[...]).astype(BF16)
    y_ref[...] = x_ref[...] + _dot(na, wa_ref[...]) + _dot(nb, wb_ref[...])


def _out_proj(x2d, oa, ob, w, tm):
    n, d = x2d.shape
    row = lambda i: (i, 0)
    const = lambda i: (0, 0)
    full = lambda a: pl.BlockSpec(a.shape, const)
    half = pl.BlockSpec((tm, A_WIDTH), row)
    return pl.pallas_call(
        _out_proj_kernel,
        grid=(n // tm,),
        in_specs=[pl.BlockSpec((tm, d), row), half, half,
                  full(w['ga_out']), full(w['gb_out']), full(w['wout_a']), full(w['wout_b'])],
        out_specs=pl.BlockSpec((tm, d), row),
        out_shape=jax.ShapeDtypeStruct((n, d), F32),
        compiler_params=_params(1),
        name="out_proj",
    )(x2d, oa, ob, w['ga_out'], w['gb_out'], w['wout_a'], w['wout_b'])


def _mem_kv_kernel(mem_ref, g_ref, wkv_ref, gk_ref, k_ref, v_ref):
    mem = mem_ref[0]
    mm = (mem * _row_rms_scale(mem, mem.shape[-1]) * g_ref[...]).astype(BF16)
    kv = _dot(mm, wkv_ref[...])
    for hd in range(M_HEADS):
        cols = slice(hd * M_HEAD_DIM, (hd + 1) * M_HEAD_DIM)
        kh = kv[:, cols]
        k_ref[0, :, cols] = (kh * _row_rms_scale(kh, M_HEAD_DIM) * gk_ref[...]).astype(k_ref.dtype)
    v_ref[0] = kv[:, M_WIDTH:].astype(v_ref.dtype)


def _mem_kv(mem, w):
    b, n_mem, d = mem.shape
    const = lambda bi: (0, 0)
    full = lambda a: pl.BlockSpec(a.shape, const)
    blk = pl.BlockSpec((1, n_mem, M_WIDTH), lambda bi: (bi, 0, 0))
    return pl.pallas_call(
        _mem_kv_kernel,
        grid=(b,),
        in_specs=[pl.BlockSpec((1, n_mem, d), lambda bi: (bi, 0, 0)),
                  full(w['gmem_kv']), full(w['wkv_m']), full(w['gk_m'])],
        out_specs=[blk, blk],
        out_shape=[jax.ShapeDtypeStruct((b, n_mem, M_WIDTH), BF16)] * 2,
        compiler_params=_params(1),
        name="mem_kv",
    )(mem, w['gmem_kv'], w['wkv_m'], w['gk_m'])


def _mem_attn_kernel(x_ref, g_ref, wq_ref, gq_ref, k_ref, v_ref, wo_ref, y_ref):
    x = x_ref[0]
    h = (x * _row_rms_scale(x, x.shape[-1]) * g_ref[...]).astype(BF16)
    q = _dot(h, wq_ref[...])
    heads = []
    for hd in range(M_HEADS):
        cols = slice(hd * M_HEAD_DIM, (hd + 1) * M_HEAD_DIM)
        qh = q[:, cols]
        qh = (qh * _row_rms_scale(qh, M_HEAD_DIM) * gq_ref[...]).astype(BF16)
        s = _dot_nt(qh, k_ref[0, :, cols])
        pr = jnp.exp(s - jnp.max(s, axis=-1, keepdims=True))
        pr = pr / jnp.sum(pr, axis=-1, keepdims=True)
        heads.append(_dot(pr.astype(BF16), v_ref[0, :, cols]).astype(BF16))
    om = jnp.concatenate(heads, axis=-1)
    y_ref[0] = x + _dot(om, wo_ref[...])


def _mem_attn(x3d, km, vm, w, tm):
    b, s, d = x3d.shape
    n_mem = km.shape[1]
    const = lambda bi, i: (0, 0)
    full = lambda a: pl.BlockSpec(a.shape, const)
    xblk = pl.BlockSpec((1, tm, d), lambda bi, i: (bi, i, 0))
    kvblk = pl.BlockSpec((1, n_mem, M_WIDTH), lambda bi, i: (bi, 0, 0))
    return pl.pallas_call(
        _mem_attn_kernel,
        grid=(b, s // tm),
        in_specs=[xblk, full(w['gmem']), full(w['wq_m']), full(w['gq_m']), kvblk, kvblk, full(w['wo_m'])],
        out_specs=xblk,
        out_shape=jax.ShapeDtypeStruct((b, s, d), F32),
        compiler_params=_params(2),
        name="mem_attn",
    )(x3d, w['gmem'], w['wq_m'], w['gq_m'], km, vm, w['wo_m'])


FFN_CHUNK = 1024


def _ffn_kernel(x_ref, g_ref, w1_ref, w2_ref, y_ref):
    x = x_ref[...]
    h = (x * _row_rms_scale(x, x.shape[-1]) * g_ref[...]).astype(BF16)
    acc = x
    for c in range(w1_ref.shape[1] // FFN_CHUNK):
        cols = slice(c * FFN_CHUNK, (c + 1) * FFN_CHUNK)
        a = jnp.maximum(_dot(h, w1_ref[:, cols]), 0.0)
        acc = acc + _dot((a * a).astype(BF16), w2_ref[cols, :])
    y_ref[...] = acc


def _ffn(x2d, w, tm):
    n, d = x2d.shape
    row = lambda i: (i, 0)
    const = lambda i: (0, 0)
    full = lambda a: pl.BlockSpec(a.shape, const)
    return pl.pallas_call(
        _ffn_kernel,
        grid=(n // tm,),
        in_specs=[pl.BlockSpec((tm, d), row), full(w['gffn']), full(w['w1']), full(w['w2'])],
        out_specs=pl.BlockSpec((tm, d), row),
        out_shape=jax.ShapeDtypeStruct((n, d), F32),
        compiler_params=_params(1),
        name="ffn",
    )(x2d, w['gffn'], w['w1'], w['w2'])


def _rope_tables(seq, rope_dim, first_lane, period):
    half = rope_dim // 2
    inv = ROPE_THETA ** (-jnp.arange(0, rope_dim, 2, dtype=F32) / rope_dim)
    ang = jnp.arange(seq, dtype=F32)[:, None] * inv[None, :]
    cos, sin = jnp.cos(ang), jnp.sin(ang)
    lane = np.arange(LANES) % period
    rel = lane - first_lane
    is_lo = (rel >= 0) & (rel < half)
    is_hi = (rel >= half) & (rel < rope_dim)
    idx = np.clip(np.where(is_hi, rel - half, rel), 0, half - 1)
    cos_t = jnp.where(is_lo | is_hi, cos[:, idx], 1.0)
    sin_lo = jnp.where(is_lo, -sin[:, idx], 0.0)
    sin_hi = jnp.where(is_hi, sin[:, idx], 0.0)
    return jnp.stack([cos_t, sin_lo, sin_hi]).astype(F32)


def _layer_weights(l, p):
    row = lambda v: v.reshape(1, -1).astype(F32)
    w_in = p['w_in'][l]
    d = w_in.shape[0]
    c_cq, c_ckv, c_kr = 3 * A_WIDTH, 3 * A_WIDTH + Q_LORA, 3 * A_WIDTH + Q_LORA + KV_LORA
    win = jnp.concatenate([
        w_in[:, :c_kr],
        jnp.zeros((d, B_NOPE_DIM), F32), w_in[:, c_kr:], jnp.zeros((d, LANES - B_QK_DIM), F32)], axis=1)
    pad_qk = lambda a: jnp.pad(a, ((0, 0), (0, 0), (0, LANES - a.shape[-1]))).reshape(a.shape[0], B_QK_PAD)
    wukv = p['w_ukv'][l].reshape(KV_LORA, B_HEADS, B_NOPE_DIM + B_V_DIM)
    pad_gain = lambda g: jnp.tile(jnp.pad(g, (0, LANES - B_QK_DIM)), B_HEADS)
    w_out = p['w_out'][l]
    return {
        'gmix': row(p['norm_mix_g'][l]),
        'win': win.astype(BF16),
        'gqa': row(jnp.tile(p['a_q_norm_g'][l], A_HEADS) * (A_HEAD_DIM ** -0.5)),
        'gka': row(jnp.tile(p['a_k_norm_g'][l], A_HEADS)),
        'gcq': row(p['b_cq_norm_g'][l]),
        'gckv': row(p['b_ckv_norm_g'][l]),
        'wuq': pad_qk(p['w_uq'][l].reshape(Q_LORA, B_HEADS, B_QK_DIM)).astype(BF16),
        'wuk': pad_qk(wukv[:, :, :B_NOPE_DIM]).astype(BF16),
        'wuv': wukv[:, :, B_NOPE_DIM:].reshape(KV_LORA, B_WIDTH).astype(BF16),
        'gqb': row(pad_gain(p['b_q_norm_g'][l]) * (B_QK_DIM ** -0.5 * LOG2_E)),
        'gkb': row(pad_gain(p['b_k_norm_g'][l])),
        'ga_out': row(p['a_out_norm_g'][l]),
        'gb_out': row(p['b_out_norm_g'][l]),
        'wout_a': w_out[:A_WIDTH].astype(BF16),
        'wout_b': w_out[A_WIDTH:].astype(BF16),
        'gmem': row(p['norm_mem_g'][l]),
        'gmem_kv': row(p['mem_kv_norm_g'][l]),
        'wq_m': p['m_wq'][l].astype(BF16),
        'wkv_m': p['m_wkv'][l].astype(BF16),
        'gq_m': row(p['m_q_norm_g'][l] * (M_HEAD_DIM ** -0.5)),
        'gk_m': row(p['m_k_norm_g'][l]),
        'wo_m': p['m_wo'][l].astype(BF16),
        'gffn': row(p['norm_ffn_g'][l]),
        'w1': p['w_ff1'][l].astype(BF16),
        'w2': p['w_ff2'][l].astype(BF16),
    }


def _tiles(seq):
    tm = 512 if seq % 512 == 0 else 256
    return dict(tm=tm, tq=256, tk=512)


def _trunk(x, mem, layers):
    b, s, d = x.shape
    assert s % (max(DILATIONS) * DIL_TQ) == 0 and s // max(DILATIONS) >= DIL_WIN
    t = _tiles(s)
    taba = _rope_tables(s, A_ROPE_DIM, 0, A_HEAD_DIM)
    tabb = _rope_tables(s, B_ROPE_DIM, B_NOPE_DIM, LANES)
    x2d = x.reshape(b * s, d)
    for w in layers:
        w = dict(w, taba=taba, tabb=tabb)
        qa, ka, va, qb, kb, vbt = _proj(x2d, s, w, t['tm'])
        as3d = lambda a: a.reshape(b, s, a.shape[-1])
        oa = _dilated(as3d(qa), as3d(ka), as3d(va))
        ob = _mla(as3d(qb), as3d(kb), vbt, t['tq'], t['tk'])
        x2d = _out_proj(x2d, oa, ob, w, t['tm'])
        km, vm = _mem_kv(mem, w)
        x2d = _mem_attn(x2d.reshape(b, s, d), km, vm, w, t['tm']).reshape(b * s, d)
        x2d = _ffn(x2d, w, t['tm'])
    return x2d.reshape(b, s, d)


def kernel(x_prompt, x_sample, mem_prompt, mem_sample, norm_mix_g, w_in, a_q_norm_g, a_k_norm_g, b_cq_norm_g, b_ckv_norm_g, w_uq, w_ukv, b_q_norm_g, b_k_norm_g, a_out_norm_g, b_out_norm_g, w_out, norm_mem_g, mem_kv_norm_g, m_wq, m_wkv, m_q_norm_g, m_k_norm_g, m_wo, norm_ffn_g, w_ff1, w_ff2):
    p = dict(norm_mix_g=norm_mix_g, w_in=w_in, a_q_norm_g=a_q_norm_g, a_k_norm_g=a_k_norm_g,
             b_cq_norm_g=b_cq_norm_g, b_ckv_norm_g=b_ckv_norm_g, w_uq=w_uq, w_ukv=w_ukv,
             b_q_norm_g=b_q_norm_g, b_k_norm_g=b_k_norm_g, a_out_norm_g=a_out_norm_g,
             b_out_norm_g=b_out_norm_g, w_out=w_out, norm_mem_g=norm_mem_g, mem_kv_norm_g=mem_kv_norm_g,
             m_wq=m_wq, m_wkv=m_wkv, m_q_norm_g=m_q_norm_g, m_k_norm_g=m_k_norm_g, m_wo=m_wo,
             norm_ffn_g=norm_ffn_g, w_ff1=w_ff1, w_ff2=w_ff2)
    layers = [_layer_weights(l, p) for l in range(w_in.shape[0])]
    return (_trunk(x_prompt, mem_prompt, layers), _trunk(x_sample, mem_sample, layers))
```

```python
import functools

import numpy as np
import jax
import jax.numpy as jnp
from jax import lax
from jax.experimental import pallas as pl
from jax.experimental.pallas import tpu as pltpu

EPS = 1e-6
ROPE_THETA = 500000.0

LANES = 128
A_HEADS = 8
A_HEAD_DIM = 64
A_WIDTH = A_HEADS * A_HEAD_DIM
A_ROPE_DIM = 16
A_HALF_WINDOW = 64
DILATIONS = (16, 4, 1)

B_HEADS = 8
B_NOPE_DIM = 64
B_ROPE_DIM = 32
B_QK_DIM = B_NOPE_DIM + B_ROPE_DIM
B_V_DIM = 64
B_WIDTH = B_HEADS * B_V_DIM
B_QK_PAD = B_HEADS * LANES
Q_LORA = 256
KV_LORA = 128

M_HEADS = 4
M_HEAD_DIM = 128
M_WIDTH = M_HEADS * M_HEAD_DIM

VMEM_LIMIT = 56 * 1024 * 1024

BF16 = jnp.bfloat16
F32 = jnp.float32
NEG_BIG = -1e30
LOG2_E = float(np.log2(np.e))


def _a_pair_perm():
    heads = [np.arange(A_HEAD_DIM), A_HEAD_DIM + np.arange(A_HEAD_DIM)]
    r = A_ROPE_DIM // 2
    rest = (A_HEAD_DIM - A_ROPE_DIM) // 2
    first = [h[:r] for h in heads] + [h[A_ROPE_DIM:A_ROPE_DIM + rest] for h in heads]
    second = [h[r:A_ROPE_DIM] for h in heads] + [h[A_ROPE_DIM + rest:] for h in heads]
    return np.concatenate(first + second)


ROPE_SPAN = 16
_A_PERM = _a_pair_perm()
_A_PERM_ALL = np.concatenate([c * LANES + _A_PERM for c in range(A_WIDTH // LANES)])
_B_SPLIT = LANES // 2 - B_ROPE_DIM // 2
_B_PERM = np.concatenate([np.arange(B_NOPE_DIM, B_NOPE_DIM + B_ROPE_DIM // 2), np.arange(0, _B_SPLIT),
                          np.arange(B_NOPE_DIM + B_ROPE_DIM // 2, B_QK_DIM), np.arange(_B_SPLIT, B_NOPE_DIM)])


def _lane_mask(member):
    lane = lax.broadcasted_iota(jnp.int32, (1, LANES), 1)
    edges = np.flatnonzero(np.diff(np.concatenate([[0], member.astype(np.int8), [0]])))
    mask = None
    for lo, hi in zip(edges[::2], edges[1::2]):
        run = (lane >= int(lo)) & (lane < int(hi))
        mask = run if mask is None else (mask | run)
    return mask


def _params(n_axes):
    return pltpu.CompilerParams(dimension_semantics=("arbitrary",) * n_axes,
                                vmem_limit_bytes=VMEM_LIMIT)


def _row_rms_scale(x, width):
    return lax.rsqrt(jnp.sum(x * x, axis=-1, keepdims=True) * (1.0 / width) + EPS)


def _dot(a, b):
    return jnp.dot(a, b, preferred_element_type=F32)


def _dot_nt(a, b):
    return lax.dot_general(a, b, (((1,), (1,)), ((), ())), preferred_element_type=F32)


def _rope_chunk(xc, cos, sin):
    return xc * cos + pltpu.roll(xc, LANES // 2, axis=1) * sin


def _proj_kernel(x_ref, gmix_ref, win_ref, gqa_ref, gka_ref, gcq_ref, gckv_ref,
                 wuq_ref, wuk_ref, wuv_ref, gqb_ref, gkb_ref, taba_ref, tabb_ref,
                 qa_ref, ka_ref, va_ref, qb_ref, kb_ref, vbt_ref):
    x = x_ref[...]
    h = (x * _row_rms_scale(x, x.shape[-1]) * gmix_ref[...]).astype(BF16)
    proj = _dot(h, win_ref[...])

    first_head = _lane_mask(_A_PERM < A_HEAD_DIM)
    cos_a, sin_a = taba_ref[0], taba_ref[1]
    cos_b, sin_b = tabb_ref[0], tabb_ref[1]

    def a_heads(col0, g_ref, out_ref):
        for c in range(A_WIDTH // LANES):
            xc = proj[:, col0 + c * LANES: col0 + (c + 1) * LANES]
            sq = xc * xc
            s_0 = jnp.sum(jnp.where(first_head, sq, 0.0), axis=-1, keepdims=True)
            s_1 = jnp.sum(jnp.where(first_head, 0.0, sq), axis=-1, keepdims=True)
            rs = jnp.where(first_head,
                           lax.rsqrt(s_0 * (1.0 / A_HEAD_DIM) + EPS),
                           lax.rsqrt(s_1 * (1.0 / A_HEAD_DIM) + EPS))
            y = xc * rs * g_ref[:, c * LANES:(c + 1) * LANES]
            y = _rope_chunk(y, cos_a, sin_a)
            out_ref[:, c * LANES:(c + 1) * LANES] = y.astype(out_ref.dtype)

    a_heads(0, gqa_ref, qa_ref)
    a_heads(A_WIDTH, gka_ref, ka_ref)
    va_ref[...] = proj[:, 2 * A_WIDTH:3 * A_WIDTH].astype(va_ref.dtype)

    c0 = 3 * A_WIDTH
    cq = proj[:, c0:c0 + Q_LORA]
    ckv = proj[:, c0 + Q_LORA:c0 + Q_LORA + KV_LORA]
    kr = proj[:, c0 + Q_LORA + KV_LORA:c0 + Q_LORA + 2 * KV_LORA]
    cqn = (cq * _row_rms_scale(cq, Q_LORA) * gcq_ref[...]).astype(BF16)
    ckvn = (ckv * _row_rms_scale(ckv, KV_LORA) * gckv_ref[...]).astype(BF16)
    qb = _dot(cqn, wuq_ref[...])
    kn = _dot(ckvn, wuk_ref[...])
    vbt_ref[...] = _dot(ckvn, wuv_ref[...]).T.astype(vbt_ref.dtype)

    def b_heads(get_chunk, g_ref, out_ref):
        for c in range(B_HEADS):
            xc = get_chunk(c)
            rs = lax.rsqrt(jnp.sum(xc * xc, axis=-1, keepdims=True) * (1.0 / B_QK_DIM) + EPS)
            y = xc * rs * g_ref[:, c * LANES:(c + 1) * LANES]
            y = _rope_chunk(y, cos_b, sin_b)
            out_ref[:, c * LANES:(c + 1) * LANES] = y.astype(out_ref.dtype)

    b_heads(lambda c: qb[:, c * LANES:(c + 1) * LANES], gqb_ref, qb_ref)
    b_heads(lambda c: kn[:, c * LANES:(c + 1) * LANES] + kr, gkb_ref, kb_ref)


def _proj(x2d, seq, w, tm):
    n, d = x2d.shape
    nblk_seq = seq // tm
    row = lambda i: (i, 0)
    const = lambda i: (0, 0)
    tab = lambda i: (0, i % nblk_seq, 0)
    full = lambda a: pl.BlockSpec(a.shape, const)
    widths = [A_WIDTH, A_WIDTH, A_WIDTH, B_QK_PAD, B_QK_PAD]
    return pl.pallas_call(
        _proj_kernel,
        grid=(n // tm,),
        in_specs=[pl.BlockSpec((tm, d), row), full(w['gmix']), full(w['win']), full(w['gqa']), full(w['gka']),
                  full(w['gcq']), full(w['gckv']), full(w['wuq']), full(w['wuk']), full(w['wuv']),
                  full(w['gqb']), full(w['gkb']),
                  pl.BlockSpec((2, tm, LANES), tab), pl.BlockSpec((2, tm, LANES), tab)],
        out_specs=[pl.BlockSpec((tm, wd), row) for wd in widths]
                  + [pl.BlockSpec((B_WIDTH, tm), lambda i: (0, i))],
        out_shape=[jax.ShapeDtypeStruct((n, wd), F32 if wd == A_WIDTH else BF16) for wd in widths]
                  + [jax.ShapeDtypeStruct((B_WIDTH, n), BF16)],
        compiler_params=_params(1),
        name="proj",
    )(x2d, w['gmix'], w['win'], w['gqa'], w['gka'], w['gcq'], w['gckv'], w['wuq'], w['wuk'], w['wuv'],
      w['gqb'], w['gkb'], w['taba'], w['tabb'])


DIL_TQ = 128
DIL_WIN = DIL_TQ + 2 * A_HALF_WINDOW
DIL_SLOTS = 4


def _strided_rows(start, size, stride):
    return pl.ds(start, size) if stride == 1 else pl.ds(start, size, stride=stride)


def _dilated_kernel(q_ref, k_ref, v_ref, o_ref, lse_scr, s_scr, bias_scr):
    seq = q_ref.shape[1]
    lane = lax.broadcasted_iota(jnp.int32, (1, LANES), 1)
    low_head = lane < A_HEAD_DIM
    q_first = _lane_mask(_A_PERM < A_HEAD_DIM)
    rel = (lax.broadcasted_iota(jnp.int32, (2 * DIL_TQ, DIL_WIN), 1)
           - lax.broadcasted_iota(jnp.int32, (2 * DIL_TQ, DIL_WIN), 0) % DIL_TQ)
    for place in range(3):
        bias_scr[place] = jnp.where(jnp.abs(rel - place * A_HALF_WINDOW) <= A_HALF_WINDOW, 0.0, NEG_BIG)

    for dil in DILATIONS:
        sub_len = seq // dil
        n_blk = sub_len // DIL_TQ
        n_total = dil * n_blk

        def coords(idx, dil=dil, sub_len=sub_len, n_blk=n_blk):
            r = idx // n_blk
            q0 = (idx % n_blk) * DIL_TQ
            k0 = jnp.clip(q0 - A_HALF_WINDOW, 0, sub_len - DIL_WIN)
            rows_q = _strided_rows(r + dil * q0, DIL_TQ, dil)
            rows_k = _strided_rows(r + dil * k0, DIL_WIN, dil)
            return rows_q, rows_k, (q0 - k0) // A_HALF_WINDOW

        def scores_into(slot, idx, coords=coords):
            rows_q, rows_k, place = coords(idx)
            qp = q_ref[0, rows_q, :]
            kw = k_ref[0, rows_k, :].astype(BF16)
            q2 = jnp.concatenate([jnp.where(q_first, qp, 0.0), jnp.where(q_first, 0.0, qp)], axis=0).astype(BF16)
            s_scr[slot] = _dot_nt(q2, kw) + bias_scr[place]

        def finish(slot, idx, dil=dil, coords=coords):
            rows_q, rows_k, _ = coords(idx)
            vw = v_ref[0, rows_k, :].astype(BF16)
            s = s_scr[slot]
            m = jnp.max(s, axis=-1, keepdims=True)
            pr = jnp.exp(s - m)
            den = jnp.sum(pr, axis=-1, keepdims=True)
            o2 = _dot(pr.astype(BF16), vw) / den
            lse2 = m + jnp.log(den)
            o = jnp.where(low_head, o2[:DIL_TQ], o2[DIL_TQ:])
            lse = jnp.where(low_head, lse2[:DIL_TQ], lse2[DIL_TQ:])
            if dil != DILATIONS[0]:
                o_prev, lse_prev = o_ref[0, rows_q, :], lse_scr[rows_q, :]
                top = jnp.maximum(lse_prev, lse)
                w_prev, w_new = jnp.exp(lse_prev - top), jnp.exp(lse - top)
                tot = w_prev + w_new
                o = (w_prev * o_prev + w_new * o) / tot
                lse = top + jnp.log(tot)
            o_ref[0, rows_q, :] = o
            if dil != DILATIONS[-1]:
                lse_scr[rows_q, :] = lse

        assert n_total % DIL_SLOTS == 0
        for slot in range(DIL_SLOTS - 1):
            scores_into(slot, slot)

        def step(i, carry, scores_into=scores_into, finish=finish, n_total=n_total):
            for slot in range(DIL_SLOTS):
                idx = DIL_SLOTS * i + slot
                ahead = idx + DIL_SLOTS - 1
                scores_into((slot + DIL_SLOTS - 1) % DIL_SLOTS, jnp.where(ahead >= n_total, ahead - n_total, ahead))
                finish(slot, idx)
            return carry

        lax.fori_loop(0, n_total // DIL_SLOTS, step, 0)


def _dilated(qa, ka, va):
    b, s, w = qa.shape
    spec = pl.BlockSpec((1, s, LANES), lambda bi, p: (bi, 0, p))
    out = pl.pallas_call(
        _dilated_kernel,
        grid=(b, w // LANES),
        in_specs=[spec, spec, spec],
        out_specs=spec,
        out_shape=jax.ShapeDtypeStruct((b, s, w), F32),
        scratch_shapes=[pltpu.VMEM((s, LANES), F32),
                        pltpu.VMEM((DIL_SLOTS, 2 * DIL_TQ, DIL_WIN), F32),
                        pltpu.VMEM((3, 2 * DIL_TQ, DIL_WIN), F32)],
        compiler_params=_params(2),
        name="dilated",
    )(qa, ka, va)
    return out.reshape(b * s, w)


MLA_DEN_ROWS = 16
MLA_SLOTS = 8
MLA_AHEAD = 2


def _mla_kernel(q_ref, k_ref, vt_ref, o_ref, s_scr, *, tq, tk):
    seq = k_ref.shape[1]
    heads = range(2)
    n_blocks = seq // tk
    n_tiles = seq // tq
    assert seq % (MLA_SLOTS * tk) == 0 and seq % tq == 0

    def scores_into(slot, tile, blk):
        q0 = pl.multiple_of(tile * tq, tq)
        k0 = pl.multiple_of(blk * tk, tk)
        for hh in heads:
            cols = slice(hh * LANES, (hh + 1) * LANES)
            s_scr[slot, hh] = _dot_nt(k_ref[0, pl.ds(k0, tk), cols], q_ref[0, pl.ds(q0, tq), cols])

    for blk in range(MLA_AHEAD):
        scores_into(blk, 0, blk)

    def tile_body(tile, carry):
        def body(t, state):
            state = list(state)
            for slot in range(MLA_SLOTS):
                blk = MLA_SLOTS * t + slot
                wrap = blk + MLA_AHEAD >= n_blocks
                scores_into((slot + MLA_AHEAD) % MLA_SLOTS,
                            jnp.where(wrap, jnp.minimum(tile + 1, n_tiles - 1), tile),
                            jnp.where(wrap, blk + MLA_AHEAD - n_blocks, blk + MLA_AHEAD))
                k0 = pl.multiple_of(blk * tk, tk)
                for hh in heads:
                    m, acc = state[hh]
                    m_new = jnp.maximum(m, jnp.max(s_scr[slot, hh], axis=0, keepdims=True))
                    alpha = jnp.exp2(m - m_new)
                    pt = jnp.exp2(s_scr[slot, hh] - m_new).astype(BF16)
                    vt1 = jnp.concatenate([vt_ref[hh * B_V_DIM:(hh + 1) * B_V_DIM, pl.ds(k0, tk)],
                                           jnp.ones((MLA_DEN_ROWS, tk), BF16)], axis=0)
                    acc = alpha * acc + _dot(vt1, pt)
                    state[hh] = (m_new, acc)
            return tuple(state)

        init = tuple((jnp.full((1, tq), NEG_BIG, F32), jnp.zeros((B_V_DIM + MLA_DEN_ROWS, tq), F32))
                     for _ in heads)
        final = lax.fori_loop(0, n_blocks // MLA_SLOTS, body, init)
        ot = jnp.concatenate([acc[:B_V_DIM] / acc[B_V_DIM:B_V_DIM + 1] for _, acc in final], axis=0)
        o_ref[0, pl.ds(pl.multiple_of(tile * tq, tq), tq), :] = ot.T
        return carry

    lax.fori_loop(0, n_tiles, tile_body, 0)


def _mla(qb, kb, vbt, tq, tk):
    b, s, _ = qb.shape
    qk_spec = pl.BlockSpec((1, s, 2 * LANES), lambda bi, p: (bi, 0, p))
    out = pl.pallas_call(
        functools.partial(_mla_kernel, tq=tq, tk=tk),
        grid=(b, B_HEADS // 2),
        in_specs=[qk_spec, qk_spec, pl.BlockSpec((LANES, s), lambda bi, p: (p, bi))],
        out_specs=pl.BlockSpec((1, s, LANES), lambda bi, p: (bi, 0, p)),
        out_shape=jax.ShapeDtypeStruct((b, s, B_WIDTH), F32),
        scratch_shapes=[pltpu.VMEM((MLA_SLOTS, 2, tk, tq), F32)],
        compiler_params=_params(2),
        name="mla",
    )(qb, kb, vbt)
    return out.reshape(b * s, B_WIDTH)


def _mem_kv_kernel(mem_ref, g_ref, wkv_ref, gk_ref, k_ref, v_ref):
    mem = mem_ref[0]
    mm = (mem * _row_rms_scale(mem, mem.shape[-1]) * g_ref[...]).astype(BF16)
    kv = _dot(mm, wkv_ref[...])
    for hd in range(M_HEADS):
        cols = slice(hd * M_HEAD_DIM, (hd + 1) * M_HEAD_DIM)
        kh = kv[:, cols]
        k_ref[0, :, cols] = (kh * _row_rms_scale(kh, M_HEAD_DIM) * gk_ref[...]).astype(k_ref.dtype)
    v_ref[0] = kv[:, M_WIDTH:].astype(v_ref.dtype)


def _mem_kv(mem, w):
    b, n_mem, d = mem.shape
    const = lambda bi: (0, 0)
    full = lambda a: pl.BlockSpec(a.shape, const)
    blk = pl.BlockSpec((1, n_mem, M_WIDTH), lambda bi: (bi, 0, 0))
    return pl.pallas_call(
        _mem_kv_kernel,
        grid=(b,),
        in_specs=[pl.BlockSpec((1, n_mem, d), lambda bi: (bi, 0, 0)),
                  full(w['gmem_kv']), full(w['wkv_m']), full(w['gk_m'])],
        out_specs=[blk, blk],
        out_shape=[jax.ShapeDtypeStruct((b, n_mem, M_WIDTH), BF16)] * 2,
        compiler_params=_params(1),
        name="mem_kv",
    )(mem, w['gmem_kv'], w['wkv_m'], w['gk_m'])


FFN_CHUNK = 1024


def _post_kernel(x_ref, oa_ref, ob_ref, ga_ref, gb_ref, wa_ref, wb_ref,
                 gmem_ref, wq_ref, gq_ref, k_ref, v_ref, wo_ref, gffn_ref, w1_ref, w2_ref, y_ref):
    oa, ob = oa_ref[0], ob_ref[0]
    na = (oa * _row_rms_scale(oa, A_WIDTH) * ga_ref[...]).astype(BF16)
    nb = (ob * _row_rms_scale(ob, B_WIDTH) * gb_ref[...]).astype(BF16)
    x = x_ref[0] + _dot(na, wa_ref[...]) + _dot(nb, wb_ref[...])

    h = (x * _row_rms_scale(x, x.shape[-1]) * gmem_ref[...]).astype(BF16)
    q = _dot(h, wq_ref[...])
    heads = []
    for hd in range(M_HEADS):
        cols = slice(hd * M_HEAD_DIM, (hd + 1) * M_HEAD_DIM)
        qh = q[:, cols]
        qh = (qh * _row_rms_scale(qh, M_HEAD_DIM) * gq_ref[...]).astype(BF16)
        s = _dot_nt(qh, k_ref[0, :, cols])
        pr = jnp.exp(s - jnp.max(s, axis=-1, keepdims=True))
        pr = pr / jnp.sum(pr, axis=-1, keepdims=True)
        heads.append(_dot(pr.astype(BF16), v_ref[0, :, cols]).astype(BF16))
    x = x + _dot(jnp.concatenate(heads, axis=-1), wo_ref[...])

    h = (x * _row_rms_scale(x, x.shape[-1]) * gffn_ref[...]).astype(BF16)
    for c in range(w1_ref.shape[1] // FFN_CHUNK):
        cols = slice(c * FFN_CHUNK, (c + 1) * FFN_CHUNK)
        a = jnp.maximum(_dot(h, w1_ref[:, cols]), 0.0)
        x = x + _dot((a * a).astype(BF16), w2_ref[cols, :])
    y_ref[0] = x


def _post(x3d, oa, ob, km, vm, w, tm):
    b, s, d = x3d.shape
    n_mem = km.shape[1]
    const = lambda bi, i: (0, 0)
    full = lambda a: pl.BlockSpec(a.shape, const, pipeline_mode=pl.Buffered(1))
    tile = lambda width: pl.BlockSpec((1, tm, width), lambda bi, i: (bi, i, 0))
    kvblk = pl.BlockSpec((1, n_mem, M_WIDTH), lambda bi, i: (bi, 0, 0))
    names = ['ga_out', 'gb_out', 'wout_a', 'wout_b', 'gmem', 'wq_m', 'gq_m']
    tail = ['wo_m', 'gffn', 'w1', 'w2']
    return pl.pallas_call(
        _post_kernel,
        grid=(b, s // tm),
        in_specs=[tile(d), tile(A_WIDTH), tile(B_WIDTH)] + [full(w[n]) for n in names] + [kvblk, kvblk]
                 + [full(w[n]) for n in tail],
        out_specs=tile(d),
        out_shape=jax.ShapeDtypeStruct((b, s, d), F32),
        compiler_params=_params(2),
        name="post",
    )(x3d, oa, ob, *[w[n] for n in names], km, vm, *[w[n] for n in tail])


def _rope_tables(seq, rope_dim):
    half = rope_dim // 2
    assert ROPE_SPAN % half == 0
    inv = ROPE_THETA ** (-jnp.arange(0, rope_dim, 2, dtype=F32) / rope_dim)
    ang = jnp.arange(seq, dtype=F32)[:, None] * inv[None, :]
    cos, sin = jnp.cos(ang), jnp.sin(ang)
    lane = np.arange(LANES)
    is_lo = lane < ROPE_SPAN
    is_hi = (lane >= LANES // 2) & (lane < LANES // 2 + ROPE_SPAN)
    idx = (lane % (LANES // 2)) % half
    cos_t = jnp.where(is_lo | is_hi, cos[:, idx], 1.0)
    sin_t = jnp.where(is_lo, -sin[:, idx], jnp.where(is_hi, sin[:, idx], 0.0))
    return jnp.stack([cos_t, sin_t]).astype(F32)


def _layer_weights(l, p):
    row = lambda v: v.reshape(1, -1).astype(F32)
    w_in = p['w_in'][l]
    d = w_in.shape[0]
    c_kr = 3 * A_WIDTH + Q_LORA + KV_LORA
    half_r = B_ROPE_DIM // 2
    zeros = lambda n: jnp.zeros((d, n), F32)
    kr = w_in[:, c_kr:]
    win = jnp.concatenate([
        w_in[:, :A_WIDTH][:, _A_PERM_ALL], w_in[:, A_WIDTH:2 * A_WIDTH][:, _A_PERM_ALL], w_in[:, 2 * A_WIDTH:c_kr],
        kr[:, :half_r], zeros(LANES // 2 - half_r), kr[:, half_r:], zeros(LANES // 2 - half_r)], axis=1)

    def b_lanes(a, with_rope):
        if with_rope:
            a = a[..., _B_PERM]
        else:
            z = jnp.zeros(a.shape[:-1] + (half_r,), a.dtype)
            a = jnp.concatenate([z, a[..., :_B_SPLIT], z, a[..., _B_SPLIT:]], axis=-1)
        return jnp.pad(a, [(0, 0)] * (a.ndim - 1) + [(0, LANES - B_QK_DIM)])

    wukv = p['w_ukv'][l].reshape(KV_LORA, B_HEADS, B_NOPE_DIM + B_V_DIM)
    b_gain = lambda g: jnp.tile(b_lanes(g, True), B_HEADS)
    a_gain = lambda g: jnp.tile(g, A_HEADS)[_A_PERM_ALL]
    w_out = p['w_out'][l]
    return {
        'gmix': row(p['norm_mix_g'][l]),
        'win': win.astype(BF16),
        'gqa': row(a_gain(p['a_q_norm_g'][l]) * (A_HEAD_DIM ** -0.5)),
        'gka': row(a_gain(p['a_k_norm_g'][l])),
        'gcq': row(p['b_cq_norm_g'][l]),
        'gckv': row(p['b_ckv_norm_g'][l]),
        'wuq': b_lanes(p['w_uq'][l].reshape(Q_LORA, B_HEADS, B_QK_DIM), True).reshape(Q_LORA, B_QK_PAD).astype(BF16),
        'wuk': b_lanes(wukv[:, :, :B_NOPE_DIM], False).reshape(KV_LORA, B_QK_PAD).astype(BF16),
        'wuv': wukv[:, :, B_NOPE_DIM:].reshape(KV_LORA, B_WIDTH).astype(BF16),
        'gqb': row(b_gain(p['b_q_norm_g'][l]) * (B_QK_DIM ** -0.5 * LOG2_E)),
        'gkb': row(b_gain(p['b_k_norm_g'][l])),
        'ga_out': row(p['a_out_norm_g'][l]),
        'gb_out': row(p['b_out_norm_g'][l]),
        'wout_a': w_out[:A_WIDTH].astype(BF16),
        'wout_b': w_out[A_WIDTH:].astype(BF16),
        'gmem': row(p['norm_mem_g'][l]),
        'gmem_kv': row(p['mem_kv_norm_g'][l]),
        'wq_m': p['m_wq'][l].astype(BF16),
        'wkv_m': p['m_wkv'][l].astype(BF16),
        'gq_m': row(p['m_q_norm_g'][l] * (M_HEAD_DIM ** -0.5)),
        'gk_m': row(p['m_k_norm_g'][l]),
        'wo_m': p['m_wo'][l].astype(BF16),
        'gffn': row(p['norm_ffn_g'][l]),
        'w1': p['w_ff1'][l].astype(BF16),
        'w2': p['w_ff2'][l].astype(BF16),
    }


def _tiles(seq):
    tm = 512 if seq % 512 == 0 else 256
    return dict(tm=tm, tq=256, tk=512)


def _trunk(x, mem, layers):
    b, s, d = x.shape
    assert s % (max(DILATIONS) * DIL_TQ) == 0 and s // max(DILATIONS) >= DIL_WIN
    t = _tiles(s)
    taba = _rope_tables(s, A_ROPE_DIM)
    tabb = _rope_tables(s, B_ROPE_DIM)
    x2d = x.reshape(b * s, d)
    for w in layers:
        w = dict(w, taba=taba, tabb=tabb)
        qa, ka, va, qb, kb, vbt = _proj(x2d, s, w, t['tm'])
        as3d = lambda a: a.reshape(b, s, a.shape[-1])
        oa = _dilated(as3d(qa), as3d(ka), as3d(va))
        ob = _mla(as3d(qb), as3d(kb), vbt, t['tq'], t['tk'])
        km, vm = _mem_kv(mem, w)
        x2d = _post(as3d(x2d), as3d(oa), as3d(ob), km, vm, w, t['tm']).reshape(b * s, d)
    return x2d.reshape(b, s, d)


def kernel(x_prompt, x_sample, mem_prompt, mem_sample, norm_mix_g, w_in, a_q_norm_g, a_k_norm_g, b_cq_norm_g, b_ckv_norm_g, w_uq, w_ukv, b_q_norm_g, b_k_norm_g, a_out_norm_g, b_out_norm_g, w_out, norm_mem_g, mem_kv_norm_g, m_wq, m_wkv, m_q_norm_g, m_k_norm_g, m_wo, norm_ffn_g, w_ff1, w_ff2):
    p = dict(norm_mix_g=norm_mix_g, w_in=w_in, a_q_norm_g=a_q_norm_g, a_k_norm_g=a_k_norm_g,
             b_cq_norm_g=b_cq_norm_g, b_ckv_norm_g=b_ckv_norm_g, w_uq=w_uq, w_ukv=w_ukv,
             b_q_norm_g=b_q_norm_g, b_k_norm_g=b_k_norm_g, a_out_norm_g=a_out_norm_g,
             b_out_norm_g=b_out_norm_g, w_out=w_out, norm_mem_g=norm_mem_g, mem_kv_norm_g=mem_kv_norm_g,
             m_wq=m_wq, m_wkv=m_wkv, m_q_norm_g=m_q_norm_g, m_k_norm_g=m_k_norm_g, m_wo=m_wo,
             norm_ffn_g=norm_ffn_g, w_ff1=w_ff1, w_ff2=w_ff2)
    layers = [_layer_weights(l, p) for l in range(w_in.shape[0])]
    return (_trunk(x_prompt, mem_prompt, layers), _trunk(x_sample, mem_sample, layers))
```

```python
import functools

import numpy as np
import jax
import jax.numpy as jnp
from jax import lax
from jax.experimental import pallas as pl
from jax.experimental.pallas import tpu as pltpu

EPS = 1e-6
ROPE_THETA = 500000.0

LANES = 128
A_HEADS = 8
A_HEAD_DIM = 64
A_WIDTH = A_HEADS * A_HEAD_DIM
A_ROPE_DIM = 16
A_HALF_WINDOW = 64
DILATIONS = (16, 4, 1)

B_HEADS = 8
B_NOPE_DIM = 64
B_ROPE_DIM = 32
B_QK_DIM = B_NOPE_DIM + B_ROPE_DIM
B_V_DIM = 64
B_WIDTH = B_HEADS * B_V_DIM
B_QK_PAD = B_HEADS * LANES
Q_LORA = 256
KV_LORA = 128

M_HEADS = 4
M_HEAD_DIM = 128
M_WIDTH = M_HEADS * M_HEAD_DIM

VMEM_LIMIT = 56 * 1024 * 1024

BF16 = jnp.bfloat16
F32 = jnp.float32
NEG_BIG = -1e30
LOG2_E = float(np.log2(np.e))


def _a_pair_perm():
    heads = [np.arange(A_HEAD_DIM), A_HEAD_DIM + np.arange(A_HEAD_DIM)]
    r = A_ROPE_DIM // 2
    rest = (A_HEAD_DIM - A_ROPE_DIM) // 2
    first = [h[:r] for h in heads] + [h[A_ROPE_DIM:A_ROPE_DIM + rest] for h in heads]
    second = [h[r:A_ROPE_DIM] for h in heads] + [h[A_ROPE_DIM + rest:] for h in heads]
    return np.concatenate(first + second)


ROPE_SPAN = 16
_A_PERM = _a_pair_perm()
_A_PERM_ALL = np.concatenate([c * LANES + _A_PERM for c in range(A_WIDTH // LANES)])
_B_SPLIT = LANES // 2 - B_ROPE_DIM // 2
_B_PERM = np.concatenate([np.arange(B_NOPE_DIM, B_NOPE_DIM + B_ROPE_DIM // 2), np.arange(0, _B_SPLIT),
                          np.arange(B_NOPE_DIM + B_ROPE_DIM // 2, B_QK_DIM), np.arange(_B_SPLIT, B_NOPE_DIM)])


def _lane_mask(member):
    lane = lax.broadcasted_iota(jnp.int32, (1, LANES), 1)
    edges = np.flatnonzero(np.diff(np.concatenate([[0], member.astype(np.int8), [0]])))
    mask = None
    for lo, hi in zip(edges[::2], edges[1::2]):
        run = (lane >= int(lo)) & (lane < int(hi))
        mask = run if mask is None else (mask | run)
    return mask


def _params(n_axes):
    return pltpu.CompilerParams(dimension_semantics=("arbitrary",) * n_axes,
                                vmem_limit_bytes=VMEM_LIMIT)


def _row_rms_scale(x, width):
    return lax.rsqrt(jnp.sum(x * x, axis=-1, keepdims=True) * (1.0 / width) + EPS)


def _dot(a, b):
    return jnp.dot(a, b, preferred_element_type=F32)


def _dot_nt(a, b):
    return lax.dot_general(a, b, (((1,), (1,)), ((), ())), preferred_element_type=F32)


def _rope_chunk(xc, cos, sin):
    return xc * cos + pltpu.roll(xc, LANES // 2, axis=1) * sin


def _proj_kernel(x_ref, gmix_ref, win_ref, gqa_ref, gka_ref, gcq_ref, gckv_ref,
                 wuq_ref, wuk_ref, wuv_ref, gqb_ref, gkb_ref, taba_ref, tabb_ref,
                 qa_ref, ka_ref, va_ref, qb_ref, kb_ref, vbt_ref):
    x = x_ref[...]
    h = (x * _row_rms_scale(x, x.shape[-1]) * gmix_ref[...]).astype(BF16)
    proj = _dot(h, win_ref[...])

    first_head = _lane_mask(_A_PERM < A_HEAD_DIM)
    cos_a, sin_a = taba_ref[0], taba_ref[1]
    cos_b, sin_b = tabb_ref[0], tabb_ref[1]

    def a_heads(col0, g_ref, out_ref):
        for c in range(A_WIDTH // LANES):
            xc = proj[:, col0 + c * LANES: col0 + (c + 1) * LANES]
            sq = xc * xc
            s_0 = jnp.sum(jnp.where(first_head, sq, 0.0), axis=-1, keepdims=True)
            s_1 = jnp.sum(jnp.where(first_head, 0.0, sq), axis=-1, keepdims=True)
            rs = jnp.where(first_head,
                           lax.rsqrt(s_0 * (1.0 / A_HEAD_DIM) + EPS),
                           lax.rsqrt(s_1 * (1.0 / A_HEAD_DIM) + EPS))
            y = xc * rs * g_ref[:, c * LANES:(c + 1) * LANES]
            y = _rope_chunk(y, cos_a, sin_a)
            out_ref[:, c * LANES:(c + 1) * LANES] = y.astype(out_ref.dtype)

    a_heads(0, gqa_ref, qa_ref)
    a_heads(A_WIDTH, gka_ref, ka_ref)
    va_ref[...] = proj[:, 2 * A_WIDTH:3 * A_WIDTH].astype(va_ref.dtype)

    c0 = 3 * A_WIDTH
    cq = proj[:, c0:c0 + Q_LORA]
    ckv = proj[:, c0 + Q_LORA:c0 + Q_LORA + KV_LORA]
    kr = proj[:, c0 + Q_LORA + KV_LORA:c0 + Q_LORA + 2 * KV_LORA]
    cqn = (cq * _row_rms_scale(cq, Q_LORA) * gcq_ref[...]).astype(BF16)
    ckvn = (ckv * _row_rms_scale(ckv, KV_LORA) * gckv_ref[...]).astype(BF16)
    qb = _dot(cqn, wuq_ref[...])
    kn = _dot(ckvn, wuk_ref[...])
    vbt_ref[...] = _dot(ckvn, wuv_ref[...]).T.astype(vbt_ref.dtype)

    def b_heads(get_chunk, g_ref, out_ref):
        for c in range(B_HEADS):
            xc = get_chunk(c)
            rs = lax.rsqrt(jnp.sum(xc * xc, axis=-1, keepdims=True) * (1.0 / B_QK_DIM) + EPS)
            y = xc * rs * g_ref[:, c * LANES:(c + 1) * LANES]
            y = _rope_chunk(y, cos_b, sin_b)
            out_ref[:, c * LANES:(c + 1) * LANES] = y.astype(out_ref.dtype)

    b_heads(lambda c: qb[:, c * LANES:(c + 1) * LANES], gqb_ref, qb_ref)
    b_heads(lambda c: kn[:, c * LANES:(c + 1) * LANES] + kr, gkb_ref, kb_ref)


def _proj(x2d, seq, w, tm):
    n, d = x2d.shape
    nblk_seq = seq // tm
    row = lambda i: (i, 0)
    const = lambda i: (0, 0)
    tab = lambda i: (0, i % nblk_seq, 0)
    full = lambda a: pl.BlockSpec(a.shape, const)
    widths = [A_WIDTH, A_WIDTH, A_WIDTH, B_QK_PAD, B_QK_PAD]
    return pl.pallas_call(
        _proj_kernel,
        grid=(n // tm,),
        in_specs=[pl.BlockSpec((tm, d), row), full(w['gmix']), full(w['win']), full(w['gqa']), full(w['gka']),
                  full(w['gcq']), full(w['gckv']), full(w['wuq']), full(w['wuk']), full(w['wuv']),
                  full(w['gqb']), full(w['gkb']),
                  pl.BlockSpec((2, tm, LANES), tab), pl.BlockSpec((2, tm, LANES), tab)],
        out_specs=[pl.BlockSpec((tm, wd), row) for wd in widths]
                  + [pl.BlockSpec((B_WIDTH, tm), lambda i: (0, i))],
        out_shape=[jax.ShapeDtypeStruct((n, wd), F32 if wd == A_WIDTH else BF16) for wd in widths]
                  + [jax.ShapeDtypeStruct((B_WIDTH, n), BF16)],
        compiler_params=_params(1),
        name="proj",
    )(x2d, w['gmix'], w['win'], w['gqa'], w['gka'], w['gcq'], w['gckv'], w['wuq'], w['wuk'], w['wuv'],
      w['gqb'], w['gkb'], w['taba'], w['tabb'])


DIL_TQ = 128
DIL_WIN = DIL_TQ + 2 * A_HALF_WINDOW
DIL_SLOTS = 4


def _strided_rows(start, size, stride):
    return pl.ds(start, size) if stride == 1 else pl.ds(start, size, stride=stride)


def _dilated_kernel(q_ref, k_ref, v_ref, o_ref, lse_scr, s_scr, bias_scr):
    seq = q_ref.shape[1]
    lane = lax.broadcasted_iota(jnp.int32, (1, LANES), 1)
    low_head = lane < A_HEAD_DIM
    q_first = _lane_mask(_A_PERM < A_HEAD_DIM)
    rel = (lax.broadcasted_iota(jnp.int32, (2 * DIL_TQ, DIL_WIN), 1)
           - lax.broadcasted_iota(jnp.int32, (2 * DIL_TQ, DIL_WIN), 0) % DIL_TQ)
    for place in range(3):
        bias_scr[place] = jnp.where(jnp.abs(rel - place * A_HALF_WINDOW) <= A_HALF_WINDOW, 0.0, NEG_BIG)

    for dil in DILATIONS:
        sub_len = seq // dil
        n_blk = sub_len // DIL_TQ
        n_total = dil * n_blk

        def coords(idx, dil=dil, sub_len=sub_len, n_blk=n_blk):
            r = idx // n_blk
            q0 = (idx % n_blk) * DIL_TQ
            k0 = jnp.clip(q0 - A_HALF_WINDOW, 0, sub_len - DIL_WIN)
            rows_q = _strided_rows(r + dil * q0, DIL_TQ, dil)
            rows_k = _strided_rows(r + dil * k0, DIL_WIN, dil)
            return rows_q, rows_k, (q0 - k0) // A_HALF_WINDOW

        def scores_into(slot, idx, coords=coords):
            rows_q, rows_k, place = coords(idx)
            qp = q_ref[0, rows_q, :]
            kw = k_ref[0, rows_k, :].astype(BF16)
            q2 = jnp.concatenate([jnp.where(q_first, qp, 0.0), jnp.where(q_first, 0.0, qp)], axis=0).astype(BF16)
            s_scr[slot] = _dot_nt(q2, kw) + bias_scr[place]

        def finish(slot, idx, dil=dil, coords=coords):
            rows_q, rows_k, _ = coords(idx)
            vw = v_ref[0, rows_k, :].astype(BF16)
            s = s_scr[slot]
            m = jnp.max(s, axis=-1, keepdims=True)
            pr = jnp.exp(s - m)
            den = jnp.sum(pr, axis=-1, keepdims=True)
            o2 = _dot(pr.astype(BF16), vw) / den
            lse2 = m + jnp.log(den)
            o = jnp.where(low_head, o2[:DIL_TQ], o2[DIL_TQ:])
            lse = jnp.where(low_head, lse2[:DIL_TQ], lse2[DIL_TQ:])
            if dil != DILATIONS[0]:
                o_prev, lse_prev = o_ref[0, rows_q, :], lse_scr[rows_q, :]
                top = jnp.maximum(lse_prev, lse)
                w_prev, w_new = jnp.exp(lse_prev - top), jnp.exp(lse - top)
                tot = w_prev + w_new
                o = (w_prev * o_prev + w_new * o) / tot
                lse = top + jnp.log(tot)
            o_ref[0, rows_q, :] = o
            if dil != DILATIONS[-1]:
                lse_scr[rows_q, :] = lse

        assert n_total % DIL_SLOTS == 0
        for slot in range(DIL_SLOTS - 1):
            scores_into(slot, slot)

        def step(i, carry, scores_into=scores_into, finish=finish, n_total=n_total):
            for slot in range(DIL_SLOTS):
                idx = DIL_SLOTS * i + slot
                ahead = idx + DIL_SLOTS - 1
                scores_into((slot + DIL_SLOTS - 1) % DIL_SLOTS, jnp.where(ahead >= n_total, ahead - n_total, ahead))
                finish(slot, idx)
            return carry

        lax.fori_loop(0, n_total // DIL_SLOTS, step, 0)


def _dilated(qa, ka, va):
    b, s, w = qa.shape
    spec = pl.BlockSpec((1, s, LANES), lambda bi, p: (bi, 0, p))
    out = pl.pallas_call(
        _dilated_kernel,
        grid=(b, w // LANES),
        in_specs=[spec, spec, spec],
        out_specs=spec,
        out_shape=jax.ShapeDtypeStruct((b, s, w), F32),
        scratch_shapes=[pltpu.VMEM((s, LANES), F32),
                        pltpu.VMEM((DIL_SLOTS, 2 * DIL_TQ, DIL_WIN), F32),
                        pltpu.VMEM((3, 2 * DIL_TQ, DIL_WIN), F32)],
        compiler_params=_params(2),
        name="dilated",
    )(qa, ka, va)
    return out.reshape(b * s, w)


MLA_DEN_ROWS = 16
MLA_SLOTS = 8
MLA_AHEAD = 2


def _mla_kernel(q_ref, k_ref, vt_ref, o_ref, s_scr, smax_scr, *, tq, tk):
    seq = k_ref.shape[1]
    heads = range(2)
    n_blocks = seq // tk
    n_tiles = seq // tq
    assert seq % (MLA_SLOTS * tk) == 0 and seq % tq == 0

    def scores_into(slot, tile, blk):
        q0 = pl.multiple_of(tile * tq, tq)
        k0 = pl.multiple_of(blk * tk, tk)
        for hh in heads:
            cols = slice(hh * LANES, (hh + 1) * LANES)
            st = _dot_nt(k_ref[0, pl.ds(k0, tk), cols], q_ref[0, pl.ds(q0, tq), cols])
            s_scr[slot, hh] = st
            smax_scr[slot, hh] = jnp.max(st.reshape(tk // 8, 8, tq), axis=0)

    for blk in range(MLA_AHEAD):
        scores_into(blk, 0, blk)

    def tile_body(tile, carry):
        def body(t, state):
            state = list(state)
            for slot in range(MLA_SLOTS):
                blk = MLA_SLOTS * t + slot
                wrap = blk + MLA_AHEAD >= n_blocks
                scores_into((slot + MLA_AHEAD) % MLA_SLOTS,
                            jnp.where(wrap, jnp.minimum(tile + 1, n_tiles - 1), tile),
                            jnp.where(wrap, blk + MLA_AHEAD - n_blocks, blk + MLA_AHEAD))
                k0 = pl.multiple_of(blk * tk, tk)
                for hh in heads:
                    m, acc = state[hh]
                    m_new = jnp.maximum(m, jnp.max(smax_scr[slot, hh], axis=0, keepdims=True))
                    alpha = jnp.exp2(m - m_new)
                    pt = jnp.exp2(s_scr[slot, hh] - m_new).astype(BF16)
                    vt1 = jnp.concatenate([vt_ref[hh * B_V_DIM:(hh + 1) * B_V_DIM, pl.ds(k0, tk)],
                                           jnp.ones((MLA_DEN_ROWS, tk), BF16)], axis=0)
                    acc = alpha * acc + _dot(vt1, pt)
                    state[hh] = (m_new, acc)
            return tuple(state)

        init = tuple((jnp.full((1, tq), NEG_BIG, F32), jnp.zeros((B_V_DIM + MLA_DEN_ROWS, tq), F32))
                     for _ in heads)
        final = lax.fori_loop(0, n_blocks // MLA_SLOTS, body, init)
        ot = jnp.concatenate([acc[:B_V_DIM] / acc[B_V_DIM:B_V_DIM + 1] for _, acc in final], axis=0)
        o_ref[0, pl.ds(pl.multiple_of(tile * tq, tq), tq), :] = ot.T
        return carry

    lax.fori_loop(0, n_tiles, tile_body, 0)


def _mla(qb, kb, vbt, tq, tk):
    b, s, _ = qb.shape
    qk_spec = pl.BlockSpec((1, s, 2 * LANES), lambda bi, p: (bi, 0, p))
    out = pl.pallas_call(
        functools.partial(_mla_kernel, tq=tq, tk=tk),
        grid=(b, B_HEADS // 2),
        in_specs=[qk_spec, qk_spec, pl.BlockSpec((LANES, s), lambda bi, p: (p, bi))],
        out_specs=pl.BlockSpec((1, s, LANES), lambda bi, p: (bi, 0, p)),
        out_shape=jax.ShapeDtypeStruct((b, s, B_WIDTH), F32),
        scratch_shapes=[pltpu.VMEM((MLA_SLOTS, 2, tk, tq), F32), pltpu.VMEM((MLA_SLOTS, 2, 8, tq), F32)],
        compiler_params=_params(2),
        name="mla",
    )(qb, kb, vbt)
    return out.reshape(b * s, B_WIDTH)


def _mem_kv_kernel(mem_ref, g_ref, wkv_ref, gk_ref, k_ref, v_ref):
    mem = mem_ref[0]
    mm = (mem * _row_rms_scale(mem, mem.shape[-1]) * g_ref[...]).astype(BF16)
    kv = _dot(mm, wkv_ref[...])
    for hd in range(M_HEADS):
        cols = slice(hd * M_HEAD_DIM, (hd + 1) * M_HEAD_DIM)
        kh = kv[:, cols]
        k_ref[0, :, cols] = (kh * _row_rms_scale(kh, M_HEAD_DIM) * gk_ref[...]).astype(k_ref.dtype)
    v_ref[0] = kv[:, M_WIDTH:].astype(v_ref.dtype)


def _mem_kv(mem, w):
    b, n_mem, d = mem.shape
    const = lambda bi: (0, 0)
    full = lambda a: pl.BlockSpec(a.shape, const)
    blk = pl.BlockSpec((1, n_mem, M_WIDTH), lambda bi: (bi, 0, 0))
    return pl.pallas_call(
        _mem_kv_kernel,
        grid=(b,),
        in_specs=[pl.BlockSpec((1, n_mem, d), lambda bi: (bi, 0, 0)),
                  full(w['gmem_kv']), full(w['wkv_m']), full(w['gk_m'])],
        out_specs=[blk, blk],
        out_shape=[jax.ShapeDtypeStruct((b, n_mem, M_WIDTH), BF16)] * 2,
        compiler_params=_params(1),
        name="mem_kv",
    )(mem, w['gmem_kv'], w['wkv_m'], w['gk_m'])


FFN_CHUNK = 1024


def _post_kernel(x_ref, oa_ref, ob_ref, ga_ref, gb_ref, wa_ref, wb_ref,
                 gmem_ref, wq_ref, gq_ref, k_ref, v_ref, wo_ref, gffn_ref, w1_ref, w2_ref, y_ref):
    oa, ob = oa_ref[0], ob_ref[0]
    na = (oa * _row_rms_scale(oa, A_WIDTH) * ga_ref[...]).astype(BF16)
    nb = (ob * _row_rms_scale(ob, B_WIDTH) * gb_ref[...]).astype(BF16)
    x = x_ref[0] + _dot(na, wa_ref[...]) + _dot(nb, wb_ref[...])

    h = (x * _row_rms_scale(x, x.shape[-1]) * gmem_ref[...]).astype(BF16)
    q = _dot(h, wq_ref[...])
    heads = []
    for hd in range(M_HEADS):
        cols = slice(hd * M_HEAD_DIM, (hd + 1) * M_HEAD_DIM)
        qh = q[:, cols]
        qh = (qh * _row_rms_scale(qh, M_HEAD_DIM) * gq_ref[...]).astype(BF16)
        s = _dot_nt(qh, k_ref[0, :, cols])
        pr = jnp.exp(s - jnp.max(s, axis=-1, keepdims=True))
        pr = pr / jnp.sum(pr, axis=-1, keepdims=True)
        heads.append(_dot(pr.astype(BF16), v_ref[0, :, cols]).astype(BF16))
    x = x + _dot(jnp.concatenate(heads, axis=-1), wo_ref[...])

    h = (x * _row_rms_scale(x, x.shape[-1]) * gffn_ref[...]).astype(BF16)
    for c in range(w1_ref.shape[1] // FFN_CHUNK):
        cols = slice(c * FFN_CHUNK, (c + 1) * FFN_CHUNK)
        a = jnp.maximum(_dot(h, w1_ref[:, cols]), 0.0)
        x = x + _dot((a * a).astype(BF16), w2_ref[cols, :])
    y_ref[0] = x


def _post(x3d, oa, ob, km, vm, w, tm):
    b, s, d = x3d.shape
    n_mem = km.shape[1]
    const = lambda bi, i: (0, 0)
    full = lambda a: pl.BlockSpec(a.shape, const, pipeline_mode=pl.Buffered(1))
    tile = lambda width: pl.BlockSpec((1, tm, width), lambda bi, i: (bi, i, 0))
    kvblk = pl.BlockSpec((1, n_mem, M_WIDTH), lambda bi, i: (bi, 0, 0))
    names = ['ga_out', 'gb_out', 'wout_a', 'wout_b', 'gmem', 'wq_m', 'gq_m']
    tail = ['wo_m', 'gffn', 'w1', 'w2']
    return pl.pallas_call(
        _post_kernel,
        grid=(b, s // tm),
        in_specs=[tile(d), tile(A_WIDTH), tile(B_WIDTH)] + [full(w[n]) for n in names] + [kvblk, kvblk]
                 + [full(w[n]) for n in tail],
        out_specs=tile(d),
        out_shape=jax.ShapeDtypeStruct((b, s, d), F32),
        compiler_params=_params(2),
        name="post",
    )(x3d, oa, ob, *[w[n] for n in names], km, vm, *[w[n] for n in tail])


def _rope_tables(seq, rope_dim):
    half = rope_dim // 2
    assert ROPE_SPAN % half == 0
    inv = ROPE_THETA ** (-jnp.arange(0, rope_dim, 2, dtype=F32) / rope_dim)
    ang = jnp.arange(seq, dtype=F32)[:, None] * inv[None, :]
    cos, sin = jnp.cos(ang), jnp.sin(ang)
    lane = np.arange(LANES)
    is_lo = lane < ROPE_SPAN
    is_hi = (lane >= LANES // 2) & (lane < LANES // 2 + ROPE_SPAN)
    idx = (lane % (LANES // 2)) % half
    cos_t = jnp.where(is_lo | is_hi, cos[:, idx], 1.0)
    sin_t = jnp.where(is_lo, -sin[:, idx], jnp.where(is_hi, sin[:, idx], 0.0))
    return jnp.stack([cos_t, sin_t]).astype(F32)


def _layer_weights(l, p):
    row = lambda v: v.reshape(1, -1).astype(F32)
    w_in = p['w_in'][l]
    d = w_in.shape[0]
    c_kr = 3 * A_WIDTH + Q_LORA + KV_LORA
    half_r = B_ROPE_DIM // 2
    zeros = lambda n: jnp.zeros((d, n), F32)
    kr = w_in[:, c_kr:]
    win = jnp.concatenate([
        w_in[:, :A_WIDTH][:, _A_PERM_ALL], w_in[:, A_WIDTH:2 * A_WIDTH][:, _A_PERM_ALL], w_in[:, 2 * A_WIDTH:c_kr],
        kr[:, :half_r], zeros(LANES // 2 - half_r), kr[:, half_r:], zeros(LANES // 2 - half_r)], axis=1)

    def b_lanes(a, with_rope):
        if with_rope:
            a = a[..., _B_PERM]
        else:
            z = jnp.zeros(a.shape[:-1] + (half_r,), a.dtype)
            a = jnp.concatenate([z, a[..., :_B_SPLIT], z, a[..., _B_SPLIT:]], axis=-1)
        return jnp.pad(a, [(0, 0)] * (a.ndim - 1) + [(0, LANES - B_QK_DIM)])

    wukv = p['w_ukv'][l].reshape(KV_LORA, B_HEADS, B_NOPE_DIM + B_V_DIM)
    b_gain = lambda g: jnp.tile(b_lanes(g, True), B_HEADS)
    a_gain = lambda g: jnp.tile(g, A_HEADS)[_A_PERM_ALL]
    w_out = p['w_out'][l]
    return {
        'gmix': row(p['norm_mix_g'][l]),
        'win': win.astype(BF16),
        'gqa': row(a_gain(p['a_q_norm_g'][l]) * (A_HEAD_DIM ** -0.5)),
        'gka': row(a_gain(p['a_k_norm_g'][l])),
        'gcq': row(p['b_cq_norm_g'][l]),
        'gckv': row(p['b_ckv_norm_g'][l]),
        'wuq': b_lanes(p['w_uq'][l].reshape(Q_LORA, B_HEADS, B_QK_DIM), True).reshape(Q_LORA, B_QK_PAD).astype(BF16),
        'wuk': b_lanes(wukv[:, :, :B_NOPE_DIM], False).reshape(KV_LORA, B_QK_PAD).astype(BF16),
        'wuv': wukv[:, :, B_NOPE_DIM:].reshape(KV_LORA, B_WIDTH).astype(BF16),
        'gqb': row(b_gain(p['b_q_norm_g'][l]) * (B_QK_DIM ** -0.5 * LOG2_E)),
        'gkb': row(b_gain(p['b_k_norm_g'][l])),
        'ga_out': row(p['a_out_norm_g'][l]),
        'gb_out': row(p['b_out_norm_g'][l]),
        'wout_a': w_out[:A_WIDTH].astype(BF16),
        'wout_b': w_out[A_WIDTH:].astype(BF16),
        'gmem': row(p['norm_mem_g'][l]),
        'gmem_kv': row(p['mem_kv_norm_g'][l]),
        'wq_m': p['m_wq'][l].astype(BF16),
        'wkv_m': p['m_wkv'][l].astype(BF16),
        'gq_m': row(p['m_q_norm_g'][l] * (M_HEAD_DIM ** -0.5)),
        'gk_m': row(p['m_k_norm_g'][l]),
        'wo_m': p['m_wo'][l].astype(BF16),
        'gffn': row(p['norm_ffn_g'][l]),
        'w1': p['w_ff1'][l].astype(BF16),
        'w2': p['w_ff2'][l].astype(BF16),
    }


def _tiles(seq):
    tm = 512 if seq % 512 == 0 else 256
    return dict(tm=tm, tq=256, tk=512)


def _trunk(x, mem, layers):
    b, s, d = x.shape
    assert s % (max(DILATIONS) * DIL_TQ) == 0 and s // max(DILATIONS) >= DIL_WIN
    t = _tiles(s)
    taba = _rope_tables(s, A_ROPE_DIM)
    tabb = _rope_tables(s, B_ROPE_DIM)
    x2d = x.reshape(b * s, d)
    for w in layers:
        w = dict(w, taba=taba, tabb=tabb)
        qa, ka, va, qb, kb, vbt = _proj(x2d, s, w, t['tm'])
        as3d = lambda a: a.reshape(b, s, a.shape[-1])
        oa = _dilated(as3d(qa), as3d(ka), as3d(va))
        ob = _mla(as3d(qb), as3d(kb), vbt, t['tq'], t['tk'])
        km, vm = _mem_kv(mem, w)
        x2d = _post(as3d(x2d), as3d(oa), as3d(ob), km, vm, w, t['tm']).reshape(b * s, d)
    return x2d.reshape(b, s, d)


def kernel(x_prompt, x_sample, mem_prompt, mem_sample, norm_mix_g, w_in, a_q_norm_g, a_k_norm_g, b_cq_norm_g, b_ckv_norm_g, w_uq, w_ukv, b_q_norm_g, b_k_norm_g, a_out_norm_g, b_out_norm_g, w_out, norm_mem_g, mem_kv_norm_g, m_wq, m_wkv, m_q_norm_g, m_k_norm_g, m_wo, norm_ffn_g, w_ff1, w_ff2):
    p = dict(norm_mix_g=norm_mix_g, w_in=w_in, a_q_norm_g=a_q_norm_g, a_k_norm_g=a_k_norm_g,
             b_cq_norm_g=b_cq_norm_g, b_ckv_norm_g=b_ckv_norm_g, w_uq=w_uq, w_ukv=w_ukv,
             b_q_norm_g=b_q_norm_g, b_k_norm_g=b_k_norm_g, a_out_norm_g=a_out_norm_g,
             b_out_norm_g=b_out_norm_g, w_out=w_out, norm_mem_g=norm_mem_g, mem_kv_norm_g=mem_kv_norm_g,
             m_wq=m_wq, m_wkv=m_wkv, m_q_norm_g=m_q_norm_g, m_k_norm_g=m_k_norm_g, m_wo=m_wo,
             norm_ffn_g=norm_ffn_g, w_ff1=w_ff1, w_ff2=w_ff2)
    layers = [_layer_weights(l, p) for l in range(w_in.shape[0])]
    return (_trunk(x_prompt, mem_prompt, layers), _trunk(x_sample, mem_sample, layers))
```

```python
import functools

import numpy as np
import jax
import jax.numpy as jnp
from jax import lax
from jax.experimental import pallas as pl
from jax.experimental.pallas import tpu as pltpu

EPS = 1e-6
ROPE_THETA = 500000.0

LANES = 128
A_HEADS = 8
A_HEAD_DIM = 64
A_WIDTH = A_HEADS * A_HEAD_DIM
A_ROPE_DIM = 16
A_HALF_WINDOW = 64
DILATIONS = (16, 4, 1)

B_HEADS = 8
B_NOPE_DIM = 64
B_ROPE_DIM = 32
B_QK_DIM = B_NOPE_DIM + B_ROPE_DIM
B_V_DIM = 64
B_WIDTH = B_HEADS * B_V_DIM
B_QK_PAD = B_HEADS * LANES
Q_LORA = 256
KV_LORA = 128

M_HEADS = 4
M_HEAD_DIM = 128
M_WIDTH = M_HEADS * M_HEAD_DIM

VMEM_LIMIT = 56 * 1024 * 1024

BF16 = jnp.bfloat16
F32 = jnp.float32
NEG_BIG = -1e30
LOG2_E = float(np.log2(np.e))


def _a_pair_perm():
    heads = [np.arange(A_HEAD_DIM), A_HEAD_DIM + np.arange(A_HEAD_DIM)]
    r = A_ROPE_DIM // 2
    rest = (A_HEAD_DIM - A_ROPE_DIM) // 2
    first = [h[:r] for h in heads] + [h[A_ROPE_DIM:A_ROPE_DIM + rest] for h in heads]
    second = [h[r:A_ROPE_DIM] for h in heads] + [h[A_ROPE_DIM + rest:] for h in heads]
    return np.concatenate(first + second)


ROPE_SPAN = 16
_A_PERM = _a_pair_perm()
_A_PERM_ALL = np.concatenate([c * LANES + _A_PERM for c in range(A_WIDTH // LANES)])
_B_SPLIT = LANES // 2 - B_ROPE_DIM // 2
_B_PERM = np.concatenate([np.arange(B_NOPE_DIM, B_NOPE_DIM + B_ROPE_DIM // 2), np.arange(0, _B_SPLIT),
                          np.arange(B_NOPE_DIM + B_ROPE_DIM // 2, B_QK_DIM), np.arange(_B_SPLIT, B_NOPE_DIM)])


def _lane_mask(member):
    lane = lax.broadcasted_iota(jnp.int32, (1, LANES), 1)
    edges = np.flatnonzero(np.diff(np.concatenate([[0], member.astype(np.int8), [0]])))
    mask = None
    for lo, hi in zip(edges[::2], edges[1::2]):
        run = (lane >= int(lo)) & (lane < int(hi))
        mask = run if mask is None else (mask | run)
    return mask


def _params(n_axes):
    return pltpu.CompilerParams(dimension_semantics=("arbitrary",) * n_axes,
                                vmem_limit_bytes=VMEM_LIMIT)


def _row_rms_scale(x, width):
    return lax.rsqrt(jnp.sum(x * x, axis=-1, keepdims=True) * (1.0 / width) + EPS)


def _dot(a, b):
    return jnp.dot(a, b, preferred_element_type=F32)


def _dot_nt(a, b):
    return lax.dot_general(a, b, (((1,), (1,)), ((), ())), preferred_element_type=F32)


def _rope_chunk(xc, cos, sin):
    return xc * cos + pltpu.roll(xc, LANES // 2, axis=1) * sin


def _proj_kernel(x_ref, gmix_ref, win_ref, gqa_ref, gka_ref, gcq_ref, gckv_ref,
                 wuq_ref, wuk_ref, wuv_ref, gqb_ref, gkb_ref, taba_ref, tabb_ref,
                 qa_ref, ka_ref, va_ref, qb_ref, kb_ref, vbt_ref):
    x = x_ref[...]
    h = (x * _row_rms_scale(x, x.shape[-1]) * gmix_ref[...]).astype(BF16)
    proj = _dot(h, win_ref[...])

    first_head = _lane_mask(_A_PERM < A_HEAD_DIM)
    cos_a, sin_a = taba_ref[0], taba_ref[1]
    cos_b, sin_b = tabb_ref[0], tabb_ref[1]

    def a_heads(col0, g_ref, out_ref):
        for c in range(A_WIDTH // LANES):
            xc = proj[:, col0 + c * LANES: col0 + (c + 1) * LANES]
            sq = xc * xc
            s_0 = jnp.sum(jnp.where(first_head, sq, 0.0), axis=-1, keepdims=True)
            s_1 = jnp.sum(jnp.where(first_head, 0.0, sq), axis=-1, keepdims=True)
            rs = jnp.where(first_head,
                           lax.rsqrt(s_0 * (1.0 / A_HEAD_DIM) + EPS),
                           lax.rsqrt(s_1 * (1.0 / A_HEAD_DIM) + EPS))
            y = xc * rs * g_ref[:, c * LANES:(c + 1) * LANES]
            y = _rope_chunk(y, cos_a, sin_a)
            out_ref[:, c * LANES:(c + 1) * LANES] = y.astype(out_ref.dtype)

    a_heads(0, gqa_ref, qa_ref)
    a_heads(A_WIDTH, gka_ref, ka_ref)
    va_ref[...] = proj[:, 2 * A_WIDTH:3 * A_WIDTH].astype(va_ref.dtype)

    c0 = 3 * A_WIDTH
    cq = proj[:, c0:c0 + Q_LORA]
    ckv = proj[:, c0 + Q_LORA:c0 + Q_LORA + KV_LORA]
    kr = proj[:, c0 + Q_LORA + KV_LORA:c0 + Q_LORA + 2 * KV_LORA]
    cqn = (cq * _row_rms_scale(cq, Q_LORA) * gcq_ref[...]).astype(BF16)
    ckvn = (ckv * _row_rms_scale(ckv, KV_LORA) * gckv_ref[...]).astype(BF16)
    qb = _dot(cqn, wuq_ref[...])
    kn = _dot(ckvn, wuk_ref[...])
    vbt_ref[...] = _dot(ckvn, wuv_ref[...]).T.astype(vbt_ref.dtype)

    def b_heads(get_chunk, g_ref, out_ref):
        for c in range(B_HEADS):
            xc = get_chunk(c)
            rs = lax.rsqrt(jnp.sum(xc * xc, axis=-1, keepdims=True) * (1.0 / B_QK_DIM) + EPS)
            y = xc * rs * g_ref[:, c * LANES:(c + 1) * LANES]
            y = _rope_chunk(y, cos_b, sin_b)
            out_ref[:, c * LANES:(c + 1) * LANES] = y.astype(out_ref.dtype)

    b_heads(lambda c: qb[:, c * LANES:(c + 1) * LANES], gqb_ref, qb_ref)
    b_heads(lambda c: kn[:, c * LANES:(c + 1) * LANES] + kr, gkb_ref, kb_ref)


def _proj(x2d, seq, w, tm):
    n, d = x2d.shape
    nblk_seq = seq // tm
    row = lambda i: (i, 0)
    const = lambda i: (0, 0)
    tab = lambda i: (0, i % nblk_seq, 0)
    full = lambda a: pl.BlockSpec(a.shape, const)
    widths = [A_WIDTH, A_WIDTH, A_WIDTH, B_QK_PAD, B_QK_PAD]
    return pl.pallas_call(
        _proj_kernel,
        grid=(n // tm,),
        in_specs=[pl.BlockSpec((tm, d), row), full(w['gmix']), full(w['win']), full(w['gqa']), full(w['gka']),
                  full(w['gcq']), full(w['gckv']), full(w['wuq']), full(w['wuk']), full(w['wuv']),
                  full(w['gqb']), full(w['gkb']),
                  pl.BlockSpec((2, tm, LANES), tab), pl.BlockSpec((2, tm, LANES), tab)],
        out_specs=[pl.BlockSpec((tm, wd), row) for wd in widths]
                  + [pl.BlockSpec((B_WIDTH, tm), lambda i: (0, i))],
        out_shape=[jax.ShapeDtypeStruct((n, wd), F32 if wd == A_WIDTH else BF16) for wd in widths]
                  + [jax.ShapeDtypeStruct((B_WIDTH, n), BF16)],
        compiler_params=_params(1),
        name="proj",
    )(x2d, w['gmix'], w['win'], w['gqa'], w['gka'], w['gcq'], w['gckv'], w['wuq'], w['wuk'], w['wuv'],
      w['gqb'], w['gkb'], w['taba'], w['tabb'])


DIL_TQ = 128
DIL_WIN = DIL_TQ + 2 * A_HALF_WINDOW
DIL_SLOTS = 4


def _strided_rows(start, size, stride):
    return pl.ds(start, size) if stride == 1 else pl.ds(start, size, stride=stride)


def _dilated_kernel(q_ref, k_ref, v_ref, o_ref, lse_scr, s_scr, bias_scr):
    seq = q_ref.shape[1]
    lane = lax.broadcasted_iota(jnp.int32, (1, LANES), 1)
    low_head = lane < A_HEAD_DIM
    q_first = _lane_mask(_A_PERM < A_HEAD_DIM)
    rel = (lax.broadcasted_iota(jnp.int32, (2 * DIL_TQ, DIL_WIN), 1)
           - lax.broadcasted_iota(jnp.int32, (2 * DIL_TQ, DIL_WIN), 0) % DIL_TQ)
    for place in range(3):
        bias_scr[place] = jnp.where(jnp.abs(rel - place * A_HALF_WINDOW) <= A_HALF_WINDOW, 0.0, NEG_BIG)

    for dil in DILATIONS:
        sub_len = seq // dil
        n_blk = sub_len // DIL_TQ
        n_total = dil * n_blk

        def coords(idx, dil=dil, sub_len=sub_len, n_blk=n_blk):
            r = idx // n_blk
            q0 = (idx % n_blk) * DIL_TQ
            k0 = jnp.clip(q0 - A_HALF_WINDOW, 0, sub_len - DIL_WIN)
            rows_q = _strided_rows(r + dil * q0, DIL_TQ, dil)
            rows_k = _strided_rows(r + dil * k0, DIL_WIN, dil)
            return rows_q, rows_k, (q0 - k0) // A_HALF_WINDOW

        def scores_into(slot, idx, coords=coords):
            rows_q, rows_k, place = coords(idx)
            qp = q_ref[0, rows_q, :]
            kw = k_ref[0, rows_k, :].astype(BF16)
            q2 = jnp.concatenate([jnp.where(q_first, qp, 0.0), jnp.where(q_first, 0.0, qp)], axis=0).astype(BF16)
            s_scr[slot] = _dot_nt(q2, kw) + bias_scr[place]

        def finish(slot, idx, dil=dil, coords=coords):
            rows_q, rows_k, _ = coords(idx)
            vw = v_ref[0, rows_k, :].astype(BF16)
            s = s_scr[slot]
            m = jnp.max(s, axis=-1, keepdims=True)
            pr = jnp.exp(s - m)
            den = jnp.sum(pr, axis=-1, keepdims=True)
            o2 = _dot(pr.astype(BF16), vw) / den
            lse2 = m + jnp.log(den)
            o = jnp.where(low_head, o2[:DIL_TQ], o2[DIL_TQ:])
            lse = jnp.where(low_head, lse2[:DIL_TQ], lse2[DIL_TQ:])
            if dil != DILATIONS[0]:
                o_prev, lse_prev = o_ref[0, rows_q, :], lse_scr[rows_q, :]
                top = jnp.maximum(lse_prev, lse)
                w_prev, w_new = jnp.exp(lse_prev - top), jnp.exp(lse - top)
                tot = w_prev + w_new
                o = (w_prev * o_prev + w_new * o) / tot
                lse = top + jnp.log(tot)
            o_ref[0, rows_q, :] = o
            if dil != DILATIONS[-1]:
                lse_scr[rows_q, :] = lse

        assert n_total % DIL_SLOTS == 0
        for slot in range(DIL_SLOTS - 1):
            scores_into(slot, slot)

        def step(i, carry, scores_into=scores_into, finish=finish, n_total=n_total):
            for slot in range(DIL_SLOTS):
                idx = DIL_SLOTS * i + slot
                ahead = idx + DIL_SLOTS - 1
                scores_into((slot + DIL_SLOTS - 1) % DIL_SLOTS, jnp.where(ahead >= n_total, ahead - n_total, ahead))
                finish(slot, idx)
            return carry

        lax.fori_loop(0, n_total // DIL_SLOTS, step, 0)


def _dilated(qa, ka, va):
    b, s, w = qa.shape
    spec = pl.BlockSpec((1, s, LANES), lambda bi, p: (bi, 0, p))
    out = pl.pallas_call(
        _dilated_kernel,
        grid=(b, w // LANES),
        in_specs=[spec, spec, spec],
        out_specs=spec,
        out_shape=jax.ShapeDtypeStruct((b, s, w), F32),
        scratch_shapes=[pltpu.VMEM((s, LANES), F32),
                        pltpu.VMEM((DIL_SLOTS, 2 * DIL_TQ, DIL_WIN), F32),
                        pltpu.VMEM((3, 2 * DIL_TQ, DIL_WIN), F32)],
        compiler_params=_params(2),
        name="dilated",
    )(qa, ka, va)
    return out.reshape(b * s, w)


MLA_DEN_ROWS = 16
MLA_SLOTS = 8
MLA_AHEAD = 2


def _mla_kernel(q_ref, k_ref, vt_ref, o_ref, s_scr, smax_scr, *, tq, tk):
    seq = k_ref.shape[1]
    heads = range(2)
    n_blocks = seq // tk
    n_tiles = seq // tq
    assert seq % (MLA_SLOTS * tk) == 0 and seq % tq == 0

    def scores_into(slot, tile, blk):
        q0 = pl.multiple_of(tile * tq, tq)
        k0 = pl.multiple_of(blk * tk, tk)
        for hh in heads:
            cols = slice(hh * LANES, (hh + 1) * LANES)
            st = _dot_nt(k_ref[0, pl.ds(k0, tk), cols], q_ref[0, pl.ds(q0, tq), cols])
            s_scr[slot, hh] = st
            smax_scr[slot, hh] = jnp.max(st.reshape(tk // 8, 8, tq), axis=0)

    for blk in range(MLA_AHEAD):
        scores_into(blk, 0, blk)

    def tile_body(tile, carry):
        def body(t, state):
            state = list(state)
            for slot in range(MLA_SLOTS):
                blk = MLA_SLOTS * t + slot
                wrap = blk + MLA_AHEAD >= n_blocks
                scores_into((slot + MLA_AHEAD) % MLA_SLOTS,
                            jnp.where(wrap, jnp.minimum(tile + 1, n_tiles - 1), tile),
                            jnp.where(wrap, blk + MLA_AHEAD - n_blocks, blk + MLA_AHEAD))
                k0 = pl.multiple_of(blk * tk, tk)
                for hh in heads:
                    m, acc = state[hh]
                    m_new = jnp.maximum(m, jnp.max(smax_scr[slot, hh], axis=0, keepdims=True))
                    alpha = jnp.exp2(m - m_new)
                    pt = jnp.exp2((s_scr[slot, hh] - m_new).astype(BF16))
                    vt1 = jnp.concatenate([vt_ref[hh * B_V_DIM:(hh + 1) * B_V_DIM, pl.ds(k0, tk)],
                                           jnp.ones((MLA_DEN_ROWS, tk), BF16)], axis=0)
                    acc = alpha * acc + _dot(vt1, pt)
                    state[hh] = (m_new, acc)
            return tuple(state)

        init = tuple((jnp.full((1, tq), NEG_BIG, F32), jnp.zeros((B_V_DIM + MLA_DEN_ROWS, tq), F32))
                     for _ in heads)
        final = lax.fori_loop(0, n_blocks // MLA_SLOTS, body, init)
        ot = jnp.concatenate([acc[:B_V_DIM] / acc[B_V_DIM:B_V_DIM + 1] for _, acc in final], axis=0)
        o_ref[0, pl.ds(pl.multiple_of(tile * tq, tq), tq), :] = ot.T
        return carry

    lax.fori_loop(0, n_tiles, tile_body, 0)


def _mla(qb, kb, vbt, tq, tk):
    b, s, _ = qb.shape
    qk_spec = pl.BlockSpec((1, s, 2 * LANES), lambda bi, p: (bi, 0, p))
    out = pl.pallas_call(
        functools.partial(_mla_kernel, tq=tq, tk=tk),
        grid=(b, B_HEADS // 2),
        in_specs=[qk_spec, qk_spec, pl.BlockSpec((LANES, s), lambda bi, p: (p, bi))],
        out_specs=pl.BlockSpec((1, s, LANES), lambda bi, p: (bi, 0, p)),
        out_shape=jax.ShapeDtypeStruct((b, s, B_WIDTH), F32),
        scratch_shapes=[pltpu.VMEM((MLA_SLOTS, 2, tk, tq), F32), pltpu.VMEM((MLA_SLOTS, 2, 8, tq), F32)],
        compiler_params=_params(2),
        name="mla",
    )(qb, kb, vbt)
    return out.reshape(b * s, B_WIDTH)


def _mem_kv_kernel(mem_ref, g_ref, wkv_ref, gk_ref, k_ref, v_ref):
    mem = mem_ref[0]
    mm = (mem * _row_rms_scale(mem, mem.shape[-1]) * g_ref[...]).astype(BF16)
    kv = _dot(mm, wkv_ref[...])
    for hd in range(M_HEADS):
        cols = slice(hd * M_HEAD_DIM, (hd + 1) * M_HEAD_DIM)
        kh = kv[:, cols]
        k_ref[0, :, cols] = (kh * _row_rms_scale(kh, M_HEAD_DIM) * gk_ref[...]).astype(k_ref.dtype)
    v_ref[0] = kv[:, M_WIDTH:].astype(v_ref.dtype)


def _mem_kv(mem, w):
    b, n_mem, d = mem.shape
    const = lambda bi: (0, 0)
    full = lambda a: pl.BlockSpec(a.shape, const)
    blk = pl.BlockSpec((1, n_mem, M_WIDTH), lambda bi: (bi, 0, 0))
    return pl.pallas_call(
        _mem_kv_kernel,
        grid=(b,),
        in_specs=[pl.BlockSpec((1, n_mem, d), lambda bi: (bi, 0, 0)),
                  full(w['gmem_kv']), full(w['wkv_m']), full(w['gk_m'])],
        out_specs=[blk, blk],
        out_shape=[jax.ShapeDtypeStruct((b, n_mem, M_WIDTH), BF16)] * 2,
        compiler_params=_params(1),
        name="mem_kv",
    )(mem, w['gmem_kv'], w['wkv_m'], w['gk_m'])


FFN_CHUNK = 1024


def _post_kernel(x_ref, oa_ref, ob_ref, ga_ref, gb_ref, wa_ref, wb_ref,
                 gmem_ref, wq_ref, gq_ref, k_ref, v_ref, wo_ref, gffn_ref, w1_ref, w2_ref, y_ref):
    oa, ob = oa_ref[0], ob_ref[0]
    na = (oa * _row_rms_scale(oa, A_WIDTH) * ga_ref[...]).astype(BF16)
    nb = (ob * _row_rms_scale(ob, B_WIDTH) * gb_ref[...]).astype(BF16)
    x = x_ref[0] + _dot(na, wa_ref[...]) + _dot(nb, wb_ref[...])

    h = (x * _row_rms_scale(x, x.shape[-1]) * gmem_ref[...]).astype(BF16)
    q = _dot(h, wq_ref[...])
    heads = []
    for hd in range(M_HEADS):
        cols = slice(hd * M_HEAD_DIM, (hd + 1) * M_HEAD_DIM)
        qh = q[:, cols]
        qh = (qh * _row_rms_scale(qh, M_HEAD_DIM) * gq_ref[...]).astype(BF16)
        s = _dot_nt(qh, k_ref[0, :, cols])
        pr = jnp.exp(s - jnp.max(s, axis=-1, keepdims=True))
        pr = pr / jnp.sum(pr, axis=-1, keepdims=True)
        heads.append(_dot(pr.astype(BF16), v_ref[0, :, cols]).astype(BF16))
    x = x + _dot(jnp.concatenate(heads, axis=-1), wo_ref[...])

    h = (x * _row_rms_scale(x, x.shape[-1]) * gffn_ref[...]).astype(BF16)
    for c in range(w1_ref.shape[1] // FFN_CHUNK):
        cols = slice(c * FFN_CHUNK, (c + 1) * FFN_CHUNK)
        a = jnp.maximum(_dot(h, w1_ref[:, cols]), 0.0)
        x = x + _dot((a * a).astype(BF16), w2_ref[cols, :])
    y_ref[0] = x


def _post(x3d, oa, ob, km, vm, w, tm):
    b, s, d = x3d.shape
    n_mem = km.shape[1]
    const = lambda bi, i: (0, 0)
    full = lambda a: pl.BlockSpec(a.shape, const, pipeline_mode=pl.Buffered(1))
    tile = lambda width: pl.BlockSpec((1, tm, width), lambda bi, i: (bi, i, 0))
    kvblk = pl.BlockSpec((1, n_mem, M_WIDTH), lambda bi, i: (bi, 0, 0))
    names = ['ga_out', 'gb_out', 'wout_a', 'wout_b', 'gmem', 'wq_m', 'gq_m']
    tail = ['wo_m', 'gffn', 'w1', 'w2']
    return pl.pallas_call(
        _post_kernel,
        grid=(b, s // tm),
        in_specs=[tile(d), tile(A_WIDTH), tile(B_WIDTH)] + [full(w[n]) for n in names] + [kvblk, kvblk]
                 + [full(w[n]) for n in tail],
        out_specs=tile(d),
        out_shape=jax.ShapeDtypeStruct((b, s, d), F32),
        compiler_params=_params(2),
        name="post",
    )(x3d, oa, ob, *[w[n] for n in names], km, vm, *[w[n] for n in tail])


def _rope_tables(seq, rope_dim):
    half = rope_dim // 2
    assert ROPE_SPAN % half == 0
    inv = ROPE_THETA ** (-jnp.arange(0, rope_dim, 2, dtype=F32) / rope_dim)
    ang = jnp.arange(seq, dtype=F32)[:, None] * inv[None, :]
    cos, sin = jnp.cos(ang), jnp.sin(ang)
    lane = np.arange(LANES)
    is_lo = lane < ROPE_SPAN
    is_hi = (lane >= LANES // 2) & (lane < LANES // 2 + ROPE_SPAN)
    idx = (lane % (LANES // 2)) % half
    cos_t = jnp.where(is_lo | is_hi, cos[:, idx], 1.0)
    sin_t = jnp.where(is_lo, -sin[:, idx], jnp.where(is_hi, sin[:, idx], 0.0))
    return jnp.stack([cos_t, sin_t]).astype(F32)


def _layer_weights(l, p):
    row = lambda v: v.reshape(1, -1).astype(F32)
    w_in = p['w_in'][l]
    d = w_in.shape[0]
    c_kr = 3 * A_WIDTH + Q_LORA + KV_LORA
    half_r = B_ROPE_DIM // 2
    zeros = lambda n: jnp.zeros((d, n), F32)
    kr = w_in[:, c_kr:]
    win = jnp.concatenate([
        w_in[:, :A_WIDTH][:, _A_PERM_ALL], w_in[:, A_WIDTH:2 * A_WIDTH][:, _A_PERM_ALL], w_in[:, 2 * A_WIDTH:c_kr],
        kr[:, :half_r], zeros(LANES // 2 - half_r), kr[:, half_r:], zeros(LANES // 2 - half_r)], axis=1)

    def b_lanes(a, with_rope):
        if with_rope:
            a = a[..., _B_PERM]
        else:
            z = jnp.zeros(a.shape[:-1] + (half_r,), a.dtype)
            a = jnp.concatenate([z, a[..., :_B_SPLIT], z, a[..., _B_SPLIT:]], axis=-1)
        return jnp.pad(a, [(0, 0)] * (a.ndim - 1) + [(0, LANES - B_QK_DIM)])

    wukv = p['w_ukv'][l].reshape(KV_LORA, B_HEADS, B_NOPE_DIM + B_V_DIM)
    b_gain = lambda g: jnp.tile(b_lanes(g, True), B_HEADS)
    a_gain = lambda g: jnp.tile(g, A_HEADS)[_A_PERM_ALL]
    w_out = p['w_out'][l]
    return {
        'gmix': row(p['norm_mix_g'][l]),
        'win': win.astype(BF16),
        'gqa': row(a_gain(p['a_q_norm_g'][l]) * (A_HEAD_DIM ** -0.5)),
        'gka': row(a_gain(p['a_k_norm_g'][l])),
        'gcq': row(p['b_cq_norm_g'][l]),
        'gckv': row(p['b_ckv_norm_g'][l]),
        'wuq': b_lanes(p['w_uq'][l].reshape(Q_LORA, B_HEADS, B_QK_DIM), True).reshape(Q_LORA, B_QK_PAD).astype(BF16),
        'wuk': b_lanes(wukv[:, :, :B_NOPE_DIM], False).reshape(KV_LORA, B_QK_PAD).astype(BF16),
        'wuv': wukv[:, :, B_NOPE_DIM:].reshape(KV_LORA, B_WIDTH).astype(BF16),
        'gqb': row(b_gain(p['b_q_norm_g'][l]) * (B_QK_DIM ** -0.5 * LOG2_E)),
        'gkb': row(b_gain(p['b_k_norm_g'][l])),
        'ga_out': row(p['a_out_norm_g'][l]),
        'gb_out': row(p['b_out_norm_g'][l]),
        'wout_a': w_out[:A_WIDTH].astype(BF16),
        'wout_b': w_out[A_WIDTH:].astype(BF16),
        'gmem': row(p['norm_mem_g'][l]),
        'gmem_kv': row(p['mem_kv_norm_g'][l]),
        'wq_m': p['m_wq'][l].astype(BF16),
        'wkv_m': p['m_wkv'][l].astype(BF16),
        'gq_m': row(p['m_q_norm_g'][l] * (M_HEAD_DIM ** -0.5)),
        'gk_m': row(p['m_k_norm_g'][l]),
        'wo_m': p['m_wo'][l].astype(BF16),
        'gffn': row(p['norm_ffn_g'][l]),
        'w1': p['w_ff1'][l].astype(BF16),
        'w2': p['w_ff2'][l].astype(BF16),
    }


def _tiles(seq):
    tm = 512 if seq % 512 == 0 else 256
    return dict(tm=tm, tq=256, tk=512)


def _trunk(x, mem, layers):
    b, s, d = x.shape
    assert s % (max(DILATIONS) * DIL_TQ) == 0 and s // max(DILATIONS) >= DIL_WIN
    t = _tiles(s)
    taba = _rope_tables(s, A_ROPE_DIM)
    tabb = _rope_tables(s, B_ROPE_DIM)
    x2d = x.reshape(b * s, d)
    for w in layers:
        w = dict(w, taba=taba, tabb=tabb)
        qa, ka, va, qb, kb, vbt = _proj(x2d, s, w, t['tm'])
        as3d = lambda a: a.reshape(b, s, a.shape[-1])
        oa = _dilated(as3d(qa), as3d(ka), as3d(va))
        ob = _mla(as3d(qb), as3d(kb), vbt, t['tq'], t['tk'])
        km, vm = _mem_kv(mem, w)
        x2d = _post(as3d(x2d), as3d(oa), as3d(ob), km, vm, w, t['tm']).reshape(b * s, d)
    return x2d.reshape(b, s, d)


def kernel(x_prompt, x_sample, mem_prompt, mem_sample, norm_mix_g, w_in, a_q_norm_g, a_k_norm_g, b_cq_norm_g, b_ckv_norm_g, w_uq, w_ukv, b_q_norm_g, b_k_norm_g, a_out_norm_g, b_out_norm_g, w_out, norm_mem_g, mem_kv_norm_g, m_wq, m_wkv, m_q_norm_g, m_k_norm_g, m_wo, norm_ffn_g, w_ff1, w_ff2):
    p = dict(norm_mix_g=norm_mix_g, w_in=w_in, a_q_norm_g=a_q_norm_g, a_k_norm_g=a_k_norm_g,
             b_cq_norm_g=b_cq_norm_g, b_ckv_norm_g=b_ckv_norm_g, w_uq=w_uq, w_ukv=w_ukv,
             b_q_norm_g=b_q_norm_g, b_k_norm_g=b_k_norm_g, a_out_norm_g=a_out_norm_g,
             b_out_norm_g=b_out_norm_g, w_out=w_out, norm_mem_g=norm_mem_g, mem_kv_norm_g=mem_kv_norm_g,
             m_wq=m_wq, m_wkv=m_wkv, m_q_norm_g=m_q_norm_g, m_k_norm_g=m_k_norm_g, m_wo=m_wo,
             norm_ffn_g=norm_ffn_g, w_ff1=w_ff1, w_ff2=w_ff2)
    layers = [_layer_weights(l, p) for l in range(w_in.shape[0])]
    return (_trunk(x_prompt, mem_prompt, layers), _trunk(x_sample, mem_sample, layers))
```

```python
import functools

import numpy as np
import jax
import jax.numpy as jnp
from jax import lax
from jax.experimental import pallas as pl
from jax.experimental.pallas import tpu as pltpu

EPS = 1e-6
ROPE_THETA = 500000.0

LANES = 128
A_HEADS = 8
A_HEAD_DIM = 64
A_WIDTH = A_HEADS * A_HEAD_DIM
A_ROPE_DIM = 16
A_HALF_WINDOW = 64
DILATIONS = (16, 4, 1)

B_HEADS = 8
B_NOPE_DIM = 64
B_ROPE_DIM = 32
B_QK_DIM = B_NOPE_DIM + B_ROPE_DIM
B_V_DIM = 64
B_WIDTH = B_HEADS * B_V_DIM
B_QK_PAD = B_HEADS * LANES
Q_LORA = 256
KV_LORA = 128

M_HEADS = 4
M_HEAD_DIM = 128
M_WIDTH = M_HEADS * M_HEAD_DIM

VMEM_LIMIT = 56 * 1024 * 1024

BF16 = jnp.bfloat16
F32 = jnp.float32
NEG_BIG = -1e30
LOG2_E = float(np.log2(np.e))


def _a_pair_perm():
    heads = [np.arange(A_HEAD_DIM), A_HEAD_DIM + np.arange(A_HEAD_DIM)]
    r = A_ROPE_DIM // 2
    rest = (A_HEAD_DIM - A_ROPE_DIM) // 2
    first = [h[:r] for h in heads] + [h[A_ROPE_DIM:A_ROPE_DIM + rest] for h in heads]
    second = [h[r:A_ROPE_DIM] for h in heads] + [h[A_ROPE_DIM + rest:] for h in heads]
    return np.concatenate(first + second)


ROPE_SPAN = 16
_A_PERM = _a_pair_perm()
_A_PERM_ALL = np.concatenate([c * LANES + _A_PERM for c in range(A_WIDTH // LANES)])
_B_SPLIT = LANES // 2 - B_ROPE_DIM // 2
_B_PERM = np.concatenate([np.arange(B_NOPE_DIM, B_NOPE_DIM + B_ROPE_DIM // 2), np.arange(0, _B_SPLIT),
                          np.arange(B_NOPE_DIM + B_ROPE_DIM // 2, B_QK_DIM), np.arange(_B_SPLIT, B_NOPE_DIM)])


def _lane_mask(member):
    lane = lax.broadcasted_iota(jnp.int32, (1, LANES), 1)
    edges = np.flatnonzero(np.diff(np.concatenate([[0], member.astype(np.int8), [0]])))
    mask = None
    for lo, hi in zip(edges[::2], edges[1::2]):
        run = (lane >= int(lo)) & (lane < int(hi))
        mask = run if mask is None else (mask | run)
    return mask


def _params(n_axes):
    return pltpu.CompilerParams(dimension_semantics=("arbitrary",) * n_axes,
                                vmem_limit_bytes=VMEM_LIMIT)


def _row_rms_scale(x, width):
    return lax.rsqrt(jnp.sum(x * x, axis=-1, keepdims=True) * (1.0 / width) + EPS)


def _dot(a, b):
    return jnp.dot(a, b, preferred_element_type=F32)


def _dot_nt(a, b):
    return lax.dot_general(a, b, (((1,), (1,)), ((), ())), preferred_element_type=F32)


def _rope_chunk(xc, cos, sin):
    return xc * cos + pltpu.roll(xc, LANES // 2, axis=1) * sin


def _proj_kernel(x_ref, gmix_ref, win_ref, gqa_ref, gka_ref, gcq_ref, gckv_ref,
                 wuq_ref, wuk_ref, wuv_ref, gqb_ref, gkb_ref, aones_ref, ones_ref, taba_ref, tabb_ref,
                 qa_ref, ka_ref, va_ref, qb_ref, kb_ref, vbt_ref):
    x = x_ref[...]
    h = (x * _row_rms_scale(x, x.shape[-1]) * gmix_ref[...]).astype(BF16)
    proj = _dot(h, win_ref[...])

    cos_a, sin_a = taba_ref[0], taba_ref[1]
    cos_b, sin_b = tabb_ref[0], tabb_ref[1]

    def a_heads(col0, g_ref, out_ref):
        for c2 in range(A_WIDTH // (2 * LANES)):
            xs = proj[:, col0 + 2 * c2 * LANES: col0 + 2 * (c2 + 1) * LANES]
            ssq = _dot((xs * xs).astype(BF16), aones_ref[...])
            for i in range(2):
                c = 2 * c2 + i
                lanes = slice(i * LANES, (i + 1) * LANES)
                rs = lax.rsqrt(ssq[:, lanes] * (1.0 / A_HEAD_DIM) + EPS)
                y = xs[:, lanes] * rs * g_ref[:, c * LANES:(c + 1) * LANES]
                y = _rope_chunk(y, cos_a, sin_a)
                out_ref[:, c * LANES:(c + 1) * LANES] = y.astype(out_ref.dtype)

    a_heads(0, gqa_ref, qa_ref)
    a_heads(A_WIDTH, gka_ref, ka_ref)
    va_ref[...] = proj[:, 2 * A_WIDTH:3 * A_WIDTH].astype(va_ref.dtype)

    c0 = 3 * A_WIDTH
    cq = proj[:, c0:c0 + Q_LORA]
    ckv = proj[:, c0 + Q_LORA:c0 + Q_LORA + KV_LORA]
    kr = proj[:, c0 + Q_LORA + KV_LORA:c0 + Q_LORA + 2 * KV_LORA]
    cqn = (cq * _row_rms_scale(cq, Q_LORA) * gcq_ref[...]).astype(BF16)
    ckvn = (ckv * _row_rms_scale(ckv, KV_LORA) * gckv_ref[...]).astype(BF16)
    qb = _dot(cqn, wuq_ref[...])
    kn = _dot(ckvn, wuk_ref[...])
    vbt_ref[...] = _dot(ckvn, wuv_ref[...]).T.astype(vbt_ref.dtype)

    def b_heads(get_chunk, g_ref, out_ref):
        for c2 in range(B_HEADS // 2):
            xs = [get_chunk(2 * c2 + i) for i in range(2)]
            ssq = _dot(jnp.concatenate([xc * xc for xc in xs], axis=1).astype(BF16), ones_ref[...])
            for i, xc in enumerate(xs):
                c = 2 * c2 + i
                rs = lax.rsqrt(ssq[:, i * LANES:(i + 1) * LANES] * (1.0 / B_QK_DIM) + EPS)
                y = xc * rs * g_ref[:, c * LANES:(c + 1) * LANES]
                y = _rope_chunk(y, cos_b, sin_b)
                out_ref[:, c * LANES:(c + 1) * LANES] = y.astype(out_ref.dtype)

    b_heads(lambda c: qb[:, c * LANES:(c + 1) * LANES], gqb_ref, qb_ref)
    b_heads(lambda c: kn[:, c * LANES:(c + 1) * LANES] + kr, gkb_ref, kb_ref)


def _proj(x2d, seq, w, tm):
    n, d = x2d.shape
    nblk_seq = seq // tm
    row = lambda i: (i, 0)
    const = lambda i: (0, 0)
    tab = lambda i: (0, i % nblk_seq, 0)
    full = lambda a: pl.BlockSpec(a.shape, const)
    widths = [A_WIDTH, A_WIDTH, A_WIDTH, B_QK_PAD, B_QK_PAD]
    return pl.pallas_call(
        _proj_kernel,
        grid=(n // tm,),
        in_specs=[pl.BlockSpec((tm, d), row), full(w['gmix']), full(w['win']), full(w['gqa']), full(w['gka']),
                  full(w['gcq']), full(w['gckv']), full(w['wuq']), full(w['wuk']), full(w['wuv']),
                  full(w['gqb']), full(w['gkb']), full(w['pair_ones']), full(w['head_ones']),
                  pl.BlockSpec((2, tm, LANES), tab), pl.BlockSpec((2, tm, LANES), tab)],
        out_specs=[pl.BlockSpec((tm, wd), row) for wd in widths]
                  + [pl.BlockSpec((B_WIDTH, tm), lambda i: (0, i))],
        out_shape=[jax.ShapeDtypeStruct((n, wd), F32 if wd == A_WIDTH else BF16) for wd in widths]
                  + [jax.ShapeDtypeStruct((B_WIDTH, n), BF16)],
        compiler_params=_params(1),
        name="proj",
    )(x2d, w['gmix'], w['win'], w['gqa'], w['gka'], w['gcq'], w['gckv'], w['wuq'], w['wuk'], w['wuv'],
      w['gqb'], w['gkb'], w['pair_ones'], w['head_ones'], w['taba'], w['tabb'])


DIL_TQ = 128
DIL_WIN = DIL_TQ + 2 * A_HALF_WINDOW
DIL_SLOTS = 4


def _strided_rows(start, size, stride):
    return pl.ds(start, size) if stride == 1 else pl.ds(start, size, stride=stride)


def _dilated_kernel(q_ref, k_ref, v_ref, o_ref, lse_scr, s_scr, bias_scr):
    seq = q_ref.shape[1]
    lane = lax.broadcasted_iota(jnp.int32, (1, LANES), 1)
    low_head = lane < A_HEAD_DIM
    q_first = _lane_mask(_A_PERM < A_HEAD_DIM)
    rel = (lax.broadcasted_iota(jnp.int32, (2 * DIL_TQ, DIL_WIN), 1)
           - lax.broadcasted_iota(jnp.int32, (2 * DIL_TQ, DIL_WIN), 0) % DIL_TQ)
    for place in range(3):
        bias_scr[place] = jnp.where(jnp.abs(rel - place * A_HALF_WINDOW) <= A_HALF_WINDOW, 0.0, NEG_BIG)

    for dil in DILATIONS:
        sub_len = seq // dil
        n_blk = sub_len // DIL_TQ
        n_total = dil * n_blk

        def coords(idx, dil=dil, sub_len=sub_len, n_blk=n_blk):
            r = idx // n_blk
            q0 = (idx % n_blk) * DIL_TQ
            k0 = jnp.clip(q0 - A_HALF_WINDOW, 0, sub_len - DIL_WIN)
            rows_q = _strided_rows(r + dil * q0, DIL_TQ, dil)
            rows_k = _strided_rows(r + dil * k0, DIL_WIN, dil)
            return rows_q, rows_k, (q0 - k0) // A_HALF_WINDOW

        def scores_into(slot, idx, coords=coords):
            rows_q, rows_k, place = coords(idx)
            qp = q_ref[0, rows_q, :]
            kw = k_ref[0, rows_k, :].astype(BF16)
            q2 = jnp.concatenate([jnp.where(q_first, qp, 0.0), jnp.where(q_first, 0.0, qp)], axis=0).astype(BF16)
            s_scr[slot] = _dot_nt(q2, kw) + bias_scr[place]

        def finish(slot, idx, dil=dil, coords=coords):
            rows_q, rows_k, _ = coords(idx)
            vw = v_ref[0, rows_k, :].astype(BF16)
            s = s_scr[slot]
            m2 = jnp.max(s, axis=-1, keepdims=True)
            pr = jnp.exp2(s - m2)
            den2 = jnp.sum(pr, axis=-1, keepdims=True)
            o2 = _dot(pr.astype(BF16), vw)
            pick = lambda a: jnp.where(low_head, a[:DIL_TQ], a[DIL_TQ:])
            den = pick(den2)
            o = pick(o2) / den
            lse = pick(m2) + jnp.log2(den)
            if dil != DILATIONS[0]:
                o_prev, lse_prev = o_ref[0, rows_q, :], lse_scr[rows_q, :]
                top = jnp.maximum(lse_prev, lse)
                w_prev, w_new = jnp.exp2(lse_prev - top), jnp.exp2(lse - top)
                tot = w_prev + w_new
                o = (w_prev * o_prev + w_new * o) / tot
                lse = top + jnp.log2(tot)
            o_ref[0, rows_q, :] = o
            if dil != DILATIONS[-1]:
                lse_scr[rows_q, :] = lse

        assert n_total % DIL_SLOTS == 0
        for slot in range(DIL_SLOTS - 1):
            scores_into(slot, slot)

        def step(i, carry, scores_into=scores_into, finish=finish, n_total=n_total):
            for slot in range(DIL_SLOTS):
                idx = DIL_SLOTS * i + slot
                ahead = idx + DIL_SLOTS - 1
                scores_into((slot + DIL_SLOTS - 1) % DIL_SLOTS, jnp.where(ahead >= n_total, ahead - n_total, ahead))
                finish(slot, idx)
            return carry

        lax.fori_loop(0, n_total // DIL_SLOTS, step, 0)


def _dilated(qa, ka, va):
    b, s, w = qa.shape
    spec = pl.BlockSpec((1, s, LANES), lambda bi, p: (bi, 0, p))
    out = pl.pallas_call(
        _dilated_kernel,
        grid=(b, w // LANES),
        in_specs=[spec, spec, spec],
        out_specs=spec,
        out_shape=jax.ShapeDtypeStruct((b, s, w), F32),
        scratch_shapes=[pltpu.VMEM((s, LANES), F32),
                        pltpu.VMEM((DIL_SLOTS, 2 * DIL_TQ, DIL_WIN), F32),
                        pltpu.VMEM((3, 2 * DIL_TQ, DIL_WIN), F32)],
        compiler_params=_params(2),
        name="dilated",
    )(qa, ka, va)
    return out.reshape(b * s, w)


MLA_DEN_ROWS = 16
MLA_SLOTS = 8
MLA_AHEAD = 2


def _mla_kernel(q_ref, k_ref, vt_ref, o_ref, s_scr, smax_scr, *, tq, tk):
    seq = k_ref.shape[1]
    heads = range(2)
    n_blocks = seq // tk
    n_tiles = seq // tq
    assert seq % (MLA_SLOTS * tk) == 0 and seq % tq == 0

    def scores_into(slot, tile, blk, hh):
        q0 = pl.multiple_of(tile * tq, tq)
        k0 = pl.multiple_of(blk * tk, tk)
        cols = slice(hh * LANES, (hh + 1) * LANES)
        st = _dot_nt(k_ref[0, pl.ds(k0, tk), cols], q_ref[0, pl.ds(q0, tq), cols])
        s_scr[slot, hh] = st
        smax_scr[slot, hh] = jnp.max(st.reshape(tk // 8, 8, tq), axis=0)

    for blk in range(MLA_AHEAD):
        for hh in heads:
            scores_into(blk, 0, blk, hh)

    def tile_body(tile, carry):
        def body(t, state):
            state = list(state)
            for slot in range(MLA_SLOTS):
                blk = MLA_SLOTS * t + slot
                wrap = blk + MLA_AHEAD >= n_blocks
                ahead = ((slot + MLA_AHEAD) % MLA_SLOTS,
                         jnp.where(wrap, jnp.minimum(tile + 1, n_tiles - 1), tile),
                         jnp.where(wrap, blk + MLA_AHEAD - n_blocks, blk + MLA_AHEAD))
                k0 = pl.multiple_of(blk * tk, tk)
                for hh in heads:
                    scores_into(*ahead, hh)
                    m, acc = state[hh]
                    m_new = jnp.maximum(m, jnp.max(smax_scr[slot, hh], axis=0, keepdims=True))
                    alpha = jnp.exp2(m - m_new)
                    pt = jnp.exp2(s_scr[slot, hh] - m_new).astype(BF16)
                    vt1 = jnp.concatenate([vt_ref[hh * B_V_DIM:(hh + 1) * B_V_DIM, pl.ds(k0, tk)],
                                           jnp.ones((MLA_DEN_ROWS, tk), BF16)], axis=0)
                    acc = alpha * acc + _dot(vt1, pt)
                    state[hh] = (m_new, acc)
            return tuple(state)

        init = tuple((jnp.full((1, tq), NEG_BIG, F32), jnp.zeros((B_V_DIM + MLA_DEN_ROWS, tq), F32))
                     for _ in heads)
        final = lax.fori_loop(0, n_blocks // MLA_SLOTS, body, init)
        ot = jnp.concatenate([acc[:B_V_DIM] / acc[B_V_DIM:B_V_DIM + 1] for _, acc in final], axis=0)
        o_ref[0, pl.ds(pl.multiple_of(tile * tq, tq), tq), :] = ot.T
        return carry

    lax.fori_loop(0, n_tiles, tile_body, 0)


def _mla(qb, kb, vbt, tq, tk):
    b, s, _ = qb.shape
    qk_spec = pl.BlockSpec((1, s, 2 * LANES), lambda bi, p: (bi, 0, p))
    out = pl.pallas_call(
        functools.partial(_mla_kernel, tq=tq, tk=tk),
        grid=(b, B_HEADS // 2),
        in_specs=[qk_spec, qk_spec, pl.BlockSpec((LANES, s), lambda bi, p: (p, bi))],
        out_specs=pl.BlockSpec((1, s, LANES), lambda bi, p: (bi, 0, p)),
        out_shape=jax.ShapeDtypeStruct((b, s, B_WIDTH), F32),
        scratch_shapes=[pltpu.VMEM((MLA_SLOTS, 2, tk, tq), F32), pltpu.VMEM((MLA_SLOTS, 2, 8, tq), F32)],
        compiler_params=_params(2),
        name="mla",
    )(qb, kb, vbt)
    return out.reshape(b * s, B_WIDTH)


def _mem_kv_kernel(mem_ref, g_ref, wkv_ref, gk_ref, k_ref, v_ref):
    mem = mem_ref[0]
    mm = (mem * _row_rms_scale(mem, mem.shape[-1]) * g_ref[...]).astype(BF16)
    kv = _dot(mm, wkv_ref[...])
    for hd in range(M_HEADS):
        cols = slice(hd * M_HEAD_DIM, (hd + 1) * M_HEAD_DIM)
        kh = kv[:, cols]
        k_ref[0, :, cols] = (kh * _row_rms_scale(kh, M_HEAD_DIM) * gk_ref[...]).astype(k_ref.dtype)
    v_ref[0] = kv[:, M_WIDTH:].astype(v_ref.dtype)


def _mem_kv(mem, w):
    b, n_mem, d = mem.shape
    const = lambda bi: (0, 0)
    full = lambda a: pl.BlockSpec(a.shape, const)
    blk = pl.BlockSpec((1, n_mem, M_WIDTH), lambda bi: (bi, 0, 0))
    return pl.pallas_call(
        _mem_kv_kernel,
        grid=(b,),
        in_specs=[pl.BlockSpec((1, n_mem, d), lambda bi: (bi, 0, 0)),
                  full(w['gmem_kv']), full(w['wkv_m']), full(w['gk_m'])],
        out_specs=[blk, blk],
        out_shape=[jax.ShapeDtypeStruct((b, n_mem, M_WIDTH), BF16)] * 2,
        compiler_params=_params(1),
        name="mem_kv",
    )(mem, w['gmem_kv'], w['wkv_m'], w['gk_m'])


FFN_CHUNK = 1024


def _post_kernel(x_ref, oa_ref, ob_ref, ga_ref, gb_ref, wa_ref, wb_ref,
                 gmem_ref, wq_ref, gq_ref, k_ref, v_ref, wo_ref, gffn_ref, w1_ref, w2_ref, y_ref):
    oa, ob = oa_ref[0], ob_ref[0]
    na = (oa * _row_rms_scale(oa, A_WIDTH) * ga_ref[...]).astype(BF16)
    nb = (ob * _row_rms_scale(ob, B_WIDTH) * gb_ref[...]).astype(BF16)
    x = x_ref[0] + _dot(na, wa_ref[...]) + _dot(nb, wb_ref[...])

    h = (x * _row_rms_scale(x, x.shape[-1]) * gmem_ref[...]).astype(BF16)
    q = _dot(h, wq_ref[...])
    heads = []
    for hd in range(M_HEADS):
        cols = slice(hd * M_HEAD_DIM, (hd + 1) * M_HEAD_DIM)
        qh = q[:, cols]
        qh = (qh * _row_rms_scale(qh, M_HEAD_DIM) * gq_ref[...]).astype(BF16)
        s = _dot_nt(qh, k_ref[0, :, cols])
        pr = jnp.exp(s - jnp.max(s, axis=-1, keepdims=True))
        pr = pr / jnp.sum(pr, axis=-1, keepdims=True)
        heads.append(_dot(pr.astype(BF16), v_ref[0, :, cols]).astype(BF16))
    x = x + _dot(jnp.concatenate(heads, axis=-1), wo_ref[...])

    h = (x * _row_rms_scale(x, x.shape[-1]) * gffn_ref[...]).astype(BF16)
    for c in range(w1_ref.shape[1] // FFN_CHUNK):
        cols = slice(c * FFN_CHUNK, (c + 1) * FFN_CHUNK)
        a = jnp.maximum(_dot(h, w1_ref[:, cols]), 0.0)
        x = x + _dot((a * a).astype(BF16), w2_ref[cols, :])
    y_ref[0] = x


def _post(x3d, oa, ob, km, vm, w, tm):
    b, s, d = x3d.shape
    n_mem = km.shape[1]
    const = lambda bi, i: (0, 0)
    full = lambda a: pl.BlockSpec(a.shape, const, pipeline_mode=pl.Buffered(1))
    tile = lambda width: pl.BlockSpec((1, tm, width), lambda bi, i: (bi, i, 0))
    kvblk = pl.BlockSpec((1, n_mem, M_WIDTH), lambda bi, i: (bi, 0, 0))
    names = ['ga_out', 'gb_out', 'wout_a', 'wout_b', 'gmem', 'wq_m', 'gq_m']
    tail = ['wo_m', 'gffn', 'w1', 'w2']
    return pl.pallas_call(
        _post_kernel,
        grid=(b, s // tm),
        in_specs=[tile(d), tile(A_WIDTH), tile(B_WIDTH)] + [full(w[n]) for n in names] + [kvblk, kvblk]
                 + [full(w[n]) for n in tail],
        out_specs=tile(d),
        out_shape=jax.ShapeDtypeStruct((b, s, d), F32),
        compiler_params=_params(2),
        name="post",
    )(x3d, oa, ob, *[w[n] for n in names], km, vm, *[w[n] for n in tail])


def _rope_tables(seq, rope_dim):
    half = rope_dim // 2
    assert ROPE_SPAN % half == 0
    inv = ROPE_THETA ** (-jnp.arange(0, rope_dim, 2, dtype=F32) / rope_dim)
    ang = jnp.arange(seq, dtype=F32)[:, None] * inv[None, :]
    cos, sin = jnp.cos(ang), jnp.sin(ang)
    lane = np.arange(LANES)
    is_lo = lane < ROPE_SPAN
    is_hi = (lane >= LANES // 2) & (lane < LANES // 2 + ROPE_SPAN)
    idx = (lane % (LANES // 2)) % half
    cos_t = jnp.where(is_lo | is_hi, cos[:, idx], 1.0)
    sin_t = jnp.where(is_lo, -sin[:, idx], jnp.where(is_hi, sin[:, idx], 0.0))
    return jnp.stack([cos_t, sin_t]).astype(F32)


def _layer_weights(l, p):
    row = lambda v: v.reshape(1, -1).astype(F32)
    w_in = p['w_in'][l]
    d = w_in.shape[0]
    c_kr = 3 * A_WIDTH + Q_LORA + KV_LORA
    half_r = B_ROPE_DIM // 2
    zeros = lambda n: jnp.zeros((d, n), F32)
    kr = w_in[:, c_kr:]
    win = jnp.concatenate([
        w_in[:, :A_WIDTH][:, _A_PERM_ALL], w_in[:, A_WIDTH:2 * A_WIDTH][:, _A_PERM_ALL], w_in[:, 2 * A_WIDTH:c_kr],
        kr[:, :half_r], zeros(LANES // 2 - half_r), kr[:, half_r:], zeros(LANES // 2 - half_r)], axis=1)

    def b_lanes(a, with_rope):
        if with_rope:
            a = a[..., _B_PERM]
        else:
            z = jnp.zeros(a.shape[:-1] + (half_r,), a.dtype)
            a = jnp.concatenate([z, a[..., :_B_SPLIT], z, a[..., _B_SPLIT:]], axis=-1)
        return jnp.pad(a, [(0, 0)] * (a.ndim - 1) + [(0, LANES - B_QK_DIM)])

    wukv = p['w_ukv'][l].reshape(KV_LORA, B_HEADS, B_NOPE_DIM + B_V_DIM)
    b_gain = lambda g: jnp.tile(b_lanes(g, True), B_HEADS)
    a_gain = lambda g: jnp.tile(g, A_HEADS)[_A_PERM_ALL]
    w_out = p['w_out'][l]
    return {
        'gmix': row(p['norm_mix_g'][l]),
        'win': win.astype(BF16),
        'gqa': row(a_gain(p['a_q_norm_g'][l]) * (A_HEAD_DIM ** -0.5 * LOG2_E)),
        'gka': row(a_gain(p['a_k_norm_g'][l])),
        'gcq': row(p['b_cq_norm_g'][l]),
        'gckv': row(p['b_ckv_norm_g'][l]),
        'wuq': b_lanes(p['w_uq'][l].reshape(Q_LORA, B_HEADS, B_QK_DIM), True).reshape(Q_LORA, B_QK_PAD).astype(BF16),
        'wuk': b_lanes(wukv[:, :, :B_NOPE_DIM], False).reshape(KV_LORA, B_QK_PAD).astype(BF16),
        'wuv': wukv[:, :, B_NOPE_DIM:].reshape(KV_LORA, B_WIDTH).astype(BF16),
        'gqb': row(b_gain(p['b_q_norm_g'][l]) * (B_QK_DIM ** -0.5 * LOG2_E)),
        'gkb': row(b_gain(p['b_k_norm_g'][l])),
        'ga_out': row(p['a_out_norm_g'][l]),
        'gb_out': row(p['b_out_norm_g'][l]),
        'wout_a': w_out[:A_WIDTH].astype(BF16),
        'wout_b': w_out[A_WIDTH:].astype(BF16),
        'gmem': row(p['norm_mem_g'][l]),
        'gmem_kv': row(p['mem_kv_norm_g'][l]),
        'wq_m': p['m_wq'][l].astype(BF16),
        'wkv_m': p['m_wkv'][l].astype(BF16),
        'gq_m': row(p['m_q_norm_g'][l] * (M_HEAD_DIM ** -0.5)),
        'gk_m': row(p['m_k_norm_g'][l]),
        'wo_m': p['m_wo'][l].astype(BF16),
        'gffn': row(p['norm_ffn_g'][l]),
        'w1': p['w_ff1'][l].astype(BF16),
        'w2': p['w_ff2'][l].astype(BF16),
    }


def _tiles(seq):
    tm = 512 if seq % 512 == 0 else 256
    return dict(tm=tm, tq=256, tk=512)


def _trunk(x, mem, layers):
    b, s, d = x.shape
    assert s % (max(DILATIONS) * DIL_TQ) == 0 and s // max(DILATIONS) >= DIL_WIN
    t = _tiles(s)
    taba = _rope_tables(s, A_ROPE_DIM)
    tabb = _rope_tables(s, B_ROPE_DIM)
    head_ones = jnp.asarray(np.kron(np.eye(2), np.ones((LANES, LANES))), BF16)
    same_head = (_A_PERM[:, None] < A_HEAD_DIM) == (_A_PERM[None, :] < A_HEAD_DIM)
    pair_ones = jnp.asarray(np.kron(np.eye(2), same_head), BF16)
    x2d = x.reshape(b * s, d)
    for w in layers:
        w = dict(w, taba=taba, tabb=tabb, head_ones=head_ones, pair_ones=pair_ones)
        qa, ka, va, qb, kb, vbt = _proj(x2d, s, w, t['tm'])
        as3d = lambda a: a.reshape(b, s, a.shape[-1])
        oa = _dilated(as3d(qa), as3d(ka), as3d(va))
        ob = _mla(as3d(qb), as3d(kb), vbt, t['tq'], t['tk'])
        km, vm = _mem_kv(mem, w)
        x2d = _post(as3d(x2d), as3d(oa), as3d(ob), km, vm, w, t['tm']).reshape(b * s, d)
    return x2d.reshape(b, s, d)


def kernel(x_prompt, x_sample, mem_prompt, mem_sample, norm_mix_g, w_in, a_q_norm_g, a_k_norm_g, b_cq_norm_g, b_ckv_norm_g, w_uq, w_ukv, b_q_norm_g, b_k_norm_g, a_out_norm_g, b_out_norm_g, w_out, norm_mem_g, mem_kv_norm_g, m_wq, m_wkv, m_q_norm_g, m_k_norm_g, m_wo, norm_ffn_g, w_ff1, w_ff2):
    p = dict(norm_mix_g=norm_mix_g, w_in=w_in, a_q_norm_g=a_q_norm_g, a_k_norm_g=a_k_norm_g,
             b_cq_norm_g=b_cq_norm_g, b_ckv_norm_g=b_ckv_norm_g, w_uq=w_uq, w_ukv=w_ukv,
             b_q_norm_g=b_q_norm_g, b_k_norm_g=b_k_norm_g, a_out_norm_g=a_out_norm_g,
             b_out_norm_g=b_out_norm_g, w_out=w_out, norm_mem_g=norm_mem_g, mem_kv_norm_g=mem_kv_norm_g,
             m_wq=m_wq, m_wkv=m_wkv, m_q_norm_g=m_q_norm_g, m_k_norm_g=m_k_norm_g, m_wo=m_wo,
             norm_ffn_g=norm_ffn_g, w_ff1=w_ff1, w_ff2=w_ff2)
    layers = [_layer_weights(l, p) for l in range(w_in.shape[0])]
    return (_trunk(x_prompt, mem_prompt, layers), _trunk(x_sample, mem_sample, layers))
```

```python
import functools

import numpy as np
import jax
import jax.numpy as jnp
from jax import lax
from jax.experimental import pallas as pl
from jax.experimental.pallas import tpu as pltpu

EPS = 1e-6
ROPE_THETA = 500000.0

LANES = 128
A_HEADS = 8
A_HEAD_DIM = 64
A_WIDTH = A_HEADS * A_HEAD_DIM
A_ROPE_DIM = 16
A_HALF_WINDOW = 64
DILATIONS = (16, 4, 1)

B_HEADS = 8
B_NOPE_DIM = 64
B_ROPE_DIM = 32
B_QK_DIM = B_NOPE_DIM + B_ROPE_DIM
B_V_DIM = 64
B_WIDTH = B_HEADS * B_V_DIM
B_QK_PAD = B_HEADS * LANES
Q_LORA = 256
KV_LORA = 128

M_HEADS = 4
M_HEAD_DIM = 128
M_WIDTH = M_HEADS * M_HEAD_DIM

VMEM_LIMIT = 56 * 1024 * 1024

BF16 = jnp.bfloat16
F32 = jnp.float32
NEG_BIG = -1e30
LOG2_E = float(np.log2(np.e))


def _a_pair_perm():
    heads = [np.arange(A_HEAD_DIM), A_HEAD_DIM + np.arange(A_HEAD_DIM)]
    r = A_ROPE_DIM // 2
    rest = (A_HEAD_DIM - A_ROPE_DIM) // 2
    first = [h[:r] for h in heads] + [h[A_ROPE_DIM:A_ROPE_DIM + rest] for h in heads]
    second = [h[r:A_ROPE_DIM] for h in heads] + [h[A_ROPE_DIM + rest:] for h in heads]
    return np.concatenate(first + second)


ROPE_SPAN = 16
_A_PERM = _a_pair_perm()
_A_PERM_ALL = np.concatenate([c * LANES + _A_PERM for c in range(A_WIDTH // LANES)])
_B_SPLIT = LANES // 2 - B_ROPE_DIM // 2
_B_PERM = np.concatenate([np.arange(B_NOPE_DIM, B_NOPE_DIM + B_ROPE_DIM // 2), np.arange(0, _B_SPLIT),
                          np.arange(B_NOPE_DIM + B_ROPE_DIM // 2, B_QK_DIM), np.arange(_B_SPLIT, B_NOPE_DIM)])


def _lane_mask(member):
    lane = lax.broadcasted_iota(jnp.int32, (1, LANES), 1)
    edges = np.flatnonzero(np.diff(np.concatenate([[0], member.astype(np.int8), [0]])))
    mask = None
    for lo, hi in zip(edges[::2], edges[1::2]):
        run = (lane >= int(lo)) & (lane < int(hi))
        mask = run if mask is None else (mask | run)
    return mask


def _params(n_axes):
    return pltpu.CompilerParams(dimension_semantics=("arbitrary",) * n_axes,
                                vmem_limit_bytes=VMEM_LIMIT)


def _row_rms_scale(x, width):
    return lax.rsqrt(jnp.sum(x * x, axis=-1, keepdims=True) * (1.0 / width) + EPS)


def _dot(a, b):
    return jnp.dot(a, b, preferred_element_type=F32)


def _dot_nt(a, b):
    return lax.dot_general(a, b, (((1,), (1,)), ((), ())), preferred_element_type=F32)


def _rope_chunk(xc, cos, sin):
    return xc * cos + pltpu.roll(xc, LANES // 2, axis=1) * sin


def _proj_kernel(x_ref, gmix_ref, win_ref, gqa_ref, gka_ref, gcq_ref, gckv_ref,
                 wuq_ref, wuk_ref, wuv_ref, gqb_ref, gkb_ref, aones_ref, ones_ref, taba_ref, tabb_ref,
                 qa_ref, ka_ref, va_ref, qb_ref, kb_ref, vbt_ref):
    x = x_ref[...]
    h = (x * _row_rms_scale(x, x.shape[-1]) * gmix_ref[...]).astype(BF16)
    proj = _dot(h, win_ref[...])

    cos_a, sin_a = taba_ref[0], taba_ref[1]
    cos_b, sin_b = tabb_ref[0], tabb_ref[1]

    def a_heads(col0, g_ref, out_ref):
        for c2 in range(A_WIDTH // (2 * LANES)):
            xs = proj[:, col0 + 2 * c2 * LANES: col0 + 2 * (c2 + 1) * LANES]
            ssq = _dot((xs * xs).astype(BF16), aones_ref[...])
            for i in range(2):
                c = 2 * c2 + i
                lanes = slice(i * LANES, (i + 1) * LANES)
                rs = lax.rsqrt(ssq[:, lanes] * (1.0 / A_HEAD_DIM) + EPS)
                y = xs[:, lanes] * rs * g_ref[:, c * LANES:(c + 1) * LANES]
                y = _rope_chunk(y, cos_a, sin_a)
                out_ref[:, c * LANES:(c + 1) * LANES] = y.astype(out_ref.dtype)

    a_heads(0, gqa_ref, qa_ref)
    a_heads(A_WIDTH, gka_ref, ka_ref)
    va_ref[...] = proj[:, 2 * A_WIDTH:3 * A_WIDTH].astype(va_ref.dtype)

    c0 = 3 * A_WIDTH
    cq = proj[:, c0:c0 + Q_LORA]
    ckv = proj[:, c0 + Q_LORA:c0 + Q_LORA + KV_LORA]
    kr = proj[:, c0 + Q_LORA + KV_LORA:c0 + Q_LORA + 2 * KV_LORA]
    cqn = (cq * _row_rms_scale(cq, Q_LORA) * gcq_ref[...]).astype(BF16)
    ckvn = (ckv * _row_rms_scale(ckv, KV_LORA) * gckv_ref[...]).astype(BF16)
    qb = _dot(cqn, wuq_ref[...])
    kn = _dot(ckvn, wuk_ref[...])
    vbt_ref[...] = _dot(ckvn, wuv_ref[...]).T.astype(vbt_ref.dtype)

    def b_heads(get_chunk, g_ref, out_ref):
        for c2 in range(B_HEADS // 2):
            xs = [get_chunk(2 * c2 + i) for i in range(2)]
            ssq = _dot(jnp.concatenate([xc * xc for xc in xs], axis=1).astype(BF16), ones_ref[...])
            for i, xc in enumerate(xs):
                c = 2 * c2 + i
                rs = lax.rsqrt(ssq[:, i * LANES:(i + 1) * LANES] * (1.0 / B_QK_DIM) + EPS)
                y = xc * rs * g_ref[:, c * LANES:(c + 1) * LANES]
                y = _rope_chunk(y, cos_b, sin_b)
                out_ref[:, c * LANES:(c + 1) * LANES] = y.astype(out_ref.dtype)

    b_heads(lambda c: qb[:, c * LANES:(c + 1) * LANES], gqb_ref, qb_ref)
    b_heads(lambda c: kn[:, c * LANES:(c + 1) * LANES] + kr, gkb_ref, kb_ref)


def _proj(x2d, seq, w, tm):
    n, d = x2d.shape
    nblk_seq = seq // tm
    row = lambda i: (i, 0)
    const = lambda i: (0, 0)
    tab = lambda i: (0, i % nblk_seq, 0)
    full = lambda a: pl.BlockSpec(a.shape, const)
    widths = [A_WIDTH, A_WIDTH, A_WIDTH, B_QK_PAD, B_QK_PAD]
    return pl.pallas_call(
        _proj_kernel,
        grid=(n // tm,),
        in_specs=[pl.BlockSpec((tm, d), row), full(w['gmix']), full(w['win']), full(w['gqa']), full(w['gka']),
                  full(w['gcq']), full(w['gckv']), full(w['wuq']), full(w['wuk']), full(w['wuv']),
                  full(w['gqb']), full(w['gkb']), full(w['pair_ones']), full(w['head_ones']),
                  pl.BlockSpec((2, tm, LANES), tab), pl.BlockSpec((2, tm, LANES), tab)],
        out_specs=[pl.BlockSpec((tm, wd), row) for wd in widths]
                  + [pl.BlockSpec((B_WIDTH, tm), lambda i: (0, i))],
        out_shape=[jax.ShapeDtypeStruct((n, wd), F32 if wd == A_WIDTH else BF16) for wd in widths]
                  + [jax.ShapeDtypeStruct((B_WIDTH, n), BF16)],
        compiler_params=_params(1),
        name="proj",
    )(x2d, w['gmix'], w['win'], w['gqa'], w['gka'], w['gcq'], w['gckv'], w['wuq'], w['wuk'], w['wuv'],
      w['gqb'], w['gkb'], w['pair_ones'], w['head_ones'], w['taba'], w['tabb'])


DIL_TQ = 128
DIL_WIN = DIL_TQ + 2 * A_HALF_WINDOW
DIL_SLOTS = 4


def _strided_rows(start, size, stride):
    return pl.ds(start, size) if stride == 1 else pl.ds(start, size, stride=stride)


def _dilated_kernel(q_ref, k_ref, v_ref, o_ref, lse_scr, s_scr, bias_scr):
    seq = q_ref.shape[1]
    lane = lax.broadcasted_iota(jnp.int32, (1, LANES), 1)
    low_head = lane < A_HEAD_DIM
    q_first = _lane_mask(_A_PERM < A_HEAD_DIM)
    rel = (lax.broadcasted_iota(jnp.int32, (2 * DIL_TQ, DIL_WIN), 1)
           - lax.broadcasted_iota(jnp.int32, (2 * DIL_TQ, DIL_WIN), 0) % DIL_TQ)
    for place in range(3):
        bias_scr[place] = jnp.where(jnp.abs(rel - place * A_HALF_WINDOW) <= A_HALF_WINDOW, 0.0, NEG_BIG)

    for dil in DILATIONS:
        sub_len = seq // dil
        n_blk = sub_len // DIL_TQ
        n_total = dil * n_blk

        def coords(idx, dil=dil, sub_len=sub_len, n_blk=n_blk):
            r = idx // n_blk
            q0 = (idx % n_blk) * DIL_TQ
            k0 = jnp.clip(q0 - A_HALF_WINDOW, 0, sub_len - DIL_WIN)
            rows_q = _strided_rows(r + dil * q0, DIL_TQ, dil)
            rows_k = _strided_rows(r + dil * k0, DIL_WIN, dil)
            return rows_q, rows_k, (q0 - k0) // A_HALF_WINDOW

        def scores_into(slot, idx, coords=coords):
            rows_q, rows_k, place = coords(idx)
            qp = q_ref[0, rows_q, :]
            kw = k_ref[0, rows_k, :].astype(BF16)
            q2 = jnp.concatenate([jnp.where(q_first, qp, 0.0), jnp.where(q_first, 0.0, qp)], axis=0).astype(BF16)
            s_scr[slot] = _dot_nt(q2, kw) + bias_scr[place]

        def finish(slot, idx, dil=dil, coords=coords):
            rows_q, rows_k, _ = coords(idx)
            vw = v_ref[0, rows_k, :].astype(BF16)
            m2 = jnp.max(s_scr[slot], axis=-1, keepdims=True)
            pr = jnp.exp2(s_scr[slot] - m2)
            den2 = jnp.sum(pr, axis=-1, keepdims=True)
            o2 = _dot(pr.astype(BF16), vw)
            pick = lambda a: jnp.where(low_head, a[:DIL_TQ], a[DIL_TQ:])
            den = pick(den2)
            o = pick(o2) / den
            lse = pick(m2) + jnp.log2(den)
            if dil != DILATIONS[0]:
                o_prev, lse_prev = o_ref[0, rows_q, :], lse_scr[rows_q, :]
                top = jnp.maximum(lse_prev, lse)
                w_prev, w_new = jnp.exp2(lse_prev - top), jnp.exp2(lse - top)
                tot = w_prev + w_new
                o = (w_prev * o_prev + w_new * o) / tot
                lse = top + jnp.log2(tot)
            o_ref[0, rows_q, :] = o
            if dil != DILATIONS[-1]:
                lse_scr[rows_q, :] = lse

        assert n_total % DIL_SLOTS == 0
        for slot in range(DIL_SLOTS - 1):
            scores_into(slot, slot)

        def step(i, carry, scores_into=scores_into, finish=finish, n_total=n_total):
            for slot in range(DIL_SLOTS):
                idx = DIL_SLOTS * i + slot
                ahead = idx + DIL_SLOTS - 1
                scores_into((slot + DIL_SLOTS - 1) % DIL_SLOTS, jnp.where(ahead >= n_total, ahead - n_total, ahead))
                finish(slot, idx)
            return carry

        lax.fori_loop(0, n_total // DIL_SLOTS, step, 0)


def _dilated(qa, ka, va):
    b, s, w = qa.shape
    spec = pl.BlockSpec((1, s, LANES), lambda bi, p: (bi, 0, p))
    out = pl.pallas_call(
        _dilated_kernel,
        grid=(b, w // LANES),
        in_specs=[spec, spec, spec],
        out_specs=spec,
        out_shape=jax.ShapeDtypeStruct((b, s, w), F32),
        scratch_shapes=[pltpu.VMEM((s, LANES), F32),
                        pltpu.VMEM((DIL_SLOTS, 2 * DIL_TQ, DIL_WIN), F32),
                        pltpu.VMEM((3, 2 * DIL_TQ, DIL_WIN), F32)],
        compiler_params=_params(2),
        name="dilated",
    )(qa, ka, va)
    return out.reshape(b * s, w)


MLA_DEN_ROWS = 16
MLA_SLOTS = 4
MLA_AHEAD = 2


def _mla_kernel(q_ref, k_ref, vt_ref, o_ref, s_scr, smax_scr, *, tq, tk):
    seq = k_ref.shape[1]
    heads = range(2)
    n_blocks = seq // tk
    n_tiles = seq // tq
    assert seq % (MLA_SLOTS * tk) == 0 and seq % tq == 0

    def scores_into(slot, tile, blk, hh):
        q0 = pl.multiple_of(tile * tq, tq)
        k0 = pl.multiple_of(blk * tk, tk)
        cols = slice(hh * LANES, (hh + 1) * LANES)
        st = _dot_nt(k_ref[0, pl.ds(k0, tk), cols], q_ref[0, pl.ds(q0, tq), cols])
        s_scr[slot, hh] = st
        smax_scr[slot, hh] = jnp.max(st.reshape(tk // 8, 8, tq), axis=0)

    for blk in range(MLA_AHEAD):
        for hh in heads:
            scores_into(blk, 0, blk, hh)

    def tile_body(tile, carry):
        def body(t, state):
            state = list(state)
            for slot in range(MLA_SLOTS):
                blk = MLA_SLOTS * t + slot
                wrap = blk + MLA_AHEAD >= n_blocks
                ahead = ((slot + MLA_AHEAD) % MLA_SLOTS,
                         jnp.where(wrap, jnp.minimum(tile + 1, n_tiles - 1), tile),
                         jnp.where(wrap, blk + MLA_AHEAD - n_blocks, blk + MLA_AHEAD))
                k0 = pl.multiple_of(blk * tk, tk)
                for hh in heads:
                    scores_into(*ahead, hh)
                    m, acc = state[hh]
                    m_new = jnp.maximum(m, jnp.max(smax_scr[slot, hh], axis=0, keepdims=True))
                    alpha = jnp.exp2(m - m_new)
                    pt = jnp.exp2(s_scr[slot, hh] - m_new).astype(BF16)
                    vt1 = jnp.concatenate([vt_ref[hh * B_V_DIM:(hh + 1) * B_V_DIM, pl.ds(k0, tk)],
                                           jnp.ones((MLA_DEN_ROWS, tk), BF16)], axis=0)
                    acc = alpha * acc + _dot(vt1, pt)
                    state[hh] = (m_new, acc)
            return tuple(state)

        init = tuple((jnp.full((1, tq), NEG_BIG, F32), jnp.zeros((B_V_DIM + MLA_DEN_ROWS, tq), F32))
                     for _ in heads)
        final = lax.fori_loop(0, n_blocks // MLA_SLOTS, body, init)
        ot = jnp.concatenate([acc[:B_V_DIM] / acc[B_V_DIM:B_V_DIM + 1] for _, acc in final], axis=0)
        o_ref[0, pl.ds(pl.multiple_of(tile * tq, tq), tq), :] = ot.T
        return carry

    lax.fori_loop(0, n_tiles, tile_body, 0)


def _mla(qb, kb, vbt, tq, tk):
    b, s, _ = qb.shape
    qk_spec = pl.BlockSpec((1, s, 2 * LANES), lambda bi, p: (bi, 0, p))
    out = pl.pallas_call(
        functools.partial(_mla_kernel, tq=tq, tk=tk),
        grid=(b, B_HEADS // 2),
        in_specs=[qk_spec, qk_spec, pl.BlockSpec((LANES, s), lambda bi, p: (p, bi))],
        out_specs=pl.BlockSpec((1, s, LANES), lambda bi, p: (bi, 0, p)),
        out_shape=jax.ShapeDtypeStruct((b, s, B_WIDTH), F32),
        scratch_shapes=[pltpu.VMEM((MLA_SLOTS, 2, tk, tq), F32), pltpu.VMEM((MLA_SLOTS, 2, 8, tq), F32)],
        compiler_params=_params(2),
        name="mla",
    )(qb, kb, vbt)
    return out.reshape(b * s, B_WIDTH)


def _mem_kv_kernel(mem_ref, g_ref, wkv_ref, gk_ref, k_ref, v_ref):
    mem = mem_ref[0]
    mm = (mem * _row_rms_scale(mem, mem.shape[-1]) * g_ref[...]).astype(BF16)
    kv = _dot(mm, wkv_ref[...])
    for hd in range(M_HEADS):
        cols = slice(hd * M_HEAD_DIM, (hd + 1) * M_HEAD_DIM)
        kh = kv[:, cols]
        k_ref[0, :, cols] = (kh * _row_rms_scale(kh, M_HEAD_DIM) * gk_ref[...]).astype(k_ref.dtype)
    v_ref[0] = kv[:, M_WIDTH:].astype(v_ref.dtype)


def _mem_kv(mem, w):
    b, n_mem, d = mem.shape
    const = lambda bi: (0, 0)
    full = lambda a: pl.BlockSpec(a.shape, const)
    blk = pl.BlockSpec((1, n_mem, M_WIDTH), lambda bi: (bi, 0, 0))
    return pl.pallas_call(
        _mem_kv_kernel,
        grid=(b,),
        in_specs=[pl.BlockSpec((1, n_mem, d), lambda bi: (bi, 0, 0)),
                  full(w['gmem_kv']), full(w['wkv_m']), full(w['gk_m'])],
        out_specs=[blk, blk],
        out_shape=[jax.ShapeDtypeStruct((b, n_mem, M_WIDTH), BF16)] * 2,
        compiler_params=_params(1),
        name="mem_kv",
    )(mem, w['gmem_kv'], w['wkv_m'], w['gk_m'])


FFN_CHUNK = 1024


def _post_kernel(x_ref, oa_ref, ob_ref, ga_ref, gb_ref, wa_ref, wb_ref,
                 gmem_ref, wq_ref, gq_ref, k_ref, v_ref, wo_ref, gffn_ref, w1_ref, w2_ref, y_ref):
    oa, ob = oa_ref[0], ob_ref[0]
    na = (oa * _row_rms_scale(oa, A_WIDTH) * ga_ref[...]).astype(BF16)
    nb = (ob * _row_rms_scale(ob, B_WIDTH) * gb_ref[...]).astype(BF16)
    x = x_ref[0] + _dot(na, wa_ref[...]) + _dot(nb, wb_ref[...])

    h = (x * _row_rms_scale(x, x.shape[-1]) * gmem_ref[...]).astype(BF16)
    q = _dot(h, wq_ref[...])
    heads = []
    for hd in range(M_HEADS):
        cols = slice(hd * M_HEAD_DIM, (hd + 1) * M_HEAD_DIM)
        qh = q[:, cols]
        qh = (qh * _row_rms_scale(qh, M_HEAD_DIM) * gq_ref[...]).astype(BF16)
        s = _dot_nt(qh, k_ref[0, :, cols])
        pr = jnp.exp(s - jnp.max(s, axis=-1, keepdims=True))
        pr = pr / jnp.sum(pr, axis=-1, keepdims=True)
        heads.append(_dot(pr.astype(BF16), v_ref[0, :, cols]).astype(BF16))
    x = x + _dot(jnp.concatenate(heads, axis=-1), wo_ref[...])

    h = (x * _row_rms_scale(x, x.shape[-1]) * gffn_ref[...]).astype(BF16)
    for c in range(w1_ref.shape[1] // FFN_CHUNK):
        cols = slice(c * FFN_CHUNK, (c + 1) * FFN_CHUNK)
        a = jnp.maximum(_dot(h, w1_ref[:, cols]), 0.0)
        x = x + _dot((a * a).astype(BF16), w2_ref[cols, :])
    y_ref[0] = x


def _post(x3d, oa, ob, km, vm, w, tm):
    b, s, d = x3d.shape
    n_mem = km.shape[1]
    const = lambda bi, i: (0, 0)
    full = lambda a: pl.BlockSpec(a.shape, const, pipeline_mode=pl.Buffered(1))
    tile = lambda width: pl.BlockSpec((1, tm, width), lambda bi, i: (bi, i, 0))
    kvblk = pl.BlockSpec((1, n_mem, M_WIDTH), lambda bi, i: (bi, 0, 0))
    names = ['ga_out', 'gb_out', 'wout_a', 'wout_b', 'gmem', 'wq_m', 'gq_m']
    tail = ['wo_m', 'gffn', 'w1', 'w2']
    return pl.pallas_call(
        _post_kernel,
        grid=(b, s // tm),
        in_specs=[tile(d), tile(A_WIDTH), tile(B_WIDTH)] + [full(w[n]) for n in names] + [kvblk, kvblk]
                 + [full(w[n]) for n in tail],
        out_specs=tile(d),
        out_shape=jax.ShapeDtypeStruct((b, s, d), F32),
        compiler_params=_params(2),
        name="post",
    )(x3d, oa, ob, *[w[n] for n in names], km, vm, *[w[n] for n in tail])


def _rope_tables(seq, rope_dim):
    half = rope_dim // 2
    assert ROPE_SPAN % half == 0
    inv = ROPE_THETA ** (-jnp.arange(0, rope_dim, 2, dtype=F32) / rope_dim)
    ang = jnp.arange(seq, dtype=F32)[:, None] * inv[None, :]
    cos, sin = jnp.cos(ang), jnp.sin(ang)
    lane = np.arange(LANES)
    is_lo = lane < ROPE_SPAN
    is_hi = (lane >= LANES // 2) & (lane < LANES // 2 + ROPE_SPAN)
    idx = (lane % (LANES // 2)) % half
    cos_t = jnp.where(is_lo | is_hi, cos[:, idx], 1.0)
    sin_t = jnp.where(is_lo, -sin[:, idx], jnp.where(is_hi, sin[:, idx], 0.0))
    return jnp.stack([cos_t, sin_t]).astype(F32)


def _layer_weights(l, p):
    row = lambda v: v.reshape(1, -1).astype(F32)
    w_in = p['w_in'][l]
    d = w_in.shape[0]
    c_kr = 3 * A_WIDTH + Q_LORA + KV_LORA
    half_r = B_ROPE_DIM // 2
    zeros = lambda n: jnp.zeros((d, n), F32)
    kr = w_in[:, c_kr:]
    win = jnp.concatenate([
        w_in[:, :A_WIDTH][:, _A_PERM_ALL], w_in[:, A_WIDTH:2 * A_WIDTH][:, _A_PERM_ALL], w_in[:, 2 * A_WIDTH:c_kr],
        kr[:, :half_r], zeros(LANES // 2 - half_r), kr[:, half_r:], zeros(LANES // 2 - half_r)], axis=1)

    def b_lanes(a, with_rope):
        if with_rope:
            a = a[..., _B_PERM]
        else:
            z = jnp.zeros(a.shape[:-1] + (half_r,), a.dtype)
            a = jnp.concatenate([z, a[..., :_B_SPLIT], z, a[..., _B_SPLIT:]], axis=-1)
        return jnp.pad(a, [(0, 0)] * (a.ndim - 1) + [(0, LANES - B_QK_DIM)])

    wukv = p['w_ukv'][l].reshape(KV_LORA, B_HEADS, B_NOPE_DIM + B_V_DIM)
    b_gain = lambda g: jnp.tile(b_lanes(g, True), B_HEADS)
    a_gain = lambda g: jnp.tile(g, A_HEADS)[_A_PERM_ALL]
    w_out = p['w_out'][l]
    return {
        'gmix': row(p['norm_mix_g'][l]),
        'win': win.astype(BF16),
        'gqa': row(a_gain(p['a_q_norm_g'][l]) * (A_HEAD_DIM ** -0.5 * LOG2_E)),
        'gka': row(a_gain(p['a_k_norm_g'][l])),
        'gcq': row(p['b_cq_norm_g'][l]),
        'gckv': row(p['b_ckv_norm_g'][l]),
        'wuq': b_lanes(p['w_uq'][l].reshape(Q_LORA, B_HEADS, B_QK_DIM), True).reshape(Q_LORA, B_QK_PAD).astype(BF16),
        'wuk': b_lanes(wukv[:, :, :B_NOPE_DIM], False).reshape(KV_LORA, B_QK_PAD).astype(BF16),
        'wuv': wukv[:, :, B_NOPE_DIM:].reshape(KV_LORA, B_WIDTH).astype(BF16),
        'gqb': row(b_gain(p['b_q_norm_g'][l]) * (B_QK_DIM ** -0.5 * LOG2_E)),
        'gkb': row(b_gain(p['b_k_norm_g'][l])),
        'ga_out': row(p['a_out_norm_g'][l]),
        'gb_out': row(p['b_out_norm_g'][l]),
        'wout_a': w_out[:A_WIDTH].astype(BF16),
        'wout_b': w_out[A_WIDTH:].astype(BF16),
        'gmem': row(p['norm_mem_g'][l]),
        'gmem_kv': row(p['mem_kv_norm_g'][l]),
        'wq_m': p['m_wq'][l].astype(BF16),
        'wkv_m': p['m_wkv'][l].astype(BF16),
        'gq_m': row(p['m_q_norm_g'][l] * (M_HEAD_DIM ** -0.5)),
        'gk_m': row(p['m_k_norm_g'][l]),
        'wo_m': p['m_wo'][l].astype(BF16),
        'gffn': row(p['norm_ffn_g'][l]),
        'w1': p['w_ff1'][l].astype(BF16),
        'w2': p['w_ff2'][l].astype(BF16),
    }


def _tiles(seq):
    tm = 512 if seq % 512 == 0 else 256
    return dict(tm=tm, tq=256, tk=1024)


def _trunk(x, mem, layers):
    b, s, d = x.shape
    assert s % (max(DILATIONS) * DIL_TQ) == 0 and s // max(DILATIONS) >= DIL_WIN
    t = _tiles(s)
    taba = _rope_tables(s, A_ROPE_DIM)
    tabb = _rope_tables(s, B_ROPE_DIM)
    head_ones = jnp.asarray(np.kron(np.eye(2), np.ones((LANES, LANES))), BF16)
    same_head = (_A_PERM[:, None] < A_HEAD_DIM) == (_A_PERM[None, :] < A_HEAD_DIM)
    pair_ones = jnp.asarray(np.kron(np.eye(2), same_head), BF16)
    x2d = x.reshape(b * s, d)
    for w in layers:
        w = dict(w, taba=taba, tabb=tabb, head_ones=head_ones, pair_ones=pair_ones)
        qa, ka, va, qb, kb, vbt = _proj(x2d, s, w, t['tm'])
        as3d = lambda a: a.reshape(b, s, a.shape[-1])
        oa = _dilated(as3d(qa), as3d(ka), as3d(va))
        ob = _mla(as3d(qb), as3d(kb), vbt, t['tq'], t['tk'])
        km, vm = _mem_kv(mem, w)
        x2d = _post(as3d(x2d), as3d(oa), as3d(ob), km, vm, w, t['tm']).reshape(b * s, d)
    return x2d.reshape(b, s, d)


def kernel(x_prompt, x_sample, mem_prompt, mem_sample, norm_mix_g, w_in, a_q_norm_g, a_k_norm_g, b_cq_norm_g, b_ckv_norm_g, w_uq, w_ukv, b_q_norm_g, b_k_norm_g, a_out_norm_g, b_out_norm_g, w_out, norm_mem_g, mem_kv_norm_g, m_wq, m_wkv, m_q_norm_g, m_k_norm_g, m_wo, norm_ffn_g, w_ff1, w_ff2):
    p = dict(norm_mix_g=norm_mix_g, w_in=w_in, a_q_norm_g=a_q_norm_g, a_k_norm_g=a_k_norm_g,
             b_cq_norm_g=b_cq_norm_g, b_ckv_norm_g=b_ckv_norm_g, w_uq=w_uq, w_ukv=w_ukv,
             b_q_norm_g=b_q_norm_g, b_k_norm_g=b_k_norm_g, a_out_norm_g=a_out_norm_g,
             b_out_norm_g=b_out_norm_g, w_out=w_out, norm_mem_g=norm_mem_g, mem_kv_norm_g=mem_kv_norm_g,
             m_wq=m_wq, m_wkv=m_wkv, m_q_norm_g=m_q_norm_g, m_k_norm_g=m_k_norm_g, m_wo=m_wo,
             norm_ffn_g=norm_ffn_g, w_ff1=w_ff1, w_ff2=w_ff2)
    layers = [_layer_weights(l, p) for l in range(w_in.shape[0])]
    return (_trunk(x_prompt, mem_prompt, layers), _trunk(x_sample, mem_sample, layers))
```

```python
import functools

import numpy as np
import jax
import jax.numpy as jnp
from jax import lax
from jax.experimental import pallas as pl
from jax.experimental.pallas import tpu as pltpu

EPS = 1e-6
ROPE_THETA = 500000.0

LANES = 128
A_HEADS = 8
A_HEAD_DIM = 64
A_WIDTH = A_HEADS * A_HEAD_DIM
A_ROPE_DIM = 16
A_HALF_WINDOW = 64
DILATIONS = (16, 4, 1)

B_HEADS = 8
B_NOPE_DIM = 64
B_ROPE_DIM = 32
B_QK_DIM = B_NOPE_DIM + B_ROPE_DIM
B_V_DIM = 64
B_WIDTH = B_HEADS * B_V_DIM
B_QK_PAD = B_HEADS * LANES
Q_LORA = 256
KV_LORA = 128

M_HEADS = 4
M_HEAD_DIM = 128
M_WIDTH = M_HEADS * M_HEAD_DIM

VMEM_LIMIT = 56 * 1024 * 1024

BF16 = jnp.bfloat16
F32 = jnp.float32
NEG_BIG = -1e30
LOG2_E = float(np.log2(np.e))


def _a_pair_perm():
    heads = [np.arange(A_HEAD_DIM), A_HEAD_DIM + np.arange(A_HEAD_DIM)]
    r = A_ROPE_DIM // 2
    rest = (A_HEAD_DIM - A_ROPE_DIM) // 2
    first = [h[:r] for h in heads] + [h[A_ROPE_DIM:A_ROPE_DIM + rest] for h in heads]
    second = [h[r:A_ROPE_DIM] for h in heads] + [h[A_ROPE_DIM + rest:] for h in heads]
    return np.concatenate(first + second)


ROPE_SPAN = 16
_A_PERM = _a_pair_perm()
_A_PERM_ALL = np.concatenate([c * LANES + _A_PERM for c in range(A_WIDTH // LANES)])
_B_SPLIT = LANES // 2 - B_ROPE_DIM // 2
_B_PERM = np.concatenate([np.arange(B_NOPE_DIM, B_NOPE_DIM + B_ROPE_DIM // 2), np.arange(0, _B_SPLIT),
                          np.arange(B_NOPE_DIM + B_ROPE_DIM // 2, B_QK_DIM), np.arange(_B_SPLIT, B_NOPE_DIM)])


def _lane_mask(member):
    lane = lax.broadcasted_iota(jnp.int32, (1, LANES), 1)
    edges = np.flatnonzero(np.diff(np.concatenate([[0], member.astype(np.int8), [0]])))
    mask = None
    for lo, hi in zip(edges[::2], edges[1::2]):
        run = (lane >= int(lo)) & (lane < int(hi))
        mask = run if mask is None else (mask | run)
    return mask


def _params(n_axes):
    return pltpu.CompilerParams(dimension_semantics=("arbitrary",) * n_axes,
                                vmem_limit_bytes=VMEM_LIMIT)


def _row_rms_scale(x, width):
    return lax.rsqrt(jnp.sum(x * x, axis=-1, keepdims=True) * (1.0 / width) + EPS)


def _dot(a, b):
    return jnp.dot(a, b, preferred_element_type=F32)


def _dot_nt(a, b):
    return lax.dot_general(a, b, (((1,), (1,)), ((), ())), preferred_element_type=F32)


def _rope_chunk(xc, cos, sin):
    return xc * cos + pltpu.roll(xc, LANES // 2, axis=1) * sin


def _proj_kernel(x_ref, gmix_ref, win_ref, gqa_ref, gka_ref, gcq_ref, gckv_ref,
                 wuq_ref, wuk_ref, wuv_ref, gqb_ref, gkb_ref, aones_ref, ones_ref, taba_ref, tabb_ref,
                 qa_ref, ka_ref, va_ref, qb_ref, kb_ref, vbt_ref):
    x = x_ref[...]
    h = (x * _row_rms_scale(x, x.shape[-1]) * gmix_ref[...]).astype(BF16)
    proj = _dot(h, win_ref[...])

    cos_a, sin_a = taba_ref[0], taba_ref[1]
    cos_b, sin_b = tabb_ref[0], tabb_ref[1]

    def a_heads(col0, g_ref, out_ref):
        for c2 in range(A_WIDTH // (2 * LANES)):
            xs = proj[:, col0 + 2 * c2 * LANES: col0 + 2 * (c2 + 1) * LANES]
            ssq = _dot((xs * xs).astype(BF16), aones_ref[...])
            for i in range(2):
                c = 2 * c2 + i
                lanes = slice(i * LANES, (i + 1) * LANES)
                rs = lax.rsqrt(ssq[:, lanes] * (1.0 / A_HEAD_DIM) + EPS)
                y = xs[:, lanes] * rs * g_ref[:, c * LANES:(c + 1) * LANES]
                y = _rope_chunk(y, cos_a, sin_a)
                out_ref[:, c * LANES:(c + 1) * LANES] = y.astype(out_ref.dtype)

    a_heads(0, gqa_ref, qa_ref)
    a_heads(A_WIDTH, gka_ref, ka_ref)
    va_ref[...] = proj[:, 2 * A_WIDTH:3 * A_WIDTH].astype(va_ref.dtype)

    c0 = 3 * A_WIDTH
    cq = proj[:, c0:c0 + Q_LORA]
    ckv = proj[:, c0 + Q_LORA:c0 + Q_LORA + KV_LORA]
    kr = proj[:, c0 + Q_LORA + KV_LORA:c0 + Q_LORA + 2 * KV_LORA]
    cqn = (cq * _row_rms_scale(cq, Q_LORA) * gcq_ref[...]).astype(BF16)
    ckvn = (ckv * _row_rms_scale(ckv, KV_LORA) * gckv_ref[...]).astype(BF16)
    qb = _dot(cqn, wuq_ref[...])
    kn = _dot(ckvn, wuk_ref[...])
    vbt_ref[...] = _dot(ckvn, wuv_ref[...]).T.astype(vbt_ref.dtype)

    def b_heads(get_chunk, g_ref, out_ref):
        for c2 in range(B_HEADS // 2):
            xs = [get_chunk(2 * c2 + i) for i in range(2)]
            ssq = _dot(jnp.concatenate([xc * xc for xc in xs], axis=1).astype(BF16), ones_ref[...])
            for i, xc in enumerate(xs):
                c = 2 * c2 + i
                rs = lax.rsqrt(ssq[:, i * LANES:(i + 1) * LANES] * (1.0 / B_QK_DIM) + EPS)
                y = xc * rs * g_ref[:, c * LANES:(c + 1) * LANES]
                y = _rope_chunk(y, cos_b, sin_b)
                out_ref[:, c * LANES:(c + 1) * LANES] = y.astype(out_ref.dtype)

    b_heads(lambda c: qb[:, c * LANES:(c + 1) * LANES], gqb_ref, qb_ref)
    b_heads(lambda c: kn[:, c * LANES:(c + 1) * LANES] + kr, gkb_ref, kb_ref)


def _proj(x2d, seq, w, tm):
    n, d = x2d.shape
    nblk_seq = seq // tm
    row = lambda i: (i, 0)
    const = lambda i: (0, 0)
    tab = lambda i: (0, i % nblk_seq, 0)
    full = lambda a: pl.BlockSpec(a.shape, const)
    widths = [A_WIDTH, A_WIDTH, A_WIDTH, B_QK_PAD, B_QK_PAD]
    return pl.pallas_call(
        _proj_kernel,
        grid=(n // tm,),
        in_specs=[pl.BlockSpec((tm, d), row), full(w['gmix']), full(w['win']), full(w['gqa']), full(w['gka']),
                  full(w['gcq']), full(w['gckv']), full(w['wuq']), full(w['wuk']), full(w['wuv']),
                  full(w['gqb']), full(w['gkb']), full(w['pair_ones']), full(w['head_ones']),
                  pl.BlockSpec((2, tm, LANES), tab), pl.BlockSpec((2, tm, LANES), tab)],
        out_specs=[pl.BlockSpec((tm, wd), row) for wd in widths]
                  + [pl.BlockSpec((B_WIDTH, tm), lambda i: (0, i))],
        out_shape=[jax.ShapeDtypeStruct((n, wd), F32 if wd == A_WIDTH else BF16) for wd in widths]
                  + [jax.ShapeDtypeStruct((B_WIDTH, n), BF16)],
        compiler_params=_params(1),
        name="proj",
    )(x2d, w['gmix'], w['win'], w['gqa'], w['gka'], w['gcq'], w['gckv'], w['wuq'], w['wuk'], w['wuv'],
      w['gqb'], w['gkb'], w['pair_ones'], w['head_ones'], w['taba'], w['tabb'])


DIL_TQ = 128
DIL_WIN = DIL_TQ + 2 * A_HALF_WINDOW
DIL_SLOTS = 4


def _strided_rows(start, size, stride):
    return pl.ds(start, size) if stride == 1 else pl.ds(start, size, stride=stride)


def _dilated_kernel(q_ref, k_ref, v_ref, o_ref, lse_scr, s_scr, bias_scr):
    seq = q_ref.shape[1]
    lane = lax.broadcasted_iota(jnp.int32, (1, LANES), 1)
    low_head = lane < A_HEAD_DIM
    q_first = _lane_mask(_A_PERM < A_HEAD_DIM)
    rel = (lax.broadcasted_iota(jnp.int32, (2 * DIL_TQ, DIL_WIN), 1)
           - lax.broadcasted_iota(jnp.int32, (2 * DIL_TQ, DIL_WIN), 0) % DIL_TQ)
    for place in range(3):
        bias_scr[place] = jnp.where(jnp.abs(rel - place * A_HALF_WINDOW) <= A_HALF_WINDOW, 0.0, NEG_BIG)

    for dil in DILATIONS:
        sub_len = seq // dil
        n_blk = sub_len // DIL_TQ
        n_total = dil * n_blk

        def coords(idx, dil=dil, sub_len=sub_len, n_blk=n_blk):
            r = idx // n_blk
            q0 = (idx % n_blk) * DIL_TQ
            k0 = jnp.clip(q0 - A_HALF_WINDOW, 0, sub_len - DIL_WIN)
            rows_q = _strided_rows(r + dil * q0, DIL_TQ, dil)
            rows_k = _strided_rows(r + dil * k0, DIL_WIN, dil)
            return rows_q, rows_k, (q0 - k0) // A_HALF_WINDOW

        def scores_into(slot, idx, coords=coords):
            rows_q, rows_k, place = coords(idx)
            qp = q_ref[0, rows_q, :]
            kw = k_ref[0, rows_k, :].astype(BF16)
            q2 = jnp.concatenate([jnp.where(q_first, qp, 0.0), jnp.where(q_first, 0.0, qp)], axis=0).astype(BF16)
            s_scr[slot] = _dot_nt(q2, kw) + bias_scr[place]

        def finish(slot, idx, dil=dil, coords=coords):
            rows_q, rows_k, _ = coords(idx)
            vw = v_ref[0, rows_k, :].astype(BF16)
            m2 = jnp.max(s_scr[slot], axis=-1, keepdims=True)
            pr = jnp.exp2(s_scr[slot] - m2)
            den2 = jnp.sum(pr, axis=-1, keepdims=True)
            o2 = _dot(pr.astype(BF16), vw)
            pick = lambda a: jnp.where(low_head, a[:DIL_TQ], a[DIL_TQ:])
            den = pick(den2)
            o = pick(o2) / den
            lse = pick(m2) + jnp.log2(den)
            if dil != DILATIONS[0]:
                o_prev, lse_prev = o_ref[0, rows_q, :], lse_scr[rows_q, :]
                top = jnp.maximum(lse_prev, lse)
                w_prev, w_new = jnp.exp2(lse_prev - top), jnp.exp2(lse - top)
                tot = w_prev + w_new
                o = (w_prev * o_prev + w_new * o) / tot
                lse = top + jnp.log2(tot)
            o_ref[0, rows_q, :] = o
            if dil != DILATIONS[-1]:
                lse_scr[rows_q, :] = lse

        assert n_total % DIL_SLOTS == 0
        for slot in range(DIL_SLOTS - 1):
            scores_into(slot, slot)

        def step(i, carry, scores_into=scores_into, finish=finish, n_total=n_total):
            for slot in range(DIL_SLOTS):
                idx = DIL_SLOTS * i + slot
                ahead = idx + DIL_SLOTS - 1
                scores_into((slot + DIL_SLOTS - 1) % DIL_SLOTS, jnp.where(ahead >= n_total, ahead - n_total, ahead))
                finish(slot, idx)
            return carry

        lax.fori_loop(0, n_total // DIL_SLOTS, step, 0)


def _dilated(qa, ka, va):
    b, s, w = qa.shape
    spec = pl.BlockSpec((1, s, LANES), lambda bi, p: (bi, 0, p))
    out = pl.pallas_call(
        _dilated_kernel,
        grid=(b, w // LANES),
        in_specs=[spec, spec, spec],
        out_specs=spec,
        out_shape=jax.ShapeDtypeStruct((b, s, w), F32),
        scratch_shapes=[pltpu.VMEM((s, LANES), F32),
                        pltpu.VMEM((DIL_SLOTS, 2 * DIL_TQ, DIL_WIN), F32),
                        pltpu.VMEM((3, 2 * DIL_TQ, DIL_WIN), F32)],
        compiler_params=_params(2),
        name="dilated",
    )(qa, ka, va)
    return out.reshape(b * s, w)


MLA_DEN_ROWS = 16
MLA_SLOTS = 8
MLA_AHEAD = 2


def _mla_kernel(q_ref, k_ref, vt_ref, o_ref, s_scr, smax_scr, *, tq, tk):
    seq = k_ref.shape[1]
    heads = range(2)
    n_blocks = seq // tk
    n_tiles = seq // tq
    assert seq % (MLA_SLOTS * tk) == 0 and seq % tq == 0

    def scores_into(slot, tile, blk, hh):
        q0 = pl.multiple_of(tile * tq, tq)
        k0 = pl.multiple_of(blk * tk, tk)
        cols = slice(hh * LANES, (hh + 1) * LANES)
        st = _dot_nt(k_ref[0, pl.ds(k0, tk), cols], q_ref[0, pl.ds(q0, tq), cols])
        s_scr[slot, hh] = st
        smax_scr[slot, hh] = jnp.max(st.reshape(tk // 8, 8, tq), axis=0)

    for blk in range(MLA_AHEAD):
        for hh in heads:
            scores_into(blk, 0, blk, hh)

    def tile_body(tile, carry):
        def body(t, state):
            state = list(state)
            for slot in range(MLA_SLOTS):
                blk = MLA_SLOTS * t + slot
                wrap = blk + MLA_AHEAD >= n_blocks
                ahead = ((slot + MLA_AHEAD) % MLA_SLOTS,
                         jnp.where(wrap, jnp.minimum(tile + 1, n_tiles - 1), tile),
                         jnp.where(wrap, blk + MLA_AHEAD - n_blocks, blk + MLA_AHEAD))
                k0 = pl.multiple_of(blk * tk, tk)
                for hh in heads:
                    scores_into(*ahead, hh)
                    m, acc = state[hh]
                    m_new = jnp.maximum(m, jnp.max(smax_scr[slot, hh], axis=0, keepdims=True))
                    alpha = jnp.exp2(m - m_new)
                    pt = jnp.exp2(s_scr[slot, hh] - m_new).astype(BF16)
                    vt1 = jnp.concatenate([vt_ref[hh * B_V_DIM:(hh + 1) * B_V_DIM, pl.ds(k0, tk)],
                                           jnp.ones((MLA_DEN_ROWS, tk), BF16)], axis=0)
                    acc = alpha * acc + _dot(vt1, pt)
                    state[hh] = (m_new, acc)
            return tuple(state)

        init = tuple((jnp.full((1, tq), NEG_BIG, F32), jnp.zeros((B_V_DIM + MLA_DEN_ROWS, tq), F32))
                     for _ in heads)
        final = lax.fori_loop(0, n_blocks // MLA_SLOTS, body, init)
        ot = jnp.concatenate([acc[:B_V_DIM] / acc[B_V_DIM:B_V_DIM + 1] for _, acc in final], axis=0)
        o_ref[0, pl.ds(pl.multiple_of(tile * tq, tq), tq), :] = ot.T
        return carry

    lax.fori_loop(0, n_tiles, tile_body, 0)


def _mla(qb, kb, vbt, tq, tk):
    b, s, _ = qb.shape
    qk_spec = pl.BlockSpec((1, s, 2 * LANES), lambda bi, p: (bi, 0, p))
    out = pl.pallas_call(
        functools.partial(_mla_kernel, tq=tq, tk=tk),
        grid=(b, B_HEADS // 2),
        in_specs=[qk_spec, qk_spec, pl.BlockSpec((LANES, s), lambda bi, p: (p, bi))],
        out_specs=pl.BlockSpec((1, s, LANES), lambda bi, p: (bi, 0, p)),
        out_shape=jax.ShapeDtypeStruct((b, s, B_WIDTH), F32),
        scratch_shapes=[pltpu.VMEM((MLA_SLOTS, 2, tk, tq), F32), pltpu.VMEM((MLA_SLOTS, 2, 8, tq), F32)],
        compiler_params=_params(2),
        name="mla",
    )(qb, kb, vbt)
    return out.reshape(b * s, B_WIDTH)


def _mem_kv_kernel(mem_ref, g_ref, wkv_ref, gk_ref, k_ref, v_ref):
    mem = mem_ref[0]
    mm = (mem * _row_rms_scale(mem, mem.shape[-1]) * g_ref[...]).astype(BF16)
    kv = _dot(mm, wkv_ref[...])
    for hd in range(M_HEADS):
        cols = slice(hd * M_HEAD_DIM, (hd + 1) * M_HEAD_DIM)
        kh = kv[:, cols]
        k_ref[0, :, cols] = (kh * _row_rms_scale(kh, M_HEAD_DIM) * gk_ref[...]).astype(k_ref.dtype)
    v_ref[0] = kv[:, M_WIDTH:].astype(v_ref.dtype)


def _mem_kv(mem, w):
    b, n_mem, d = mem.shape
    const = lambda bi: (0, 0)
    full = lambda a: pl.BlockSpec(a.shape, const)
    blk = pl.BlockSpec((1, n_mem, M_WIDTH), lambda bi: (bi, 0, 0))
    return pl.pallas_call(
        _mem_kv_kernel,
        grid=(b,),
        in_specs=[pl.BlockSpec((1, n_mem, d), lambda bi: (bi, 0, 0)),
                  full(w['gmem_kv']), full(w['wkv_m']), full(w['gk_m'])],
        out_specs=[blk, blk],
        out_shape=[jax.ShapeDtypeStruct((b, n_mem, M_WIDTH), BF16)] * 2,
        compiler_params=_params(1),
        name="mem_kv",
    )(mem, w['gmem_kv'], w['wkv_m'], w['gk_m'])


FFN_CHUNK = 1024


def _post_kernel(x_ref, oa_ref, ob_ref, ga_ref, gb_ref, wa_ref, wb_ref,
                 gmem_ref, wq_ref, gq_ref, k_ref, v_ref, wo_ref, gffn_ref, w1_ref, w2_ref, y_ref):
    oa, ob = oa_ref[0], ob_ref[0]
    na = (oa * _row_rms_scale(oa, A_WIDTH) * ga_ref[...]).astype(BF16)
    nb = (ob * _row_rms_scale(ob, B_WIDTH) * gb_ref[...]).astype(BF16)
    x = x_ref[0] + _dot(na, wa_ref[...]) + _dot(nb, wb_ref[...])

    h = (x * _row_rms_scale(x, x.shape[-1]) * gmem_ref[...]).astype(BF16)
    q = _dot(h, wq_ref[...])
    heads = []
    for hd in range(M_HEADS):
        cols = slice(hd * M_HEAD_DIM, (hd + 1) * M_HEAD_DIM)
        qh = q[:, cols]
        qh = (qh * _row_rms_scale(qh, M_HEAD_DIM) * gq_ref[...]).astype(BF16)
        s = _dot_nt(qh, k_ref[0, :, cols])
        pr = jnp.exp(s - jnp.max(s, axis=-1, keepdims=True))
        pr = pr / jnp.sum(pr, axis=-1, keepdims=True)
        heads.append(_dot(pr.astype(BF16), v_ref[0, :, cols]).astype(BF16))
    x = x + _dot(jnp.concatenate(heads, axis=-1), wo_ref[...])

    h = (x * _row_rms_scale(x, x.shape[-1]) * gffn_ref[...]).astype(BF16)
    for c in range(w1_ref.shape[1] // FFN_CHUNK):
        cols = slice(c * FFN_CHUNK, (c + 1) * FFN_CHUNK)
        a = jnp.maximum(_dot(h, w1_ref[:, cols]), 0.0)
        x = x + _dot((a * a).astype(BF16), w2_ref[cols, :])
    y_ref[0] = x


def _post(x3d, oa, ob, km, vm, w, tm):
    b, s, d = x3d.shape
    n_mem = km.shape[1]
    const = lambda bi, i: (0, 0)
    full = lambda a: pl.BlockSpec(a.shape, const, pipeline_mode=pl.Buffered(1))
    tile = lambda width: pl.BlockSpec((1, tm, width), lambda bi, i: (bi, i, 0))
    kvblk = pl.BlockSpec((1, n_mem, M_WIDTH), lambda bi, i: (bi, 0, 0))
    names = ['ga_out', 'gb_out', 'wout_a', 'wout_b', 'gmem', 'wq_m', 'gq_m']
    tail = ['wo_m', 'gffn', 'w1', 'w2']
    return pl.pallas_call(
        _post_kernel,
        grid=(b, s // tm),
        in_specs=[tile(d), tile(A_WIDTH), tile(B_WIDTH)] + [full(w[n]) for n in names] + [kvblk, kvblk]
                 + [full(w[n]) for n in tail],
        out_specs=tile(d),
        out_shape=jax.ShapeDtypeStruct((b, s, d), F32),
        compiler_params=_params(2),
        name="post",
    )(x3d, oa, ob, *[w[n] for n in names], km, vm, *[w[n] for n in tail])


def _rope_tables(seq, rope_dim):
    half = rope_dim // 2
    assert ROPE_SPAN % half == 0
    inv = ROPE_THETA ** (-jnp.arange(0, rope_dim, 2, dtype=F32) / rope_dim)
    ang = jnp.arange(seq, dtype=F32)[:, None] * inv[None, :]
    cos, sin = jnp.cos(ang), jnp.sin(ang)
    lane = np.arange(LANES)
    is_lo = lane < ROPE_SPAN
    is_hi = (lane >= LANES // 2) & (lane < LANES // 2 + ROPE_SPAN)
    idx = (lane % (LANES // 2)) % half
    cos_t = jnp.where(is_lo | is_hi, cos[:, idx], 1.0)
    sin_t = jnp.where(is_lo, -sin[:, idx], jnp.where(is_hi, sin[:, idx], 0.0))
    return jnp.stack([cos_t, sin_t]).astype(F32)


def _layer_weights(l, p):
    row = lambda v: v.reshape(1, -1).astype(F32)
    w_in = p['w_in'][l]
    d = w_in.shape[0]
    c_kr = 3 * A_WIDTH + Q_LORA + KV_LORA
    half_r = B_ROPE_DIM // 2
    zeros = lambda n: jnp.zeros((d, n), F32)
    kr = w_in[:, c_kr:]
    win = jnp.concatenate([
        w_in[:, :A_WIDTH][:, _A_PERM_ALL], w_in[:, A_WIDTH:2 * A_WIDTH][:, _A_PERM_ALL], w_in[:, 2 * A_WIDTH:c_kr],
        kr[:, :half_r], zeros(LANES // 2 - half_r), kr[:, half_r:], zeros(LANES // 2 - half_r)], axis=1)

    def b_lanes(a, with_rope):
        if with_rope:
            a = a[..., _B_PERM]
        else:
            z = jnp.zeros(a.shape[:-1] + (half_r,), a.dtype)
            a = jnp.concatenate([z, a[..., :_B_SPLIT], z, a[..., _B_SPLIT:]], axis=-1)
        return jnp.pad(a, [(0, 0)] * (a.ndim - 1) + [(0, LANES - B_QK_DIM)])

    wukv = p['w_ukv'][l].reshape(KV_LORA, B_HEADS, B_NOPE_DIM + B_V_DIM)
    b_gain = lambda g: jnp.tile(b_lanes(g, True), B_HEADS)
    a_gain = lambda g: jnp.tile(g, A_HEADS)[_A_PERM_ALL]
    w_out = p['w_out'][l]
    return {
        'gmix': row(p['norm_mix_g'][l]),
        'win': win.astype(BF16),
        'gqa': row(a_gain(p['a_q_norm_g'][l]) * (A_HEAD_DIM ** -0.5 * LOG2_E)),
        'gka': row(a_gain(p['a_k_norm_g'][l])),
        'gcq': row(p['b_cq_norm_g'][l]),
        'gckv': row(p['b_ckv_norm_g'][l]),
        'wuq': b_lanes(p['w_uq'][l].reshape(Q_LORA, B_HEADS, B_QK_DIM), True).reshape(Q_LORA, B_QK_PAD).astype(BF16),
        'wuk': b_lanes(wukv[:, :, :B_NOPE_DIM], False).reshape(KV_LORA, B_QK_PAD).astype(BF16),
        'wuv': wukv[:, :, B_NOPE_DIM:].reshape(KV_LORA, B_WIDTH).astype(BF16),
        'gqb': row(b_gain(p['b_q_norm_g'][l]) * (B_QK_DIM ** -0.5 * LOG2_E)),
        'gkb': row(b_gain(p['b_k_norm_g'][l])),
        'ga_out': row(p['a_out_norm_g'][l]),
        'gb_out': row(p['b_out_norm_g'][l]),
        'wout_a': w_out[:A_WIDTH].astype(BF16),
        'wout_b': w_out[A_WIDTH:].astype(BF16),
        'gmem': row(p['norm_mem_g'][l]),
        'gmem_kv': row(p['mem_kv_norm_g'][l]),
        'wq_m': p['m_wq'][l].astype(BF16),
        'wkv_m': p['m_wkv'][l].astype(BF16),
        'gq_m': row(p['m_q_norm_g'][l] * (M_HEAD_DIM ** -0.5)),
        'gk_m': row(p['m_k_norm_g'][l]),
        'wo_m': p['m_wo'][l].astype(BF16),
        'gffn': row(p['norm_ffn_g'][l]),
        'w1': p['w_ff1'][l].astype(BF16),
        'w2': p['w_ff2'][l].astype(BF16),
    }


def _tiles(seq):
    tm = 512 if seq % 512 == 0 else 256
    return dict(tm=tm, tq=512, tk=512)


def _trunk(x, mem, layers):
    b, s, d = x.shape
    assert s % (max(DILATIONS) * DIL_TQ) == 0 and s // max(DILATIONS) >= DIL_WIN
    t = _tiles(s)
    taba = _rope_tables(s, A_ROPE_DIM)
    tabb = _rope_tables(s, B_ROPE_DIM)
    head_ones = jnp.asarray(np.kron(np.eye(2), np.ones((LANES, LANES))), BF16)
    same_head = (_A_PERM[:, None] < A_HEAD_DIM) == (_A_PERM[None, :] < A_HEAD_DIM)
    pair_ones = jnp.asarray(np.kron(np.eye(2), same_head), BF16)
    x2d = x.reshape(b * s, d)
    for w in layers:
        w = dict(w, taba=taba, tabb=tabb, head_ones=head_ones, pair_ones=pair_ones)
        qa, ka, va, qb, kb, vbt = _proj(x2d, s, w, t['tm'])
        as3d = lambda a: a.reshape(b, s, a.shape[-1])
        oa = _dilated(as3d(qa), as3d(ka), as3d(va))
        ob = _mla(as3d(qb), as3d(kb), vbt, t['tq'], t['tk'])
        km, vm = _mem_kv(mem, w)
        x2d = _post(as3d(x2d), as3d(oa), as3d(ob), km, vm, w, t['tm']).reshape(b * s, d)
    return x2d.reshape(b, s, d)


def kernel(x_prompt, x_sample, mem_prompt, mem_sample, norm_mix_g, w_in, a_q_norm_g, a_k_norm_g, b_cq_norm_g, b_ckv_norm_g, w_uq, w_ukv, b_q_norm_g, b_k_norm_g, a_out_norm_g, b_out_norm_g, w_out, norm_mem_g, mem_kv_norm_g, m_wq, m_wkv, m_q_norm_g, m_k_norm_g, m_wo, norm_ffn_g, w_ff1, w_ff2):
    p = dict(norm_mix_g=norm_mix_g, w_in=w_in, a_q_norm_g=a_q_norm_g, a_k_norm_g=a_k_norm_g,
             b_cq_norm_g=b_cq_norm_g, b_ckv_norm_g=b_ckv_norm_g, w_uq=w_uq, w_ukv=w_ukv,
             b_q_norm_g=b_q_norm_g, b_k_norm_g=b_k_norm_g, a_out_norm_g=a_out_norm_g,
             b_out_norm_g=b_out_norm_g, w_out=w_out, norm_mem_g=norm_mem_g, mem_kv_norm_g=mem_kv_norm_g,
             m_wq=m_wq, m_wkv=m_wkv, m_q_norm_g=m_q_norm_g, m_k_norm_g=m_k_norm_g, m_wo=m_wo,
             norm_ffn_g=norm_ffn_g, w_ff1=w_ff1, w_ff2=w_ff2)
    layers = [_layer_weights(l, p) for l in range(w_in.shape[0])]
    return (_trunk(x_prompt, mem_prompt, layers), _trunk(x_sample, mem_sample, layers))
```

```python
import functools

import numpy as np
import jax
import jax.numpy as jnp
from jax import lax
from jax.experimental import pallas as pl
from jax.experimental.pallas import tpu as pltpu

EPS = 1e-6
ROPE_THETA = 500000.0

LANES = 128
A_HEADS = 8
A_HEAD_DIM = 64
A_WIDTH = A_HEADS * A_HEAD_DIM
A_ROPE_DIM = 16
A_HALF_WINDOW = 64
DILATIONS = (16, 4, 1)

B_HEADS = 8
B_NOPE_DIM = 64
B_ROPE_DIM = 32
B_QK_DIM = B_NOPE_DIM + B_ROPE_DIM
B_V_DIM = 64
B_WIDTH = B_HEADS * B_V_DIM
B_QK_PAD = B_HEADS * LANES
Q_LORA = 256
KV_LORA = 128

M_HEADS = 4
M_HEAD_DIM = 128
M_WIDTH = M_HEADS * M_HEAD_DIM

VMEM_LIMIT = 56 * 1024 * 1024

BF16 = jnp.bfloat16
F32 = jnp.float32
NEG_BIG = -1e30
LOG2_E = float(np.log2(np.e))


def _a_pair_perm():
    heads = [np.arange(A_HEAD_DIM), A_HEAD_DIM + np.arange(A_HEAD_DIM)]
    r = A_ROPE_DIM // 2
    rest = (A_HEAD_DIM - A_ROPE_DIM) // 2
    first = [h[:r] for h in heads] + [h[A_ROPE_DIM:A_ROPE_DIM + rest] for h in heads]
    second = [h[r:A_ROPE_DIM] for h in heads] + [h[A_ROPE_DIM + rest:] for h in heads]
    return np.concatenate(first + second)


ROPE_SPAN = 16
_A_PERM = _a_pair_perm()
_A_PERM_ALL = np.concatenate([c * LANES + _A_PERM for c in range(A_WIDTH // LANES)])
_B_SPLIT = LANES // 2 - B_ROPE_DIM // 2
_B_PERM = np.concatenate([np.arange(B_NOPE_DIM, B_NOPE_DIM + B_ROPE_DIM // 2), np.arange(0, _B_SPLIT),
                          np.arange(B_NOPE_DIM + B_ROPE_DIM // 2, B_QK_DIM), np.arange(_B_SPLIT, B_NOPE_DIM)])


def _lane_mask(member):
    lane = lax.broadcasted_iota(jnp.int32, (1, LANES), 1)
    edges = np.flatnonzero(np.diff(np.concatenate([[0], member.astype(np.int8), [0]])))
    mask = None
    for lo, hi in zip(edges[::2], edges[1::2]):
        run = (lane >= int(lo)) & (lane < int(hi))
        mask = run if mask is None else (mask | run)
    return mask


def _params(n_axes):
    return pltpu.CompilerParams(dimension_semantics=("arbitrary",) * n_axes,
                                vmem_limit_bytes=VMEM_LIMIT)


def _row_rms_scale(x, width):
    return lax.rsqrt(jnp.sum(x * x, axis=-1, keepdims=True) * (1.0 / width) + EPS)


def _dot(a, b):
    return jnp.dot(a, b, preferred_element_type=F32)


def _dot_nt(a, b):
    return lax.dot_general(a, b, (((1,), (1,)), ((), ())), preferred_element_type=F32)


def _rope_chunk(xc, cos, sin):
    return xc * cos + pltpu.roll(xc, LANES // 2, axis=1) * sin


def _proj_kernel(x_ref, gmix_ref, win_ref, gqa_ref, gka_ref, gcq_ref, gckv_ref,
                 wuq_ref, wuk_ref, wuv_ref, gqb_ref, gkb_ref, aones_ref, ones_ref, taba_ref, tabb_ref,
                 qa_ref, ka_ref, va_ref, qb_ref, kb_ref, vbt_ref):
    x = x_ref[...]
    h = (x * _row_rms_scale(x, x.shape[-1]) * gmix_ref[...]).astype(BF16)
    proj = _dot(h, win_ref[...])

    cos_a, sin_a = taba_ref[0], taba_ref[1]
    cos_b, sin_b = tabb_ref[0], tabb_ref[1]

    def a_heads(col0, g_ref, out_ref):
        for c2 in range(A_WIDTH // (2 * LANES)):
            xs = proj[:, col0 + 2 * c2 * LANES: col0 + 2 * (c2 + 1) * LANES]
            ssq = _dot((xs * xs).astype(BF16), aones_ref[...])
            for i in range(2):
                c = 2 * c2 + i
                lanes = slice(i * LANES, (i + 1) * LANES)
                rs = lax.rsqrt(ssq[:, lanes] * (1.0 / A_HEAD_DIM) + EPS)
                y = xs[:, lanes] * rs * g_ref[:, c * LANES:(c + 1) * LANES]
                y = _rope_chunk(y, cos_a, sin_a)
                out_ref[:, c * LANES:(c + 1) * LANES] = y.astype(out_ref.dtype)

    a_heads(0, gqa_ref, qa_ref)
    a_heads(A_WIDTH, gka_ref, ka_ref)
    va_ref[...] = proj[:, 2 * A_WIDTH:3 * A_WIDTH].astype(va_ref.dtype)

    c0 = 3 * A_WIDTH
    cq = proj[:, c0:c0 + Q_LORA]
    ckv = proj[:, c0 + Q_LORA:c0 + Q_LORA + KV_LORA]
    kr = proj[:, c0 + Q_LORA + KV_LORA:c0 + Q_LORA + 2 * KV_LORA]
    cqn = (cq * _row_rms_scale(cq, Q_LORA) * gcq_ref[...]).astype(BF16)
    ckvn = (ckv * _row_rms_scale(ckv, KV_LORA) * gckv_ref[...]).astype(BF16)
    qb = _dot(cqn, wuq_ref[...])
    kn = _dot(ckvn, wuk_ref[...])
    vbt_ref[...] = _dot(ckvn, wuv_ref[...]).T.astype(vbt_ref.dtype)

    def b_heads(get_chunk, g_ref, out_ref):
        for c2 in range(B_HEADS // 2):
            xs = [get_chunk(2 * c2 + i) for i in range(2)]
            ssq = _dot(jnp.concatenate([xc * xc for xc in xs], axis=1).astype(BF16), ones_ref[...])
            for i, xc in enumerate(xs):
                c = 2 * c2 + i
                rs = lax.rsqrt(ssq[:, i * LANES:(i + 1) * LANES] * (1.0 / B_QK_DIM) + EPS)
                y = xc * rs * g_ref[:, c * LANES:(c + 1) * LANES]
                y = _rope_chunk(y, cos_b, sin_b)
                out_ref[:, c * LANES:(c + 1) * LANES] = y.astype(out_ref.dtype)

    b_heads(lambda c: qb[:, c * LANES:(c + 1) * LANES], gqb_ref, qb_ref)
    b_heads(lambda c: kn[:, c * LANES:(c + 1) * LANES] + kr, gkb_ref, kb_ref)


def _proj(x2d, seq, w, tm):
    n, d = x2d.shape
    nblk_seq = seq // tm
    row = lambda i: (i, 0)
    const = lambda i: (0, 0)
    tab = lambda i: (0, i % nblk_seq, 0)
    full = lambda a: pl.BlockSpec(a.shape, const)
    widths = [A_WIDTH, A_WIDTH, A_WIDTH, B_QK_PAD, B_QK_PAD]
    return pl.pallas_call(
        _proj_kernel,
        grid=(n // tm,),
        in_specs=[pl.BlockSpec((tm, d), row), full(w['gmix']), full(w['win']), full(w['gqa']), full(w['gka']),
                  full(w['gcq']), full(w['gckv']), full(w['wuq']), full(w['wuk']), full(w['wuv']),
                  full(w['gqb']), full(w['gkb']), full(w['pair_ones']), full(w['head_ones']),
                  pl.BlockSpec((2, tm, LANES), tab), pl.BlockSpec((2, tm, LANES), tab)],
        out_specs=[pl.BlockSpec((tm, wd), row) for wd in widths]
                  + [pl.BlockSpec((B_WIDTH, tm), lambda i: (0, i))],
        out_shape=[jax.ShapeDtypeStruct((n, wd), F32 if wd == A_WIDTH else BF16) for wd in widths]
                  + [jax.ShapeDtypeStruct((B_WIDTH, n), BF16)],
        compiler_params=_params(1),
        name="proj",
    )(x2d, w['gmix'], w['win'], w['gqa'], w['gka'], w['gcq'], w['gckv'], w['wuq'], w['wuk'], w['wuv'],
      w['gqb'], w['gkb'], w['pair_ones'], w['head_ones'], w['taba'], w['tabb'])


DIL_TQ = 128
DIL_WIN = DIL_TQ + 2 * A_HALF_WINDOW
DIL_SLOTS = 4


def _strided_rows(start, size, stride):
    return pl.ds(start, size) if stride == 1 else pl.ds(start, size, stride=stride)


def _dilated_kernel(q_ref, k_ref, v_ref, o_ref, lse_scr, s_scr, bias_scr):
    seq = q_ref.shape[1]
    lane = lax.broadcasted_iota(jnp.int32, (1, LANES), 1)
    low_head = lane < A_HEAD_DIM
    q_first = _lane_mask(_A_PERM < A_HEAD_DIM)
    rel = (lax.broadcasted_iota(jnp.int32, (2 * DIL_TQ, DIL_WIN), 1)
           - lax.broadcasted_iota(jnp.int32, (2 * DIL_TQ, DIL_WIN), 0) % DIL_TQ)
    for place in range(3):
        bias_scr[place] = jnp.where(jnp.abs(rel - place * A_HALF_WINDOW) <= A_HALF_WINDOW, 0.0, NEG_BIG)

    for dil in DILATIONS:
        sub_len = seq // dil
        n_blk = sub_len // DIL_TQ
        n_total = dil * n_blk

        def coords(idx, dil=dil, sub_len=sub_len, n_blk=n_blk):
            r = idx // n_blk
            q0 = (idx % n_blk) * DIL_TQ
            k0 = jnp.clip(q0 - A_HALF_WINDOW, 0, sub_len - DIL_WIN)
            rows_q = _strided_rows(r + dil * q0, DIL_TQ, dil)
            rows_k = _strided_rows(r + dil * k0, DIL_WIN, dil)
            return rows_q, rows_k, (q0 - k0) // A_HALF_WINDOW

        def scores_into(slot, idx, coords=coords):
            rows_q, rows_k, place = coords(idx)
            qp = q_ref[0, rows_q, :]
            kw = k_ref[0, rows_k, :].astype(BF16)
            q2 = jnp.concatenate([jnp.where(q_first, qp, 0.0), jnp.where(q_first, 0.0, qp)], axis=0).astype(BF16)
            s_scr[slot] = _dot_nt(q2, kw) + bias_scr[place]

        def finish(slot, idx, dil=dil, coords=coords):
            rows_q, rows_k, _ = coords(idx)
            vw = v_ref[0, rows_k, :].astype(BF16)
            m2 = jnp.max(s_scr[slot], axis=-1, keepdims=True)
            pr = jnp.exp2(s_scr[slot] - m2)
            den2 = jnp.sum(pr, axis=-1, keepdims=True)
            o2 = _dot(pr.astype(BF16), vw)
            pick = lambda a: jnp.where(low_head, a[:DIL_TQ], a[DIL_TQ:])
            den = pick(den2)
            o = pick(o2) / den
            lse = pick(m2) + jnp.log2(den)
            if dil != DILATIONS[0]:
                o_prev, lse_prev = o_ref[0, rows_q, :], lse_scr[rows_q, :]
                top = jnp.maximum(lse_prev, lse)
                w_prev, w_new = jnp.exp2(lse_prev - top), jnp.exp2(lse - top)
                tot = w_prev + w_new
                o = (w_prev * o_prev + w_new * o) / tot
                lse = top + jnp.log2(tot)
            o_ref[0, rows_q, :] = o
            if dil != DILATIONS[-1]:
                lse_scr[rows_q, :] = lse

        assert n_total % DIL_SLOTS == 0
        for slot in range(DIL_SLOTS - 1):
            scores_into(slot, slot)

        def step(i, carry, scores_into=scores_into, finish=finish, n_total=n_total):
            for slot in range(DIL_SLOTS):
                idx = DIL_SLOTS * i + slot
                ahead = idx + DIL_SLOTS - 1
                scores_into((slot + DIL_SLOTS - 1) % DIL_SLOTS, jnp.where(ahead >= n_total, ahead - n_total, ahead))
                finish(slot, idx)
            return carry

        lax.fori_loop(0, n_total // DIL_SLOTS, step, 0)


def _dilated(qa, ka, va):
    b, s, w = qa.shape
    spec = pl.BlockSpec((1, s, LANES), lambda bi, p: (bi, 0, p))
    out = pl.pallas_call(
        _dilated_kernel,
        grid=(b, w // LANES),
        in_specs=[spec, spec, spec],
        out_specs=spec,
        out_shape=jax.ShapeDtypeStruct((b, s, w), F32),
        scratch_shapes=[pltpu.VMEM((s, LANES), F32),
                        pltpu.VMEM((DIL_SLOTS, 2 * DIL_TQ, DIL_WIN), F32),
                        pltpu.VMEM((3, 2 * DIL_TQ, DIL_WIN), F32)],
        compiler_params=_params(2),
        name="dilated",
    )(qa, ka, va)
    return out.reshape(b * s, w)


MLA_DEN_ROWS = 16
MLA_SLOTS = 16
MLA_AHEAD = 2


def _mla_kernel(q_ref, k_ref, vt_ref, o_ref, s_scr, smax_scr, *, tq, tk):
    seq = k_ref.shape[1]
    heads = range(2)
    n_blocks = seq // tk
    n_tiles = seq // tq
    slots = s_scr.shape[0]
    assert seq % (slots * tk) == 0 and seq % tq == 0

    def scores_into(slot, tile, blk, hh):
        q0 = pl.multiple_of(tile * tq, tq)
        k0 = pl.multiple_of(blk * tk, tk)
        cols = slice(hh * LANES, (hh + 1) * LANES)
        st = _dot_nt(k_ref[0, pl.ds(k0, tk), cols], q_ref[0, pl.ds(q0, tq), cols])
        s_scr[slot, hh] = st
        smax_scr[slot, hh] = jnp.max(st.reshape(tk // 8, 8, tq), axis=0)

    for blk in range(MLA_AHEAD):
        for hh in heads:
            scores_into(blk, 0, blk, hh)

    def tile_body(tile, carry):
        def body(t, state):
            state = list(state)
            for slot in range(slots):
                blk = slots * t + slot
                wrap = blk + MLA_AHEAD >= n_blocks
                ahead = ((slot + MLA_AHEAD) % slots,
                         jnp.where(wrap, jnp.minimum(tile + 1, n_tiles - 1), tile),
                         jnp.where(wrap, blk + MLA_AHEAD - n_blocks, blk + MLA_AHEAD))
                k0 = pl.multiple_of(blk * tk, tk)
                for hh in heads:
                    scores_into(*ahead, hh)
                    m, acc = state[hh]
                    m_new = jnp.maximum(m, jnp.max(smax_scr[slot, hh], axis=0, keepdims=True))
                    alpha = jnp.exp2(m - m_new)
                    pt = jnp.exp2(s_scr[slot, hh] - m_new).astype(BF16)
                    vt1 = jnp.concatenate([vt_ref[hh * B_V_DIM:(hh + 1) * B_V_DIM, pl.ds(k0, tk)],
                                           jnp.ones((MLA_DEN_ROWS, tk), BF16)], axis=0)
                    acc = alpha * acc + _dot(vt1, pt)
                    state[hh] = (m_new, acc)
            return tuple(state)

        init = tuple((jnp.full((1, tq), NEG_BIG, F32), jnp.zeros((B_V_DIM + MLA_DEN_ROWS, tq), F32))
                     for _ in heads)
        final = lax.fori_loop(0, n_blocks // slots, body, init)
        ot = jnp.concatenate([acc[:B_V_DIM] / acc[B_V_DIM:B_V_DIM + 1] for _, acc in final], axis=0)
        o_ref[0, pl.ds(pl.multiple_of(tile * tq, tq), tq), :] = ot.T
        return carry

    lax.fori_loop(0, n_tiles, tile_body, 0)


def _mla(qb, kb, vbt, tq, tk):
    b, s, _ = qb.shape
    qk_spec = pl.BlockSpec((1, s, 2 * LANES), lambda bi, p: (bi, 0, p))
    slots = min(MLA_SLOTS, s // tk)
    out = pl.pallas_call(
        functools.partial(_mla_kernel, tq=tq, tk=tk),
        grid=(b, B_HEADS // 2),
        in_specs=[qk_spec, qk_spec, pl.BlockSpec((LANES, s), lambda bi, p: (p, bi))],
        out_specs=pl.BlockSpec((1, s, LANES), lambda bi, p: (bi, 0, p)),
        out_shape=jax.ShapeDtypeStruct((b, s, B_WIDTH), F32),
        scratch_shapes=[pltpu.VMEM((slots, 2, tk, tq), F32), pltpu.VMEM((slots, 2, 8, tq), F32)],
        compiler_params=_params(2),
        name="mla",
    )(qb, kb, vbt)
    return out.reshape(b * s, B_WIDTH)


def _mem_kv_kernel(mem_ref, g_ref, wkv_ref, gk_ref, k_ref, v_ref):
    mem = mem_ref[0]
    mm = (mem * _row_rms_scale(mem, mem.shape[-1]) * g_ref[...]).astype(BF16)
    kv = _dot(mm, wkv_ref[...])
    for hd in range(M_HEADS):
        cols = slice(hd * M_HEAD_DIM, (hd + 1) * M_HEAD_DIM)
        kh = kv[:, cols]
        k_ref[0, :, cols] = (kh * _row_rms_scale(kh, M_HEAD_DIM) * gk_ref[...]).astype(k_ref.dtype)
    v_ref[0] = kv[:, M_WIDTH:].astype(v_ref.dtype)


def _mem_kv(mem, w):
    b, n_mem, d = mem.shape
    const = lambda bi: (0, 0)
    full = lambda a: pl.BlockSpec(a.shape, const)
    blk = pl.BlockSpec((1, n_mem, M_WIDTH), lambda bi: (bi, 0, 0))
    return pl.pallas_call(
        _mem_kv_kernel,
        grid=(b,),
        in_specs=[pl.BlockSpec((1, n_mem, d), lambda bi: (bi, 0, 0)),
                  full(w['gmem_kv']), full(w['wkv_m']), full(w['gk_m'])],
        out_specs=[blk, blk],
        out_shape=[jax.ShapeDtypeStruct((b, n_mem, M_WIDTH), BF16)] * 2,
        compiler_params=_params(1),
        name="mem_kv",
    )(mem, w['gmem_kv'], w['wkv_m'], w['gk_m'])


FFN_CHUNK = 1024


def _post_kernel(x_ref, oa_ref, ob_ref, ga_ref, gb_ref, wa_ref, wb_ref,
                 gmem_ref, wq_ref, gq_ref, k_ref, v_ref, wo_ref, gffn_ref, w1_ref, w2_ref, y_ref):
    oa, ob = oa_ref[0], ob_ref[0]
    na = (oa * _row_rms_scale(oa, A_WIDTH) * ga_ref[...]).astype(BF16)
    nb = (ob * _row_rms_scale(ob, B_WIDTH) * gb_ref[...]).astype(BF16)
    x = x_ref[0] + _dot(na, wa_ref[...]) + _dot(nb, wb_ref[...])

    h = (x * _row_rms_scale(x, x.shape[-1]) * gmem_ref[...]).astype(BF16)
    q = _dot(h, wq_ref[...])
    heads = []
    for hd in range(M_HEADS):
        cols = slice(hd * M_HEAD_DIM, (hd + 1) * M_HEAD_DIM)
        qh = q[:, cols]
        qh = (qh * _row_rms_scale(qh, M_HEAD_DIM) * gq_ref[...]).astype(BF16)
        s = _dot_nt(qh, k_ref[0, :, cols])
        pr = jnp.exp(s - jnp.max(s, axis=-1, keepdims=True))
        pr = pr / jnp.sum(pr, axis=-1, keepdims=True)
        heads.append(_dot(pr.astype(BF16), v_ref[0, :, cols]).astype(BF16))
    x = x + _dot(jnp.concatenate(heads, axis=-1), wo_ref[...])

    h = (x * _row_rms_scale(x, x.shape[-1]) * gffn_ref[...]).astype(BF16)
    for c in range(w1_ref.shape[1] // FFN_CHUNK):
        cols = slice(c * FFN_CHUNK, (c + 1) * FFN_CHUNK)
        a = jnp.maximum(_dot(h, w1_ref[:, cols]), 0.0)
        x = x + _dot((a * a).astype(BF16), w2_ref[cols, :])
    y_ref[0] = x


def _post(x3d, oa, ob, km, vm, w, tm):
    b, s, d = x3d.shape
    n_mem = km.shape[1]
    const = lambda bi, i: (0, 0)
    full = lambda a: pl.BlockSpec(a.shape, const, pipeline_mode=pl.Buffered(1))
    tile = lambda width: pl.BlockSpec((1, tm, width), lambda bi, i: (bi, i, 0))
    kvblk = pl.BlockSpec((1, n_mem, M_WIDTH), lambda bi, i: (bi, 0, 0))
    names = ['ga_out', 'gb_out', 'wout_a', 'wout_b', 'gmem', 'wq_m', 'gq_m']
    tail = ['wo_m', 'gffn', 'w1', 'w2']
    return pl.pallas_call(
        _post_kernel,
        grid=(b, s // tm),
        in_specs=[tile(d), tile(A_WIDTH), tile(B_WIDTH)] + [full(w[n]) for n in names] + [kvblk, kvblk]
                 + [full(w[n]) for n in tail],
        out_specs=tile(d),
        out_shape=jax.ShapeDtypeStruct((b, s, d), F32),
        compiler_params=_params(2),
        name="post",
    )(x3d, oa, ob, *[w[n] for n in names], km, vm, *[w[n] for n in tail])


def _rope_tables(seq, rope_dim):
    half = rope_dim // 2
    assert ROPE_SPAN % half == 0
    inv = ROPE_THETA ** (-jnp.arange(0, rope_dim, 2, dtype=F32) / rope_dim)
    ang = jnp.arange(seq, dtype=F32)[:, None] * inv[None, :]
    cos, sin = jnp.cos(ang), jnp.sin(ang)
    lane = np.arange(LANES)
    is_lo = lane < ROPE_SPAN
    is_hi = (lane >= LANES // 2) & (lane < LANES // 2 + ROPE_SPAN)
    idx = (lane % (LANES // 2)) % half
    cos_t = jnp.where(is_lo | is_hi, cos[:, idx], 1.0)
    sin_t = jnp.where(is_lo, -sin[:, idx], jnp.where(is_hi, sin[:, idx], 0.0))
    return jnp.stack([cos_t, sin_t]).astype(F32)


def _layer_weights(l, p):
    row = lambda v: v.reshape(1, -1).astype(F32)
    w_in = p['w_in'][l]
    d = w_in.shape[0]
    c_kr = 3 * A_WIDTH + Q_LORA + KV_LORA
    half_r = B_ROPE_DIM // 2
    zeros = lambda n: jnp.zeros((d, n), F32)
    kr = w_in[:, c_kr:]
    win = jnp.concatenate([
        w_in[:, :A_WIDTH][:, _A_PERM_ALL], w_in[:, A_WIDTH:2 * A_WIDTH][:, _A_PERM_ALL], w_in[:, 2 * A_WIDTH:c_kr],
        kr[:, :half_r], zeros(LANES // 2 - half_r), kr[:, half_r:], zeros(LANES // 2 - half_r)], axis=1)

    def b_lanes(a, with_rope):
        if with_rope:
            a = a[..., _B_PERM]
        else:
            z = jnp.zeros(a.shape[:-1] + (half_r,), a.dtype)
            a = jnp.concatenate([z, a[..., :_B_SPLIT], z, a[..., _B_SPLIT:]], axis=-1)
        return jnp.pad(a, [(0, 0)] * (a.ndim - 1) + [(0, LANES - B_QK_DIM)])

    wukv = p['w_ukv'][l].reshape(KV_LORA, B_HEADS, B_NOPE_DIM + B_V_DIM)
    b_gain = lambda g: jnp.tile(b_lanes(g, True), B_HEADS)
    a_gain = lambda g: jnp.tile(g, A_HEADS)[_A_PERM_ALL]
    w_out = p['w_out'][l]
    return {
        'gmix': row(p['norm_mix_g'][l]),
        'win': win.astype(BF16),
        'gqa': row(a_gain(p['a_q_norm_g'][l]) * (A_HEAD_DIM ** -0.5 * LOG2_E)),
        'gka': row(a_gain(p['a_k_norm_g'][l])),
        'gcq': row(p['b_cq_norm_g'][l]),
        'gckv': row(p['b_ckv_norm_g'][l]),
        'wuq': b_lanes(p['w_uq'][l].reshape(Q_LORA, B_HEADS, B_QK_DIM), True).reshape(Q_LORA, B_QK_PAD).astype(BF16),
        'wuk': b_lanes(wukv[:, :, :B_NOPE_DIM], False).reshape(KV_LORA, B_QK_PAD).astype(BF16),
        'wuv': wukv[:, :, B_NOPE_DIM:].reshape(KV_LORA, B_WIDTH).astype(BF16),
        'gqb': row(b_gain(p['b_q_norm_g'][l]) * (B_QK_DIM ** -0.5 * LOG2_E)),
        'gkb': row(b_gain(p['b_k_norm_g'][l])),
        'ga_out': row(p['a_out_norm_g'][l]),
        'gb_out': row(p['b_out_norm_g'][l]),
        'wout_a': w_out[:A_WIDTH].astype(BF16),
        'wout_b': w_out[A_WIDTH:].astype(BF16),
        'gmem': row(p['norm_mem_g'][l]),
        'gmem_kv': row(p['mem_kv_norm_g'][l]),
        'wq_m': p['m_wq'][l].astype(BF16),
        'wkv_m': p['m_wkv'][l].astype(BF16),
        'gq_m': row(p['m_q_norm_g'][l] * (M_HEAD_DIM ** -0.5)),
        'gk_m': row(p['m_k_norm_g'][l]),
        'wo_m': p['m_wo'][l].astype(BF16),
        'gffn': row(p['norm_ffn_g'][l]),
        'w1': p['w_ff1'][l].astype(BF16),
        'w2': p['w_ff2'][l].astype(BF16),
    }


def _tiles(seq):
    tm = 512 if seq % 512 == 0 else 256
    return dict(tm=tm, tq=256, tk=512)


def _trunk(x, mem, layers):
    b, s, d = x.shape
    assert s % (max(DILATIONS) * DIL_TQ) == 0 and s // max(DILATIONS) >= DIL_WIN
    t = _tiles(s)
    taba = _rope_tables(s, A_ROPE_DIM)
    tabb = _rope_tables(s, B_ROPE_DIM)
    head_ones = jnp.asarray(np.kron(np.eye(2), np.ones((LANES, LANES))), BF16)
    same_head = (_A_PERM[:, None] < A_HEAD_DIM) == (_A_PERM[None, :] < A_HEAD_DIM)
    pair_ones = jnp.asarray(np.kron(np.eye(2), same_head), BF16)
    x2d = x.reshape(b * s, d)
    for w in layers:
        w = dict(w, taba=taba, tabb=tabb, head_ones=head_ones, pair_ones=pair_ones)
        qa, ka, va, qb, kb, vbt = _proj(x2d, s, w, t['tm'])
        as3d = lambda a: a.reshape(b, s, a.shape[-1])
        oa = _dilated(as3d(qa), as3d(ka), as3d(va))
        ob = _mla(as3d(qb), as3d(kb), vbt, t['tq'], t['tk'])
        km, vm = _mem_kv(mem, w)
        x2d = _post(as3d(x2d), as3d(oa), as3d(ob), km, vm, w, t['tm']).reshape(b * s, d)
    return x2d.reshape(b, s, d)


def kernel(x_prompt, x_sample, mem_prompt, mem_sample, norm_mix_g, w_in, a_q_norm_g, a_k_norm_g, b_cq_norm_g, b_ckv_norm_g, w_uq, w_ukv, b_q_norm_g, b_k_norm_g, a_out_norm_g, b_out_norm_g, w_out, norm_mem_g, mem_kv_norm_g, m_wq, m_wkv, m_q_norm_g, m_k_norm_g, m_wo, norm_ffn_g, w_ff1, w_ff2):
    p = dict(norm_mix_g=norm_mix_g, w_in=w_in, a_q_norm_g=a_q_norm_g, a_k_norm_g=a_k_norm_g,
             b_cq_norm_g=b_cq_norm_g, b_ckv_norm_g=b_ckv_norm_g, w_uq=w_uq, w_ukv=w_ukv,
             b_q_norm_g=b_q_norm_g, b_k_norm_g=b_k_norm_g, a_out_norm_g=a_out_norm_g,
             b_out_norm_g=b_out_norm_g, w_out=w_out, norm_mem_g=norm_mem_g, mem_kv_norm_g=mem_kv_norm_g,
             m_wq=m_wq, m_wkv=m_wkv, m_q_norm_g=m_q_norm_g, m_k_norm_g=m_k_norm_g, m_wo=m_wo,
             norm_ffn_g=norm_ffn_g, w_ff1=w_ff1, w_ff2=w_ff2)
    layers = [_layer_weights(l, p) for l in range(w_in.shape[0])]
    return (_trunk(x_prompt, mem_prompt, layers), _trunk(x_sample, mem_sample, layers))
```

```python
import functools

import numpy as np
import jax
import jax.numpy as jnp
from jax import lax
from jax.experimental import pallas as pl
from jax.experimental.pallas import tpu as pltpu

EPS = 1e-6
ROPE_THETA = 500000.0

LANES = 128
A_HEADS = 8
A_HEAD_DIM = 64
A_WIDTH = A_HEADS * A_HEAD_DIM
A_ROPE_DIM = 16
A_HALF_WINDOW = 64
DILATIONS = (16, 4, 1)

B_HEADS = 8
B_NOPE_DIM = 64
B_ROPE_DIM = 32
B_QK_DIM = B_NOPE_DIM + B_ROPE_DIM
B_V_DIM = 64
B_WIDTH = B_HEADS * B_V_DIM
B_QK_PAD = B_HEADS * LANES
Q_LORA = 256
KV_LORA = 128

M_HEADS = 4
M_HEAD_DIM = 128
M_WIDTH = M_HEADS * M_HEAD_DIM

VMEM_LIMIT = 56 * 1024 * 1024

BF16 = jnp.bfloat16
F32 = jnp.float32
NEG_BIG = -1e30
LOG2_E = float(np.log2(np.e))


def _a_pair_perm():
    heads = [np.arange(A_HEAD_DIM), A_HEAD_DIM + np.arange(A_HEAD_DIM)]
    r = A_ROPE_DIM // 2
    rest = (A_HEAD_DIM - A_ROPE_DIM) // 2
    first = [h[:r] for h in heads] + [h[A_ROPE_DIM:A_ROPE_DIM + rest] for h in heads]
    second = [h[r:A_ROPE_DIM] for h in heads] + [h[A_ROPE_DIM + rest:] for h in heads]
    return np.concatenate(first + second)


ROPE_SPAN = 16
_A_PERM = _a_pair_perm()
_A_PERM_ALL = np.concatenate([c * LANES + _A_PERM for c in range(A_WIDTH // LANES)])
_B_SPLIT = LANES // 2 - B_ROPE_DIM // 2
_B_PERM = np.concatenate([np.arange(B_NOPE_DIM, B_NOPE_DIM + B_ROPE_DIM // 2), np.arange(0, _B_SPLIT),
                          np.arange(B_NOPE_DIM + B_ROPE_DIM // 2, B_QK_DIM), np.arange(_B_SPLIT, B_NOPE_DIM)])


def _lane_mask(member):
    lane = lax.broadcasted_iota(jnp.int32, (1, LANES), 1)
    edges = np.flatnonzero(np.diff(np.concatenate([[0], member.astype(np.int8), [0]])))
    mask = None
    for lo, hi in zip(edges[::2], edges[1::2]):
        run = (lane >= int(lo)) & (lane < int(hi))
        mask = run if mask is None else (mask | run)
    return mask


def _params(n_axes):
    return pltpu.CompilerParams(dimension_semantics=("arbitrary",) * n_axes,
                                vmem_limit_bytes=VMEM_LIMIT)


def _row_rms_scale(x, width):
    return lax.rsqrt(jnp.sum(x * x, axis=-1, keepdims=True) * (1.0 / width) + EPS)


def _dot(a, b):
    return jnp.dot(a, b, preferred_element_type=F32)


def _dot_nt(a, b):
    return lax.dot_general(a, b, (((1,), (1,)), ((), ())), preferred_element_type=F32)


def _rope_chunk(xc, cos, sin):
    return xc * cos + pltpu.roll(xc, LANES // 2, axis=1) * sin


def _proj_kernel(x_ref, gmix_ref, win_ref, gqa_ref, gka_ref, gcq_ref, gckv_ref,
                 wuq_ref, wuk_ref, wuv_ref, gqb_ref, gkb_ref, aones_ref, ones_ref, taba_ref, tabb_ref,
                 qa_ref, ka_ref, va_ref, qb_ref, kb_ref, vbt_ref):
    x = x_ref[...]
    h = (x * _row_rms_scale(x, x.shape[-1]) * gmix_ref[...]).astype(BF16)
    proj = _dot(h, win_ref[...])

    cos_a, sin_a = taba_ref[0], taba_ref[1]
    cos_b, sin_b = tabb_ref[0], tabb_ref[1]

    def a_heads(col0, g_ref, out_ref):
        for c2 in range(A_WIDTH // (2 * LANES)):
            xs = proj[:, col0 + 2 * c2 * LANES: col0 + 2 * (c2 + 1) * LANES]
            ssq = _dot((xs * xs).astype(BF16), aones_ref[...])
            for i in range(2):
                c = 2 * c2 + i
                lanes = slice(i * LANES, (i + 1) * LANES)
                rs = lax.rsqrt(ssq[:, lanes] * (1.0 / A_HEAD_DIM) + EPS)
                y = xs[:, lanes] * rs * g_ref[:, c * LANES:(c + 1) * LANES]
                y = _rope_chunk(y, cos_a, sin_a)
                out_ref[:, c * LANES:(c + 1) * LANES] = y.astype(out_ref.dtype)

    a_heads(0, gqa_ref, qa_ref)
    a_heads(A_WIDTH, gka_ref, ka_ref)
    va_ref[...] = proj[:, 2 * A_WIDTH:3 * A_WIDTH].astype(va_ref.dtype)

    c0 = 3 * A_WIDTH
    cq = proj[:, c0:c0 + Q_LORA]
    ckv = proj[:, c0 + Q_LORA:c0 + Q_LORA + KV_LORA]
    kr = proj[:, c0 + Q_LORA + KV_LORA:c0 + Q_LORA + 2 * KV_LORA]
    cqn = (cq * _row_rms_scale(cq, Q_LORA) * gcq_ref[...]).astype(BF16)
    ckvn = (ckv * _row_rms_scale(ckv, KV_LORA) * gckv_ref[...]).astype(BF16)
    qb = _dot(cqn, wuq_ref[...])
    kn = _dot(ckvn, wuk_ref[...])
    vbt_ref[...] = _dot(ckvn, wuv_ref[...]).T.astype(vbt_ref.dtype)

    def b_heads(get_chunk, g_ref, out_ref):
        for c2 in range(B_HEADS // 2):
            xs = [get_chunk(2 * c2 + i) for i in range(2)]
            ssq = _dot(jnp.concatenate([xc * xc for xc in xs], axis=1).astype(BF16), ones_ref[...])
            for i, xc in enumerate(xs):
                c = 2 * c2 + i
                rs = lax.rsqrt(ssq[:, i * LANES:(i + 1) * LANES] * (1.0 / B_QK_DIM) + EPS)
                y = xc * rs * g_ref[:, c * LANES:(c + 1) * LANES]
                y = _rope_chunk(y, cos_b, sin_b)
                out_ref[:, c * LANES:(c + 1) * LANES] = y.astype(out_ref.dtype)

    b_heads(lambda c: qb[:, c * LANES:(c + 1) * LANES], gqb_ref, qb_ref)
    b_heads(lambda c: kn[:, c * LANES:(c + 1) * LANES] + kr, gkb_ref, kb_ref)


def _proj(x2d, seq, w, tm):
    n, d = x2d.shape
    nblk_seq = seq // tm
    row = lambda i: (i, 0)
    const = lambda i: (0, 0)
    tab = lambda i: (0, i % nblk_seq, 0)
    full = lambda a: pl.BlockSpec(a.shape, const)
    widths = [A_WIDTH, A_WIDTH, A_WIDTH, B_QK_PAD, B_QK_PAD]
    return pl.pallas_call(
        _proj_kernel,
        grid=(n // tm,),
        in_specs=[pl.BlockSpec((tm, d), row), full(w['gmix']), full(w['win']), full(w['gqa']), full(w['gka']),
                  full(w['gcq']), full(w['gckv']), full(w['wuq']), full(w['wuk']), full(w['wuv']),
                  full(w['gqb']), full(w['gkb']), full(w['pair_ones']), full(w['head_ones']),
                  pl.BlockSpec((2, tm, LANES), tab), pl.BlockSpec((2, tm, LANES), tab)],
        out_specs=[pl.BlockSpec((tm, wd), row) for wd in widths]
                  + [pl.BlockSpec((B_WIDTH, tm), lambda i: (0, i))],
        out_shape=[jax.ShapeDtypeStruct((n, wd), F32 if wd == A_WIDTH else BF16) for wd in widths]
                  + [jax.ShapeDtypeStruct((B_WIDTH, n), BF16)],
        compiler_params=_params(1),
        name="proj",
    )(x2d, w['gmix'], w['win'], w['gqa'], w['gka'], w['gcq'], w['gckv'], w['wuq'], w['wuk'], w['wuv'],
      w['gqb'], w['gkb'], w['pair_ones'], w['head_ones'], w['taba'], w['tabb'])


DIL_TQ = 128
DIL_WIN = DIL_TQ + 2 * A_HALF_WINDOW
DIL_SLOTS = 4


def _strided_rows(start, size, stride):
    return pl.ds(start, size) if stride == 1 else pl.ds(start, size, stride=stride)


def _dilated_kernel(q_ref, k_ref, v_ref, o_ref, lse_scr, s_scr, bias_scr):
    seq = q_ref.shape[1]
    lane = lax.broadcasted_iota(jnp.int32, (1, LANES), 1)
    low_head = lane < A_HEAD_DIM
    q_first = _lane_mask(_A_PERM < A_HEAD_DIM)
    rel = (lax.broadcasted_iota(jnp.int32, (2 * DIL_TQ, DIL_WIN), 1)
           - lax.broadcasted_iota(jnp.int32, (2 * DIL_TQ, DIL_WIN), 0) % DIL_TQ)
    for place in range(3):
        bias_scr[place] = jnp.where(jnp.abs(rel - place * A_HALF_WINDOW) <= A_HALF_WINDOW, 0.0, NEG_BIG)

    for dil in DILATIONS:
        sub_len = seq // dil
        n_blk = sub_len // DIL_TQ
        n_total = dil * n_blk

        def coords(idx, dil=dil, sub_len=sub_len, n_blk=n_blk):
            r = idx // n_blk
            q0 = (idx % n_blk) * DIL_TQ
            k0 = jnp.clip(q0 - A_HALF_WINDOW, 0, sub_len - DIL_WIN)
            rows_q = _strided_rows(r + dil * q0, DIL_TQ, dil)
            rows_k = _strided_rows(r + dil * k0, DIL_WIN, dil)
            return rows_q, rows_k, (q0 - k0) // A_HALF_WINDOW

        def scores_into(slot, idx, coords=coords):
            rows_q, rows_k, place = coords(idx)
            qp = q_ref[0, rows_q, :]
            kw = k_ref[0, rows_k, :].astype(BF16)
            q2 = jnp.concatenate([jnp.where(q_first, qp, 0.0), jnp.where(q_first, 0.0, qp)], axis=0).astype(BF16)
            s_scr[slot] = _dot_nt(q2, kw) + bias_scr[place]

        def finish(slot, idx, dil=dil, coords=coords):
            rows_q, rows_k, _ = coords(idx)
            vw = v_ref[0, rows_k, :].astype(BF16)
            m2 = jnp.max(s_scr[slot], axis=-1, keepdims=True)
            pr = jnp.exp2(s_scr[slot] - m2)
            den2 = jnp.sum(pr, axis=-1, keepdims=True)
            o2 = _dot(pr.astype(BF16), vw)
            pick = lambda a: jnp.where(low_head, a[:DIL_TQ], a[DIL_TQ:])
            den = pick(den2)
            o = pick(o2) / den
            lse = pick(m2) + jnp.log2(den)
            if dil != DILATIONS[0]:
                o_prev, lse_prev = o_ref[0, rows_q, :], lse_scr[rows_q, :]
                top = jnp.maximum(lse_prev, lse)
                w_prev, w_new = jnp.exp2(lse_prev - top), jnp.exp2(lse - top)
                tot = w_prev + w_new
                o = (w_prev * o_prev + w_new * o) / tot
                lse = top + jnp.log2(tot)
            o_ref[0, rows_q, :] = o
            if dil != DILATIONS[-1]:
                lse_scr[rows_q, :] = lse

        assert n_total % DIL_SLOTS == 0
        for slot in range(DIL_SLOTS - 1):
            scores_into(slot, slot)

        def step(i, carry, scores_into=scores_into, finish=finish, n_total=n_total):
            for slot in range(DIL_SLOTS):
                idx = DIL_SLOTS * i + slot
                ahead = idx + DIL_SLOTS - 1
                scores_into((slot + DIL_SLOTS - 1) % DIL_SLOTS, jnp.where(ahead >= n_total, ahead - n_total, ahead))
                finish(slot, idx)
            return carry

        lax.fori_loop(0, n_total // DIL_SLOTS, step, 0)


def _dilated(qa, ka, va):
    b, s, w = qa.shape
    spec = pl.BlockSpec((1, s, LANES), lambda bi, p: (bi, 0, p))
    out = pl.pallas_call(
        _dilated_kernel,
        grid=(b, w // LANES),
        in_specs=[spec, spec, spec],
        out_specs=spec,
        out_shape=jax.ShapeDtypeStruct((b, s, w), F32),
        scratch_shapes=[pltpu.VMEM((s, LANES), F32),
                        pltpu.VMEM((DIL_SLOTS, 2 * DIL_TQ, DIL_WIN), F32),
                        pltpu.VMEM((3, 2 * DIL_TQ, DIL_WIN), F32)],
        compiler_params=_params(2),
        name="dilated",
    )(qa, ka, va)
    return out.reshape(b * s, w)


MLA_DEN_ROWS = 16
MLA_STEPS = 16
MLA_AHEAD = 2


def _mla_kernel(q_ref, k_ref, vt_ref, o_ref, s_scr, smax_scr, *, tq, tk):
    seq = k_ref.shape[1]
    heads = range(2)
    n_blocks = seq // tk
    n_tiles = seq // tq
    steps = s_scr.shape[0]
    tiles_per_iter = steps // n_blocks
    assert steps % n_blocks == 0 and n_tiles % tiles_per_iter == 0 and MLA_AHEAD < steps

    def scores_into(slot, tile, blk, hh):
        q0 = tile * tq if isinstance(tile, int) else pl.multiple_of(tile * tq, tq)
        cols = slice(hh * LANES, (hh + 1) * LANES)
        st = _dot_nt(k_ref[0, blk * tk:(blk + 1) * tk, cols], q_ref[0, pl.ds(q0, tq), cols])
        s_scr[slot, hh] = st
        smax_scr[slot, hh] = jnp.max(st.reshape(tk // 8, 8, tq), axis=0)

    def fresh():
        return [(jnp.full((1, tq), NEG_BIG, F32), jnp.zeros((B_V_DIM + MLA_DEN_ROWS, tq), F32)) for _ in heads]

    for u in range(MLA_AHEAD):
        for hh in heads:
            scores_into(u, u // n_blocks, u % n_blocks, hh)

    def iteration(it, carry):
        tile0 = it * tiles_per_iter
        state = fresh()
        for u in range(steps):
            tile, blk = tile0 + u // n_blocks, u % n_blocks
            v = u + MLA_AHEAD
            ahead_tile = jnp.minimum(tile0 + v // n_blocks, n_tiles - 1)
            for hh in heads:
                scores_into(v % steps, ahead_tile, v % n_blocks, hh)
                m, acc = state[hh]
                m_new = jnp.maximum(m, jnp.max(smax_scr[u, hh], axis=0, keepdims=True))
                alpha = jnp.exp2(m - m_new)
                pt = jnp.exp2(s_scr[u, hh] - m_new).astype(BF16)
                vt1 = jnp.concatenate([vt_ref[hh * B_V_DIM:(hh + 1) * B_V_DIM, blk * tk:(blk + 1) * tk],
                                       jnp.ones((MLA_DEN_ROWS, tk), BF16)], axis=0)
                acc = alpha * acc + _dot(vt1, pt)
                state[hh] = (m_new, acc)
            if blk == n_blocks - 1:
                ot = jnp.concatenate([acc[:B_V_DIM] / acc[B_V_DIM:B_V_DIM + 1] for _, acc in state], axis=0)
                o_ref[0, pl.ds(pl.multiple_of(tile * tq, tq), tq), :] = ot.T
                state = fresh()
        return carry

    lax.fori_loop(0, n_tiles // tiles_per_iter, iteration, 0)


def _mla(qb, kb, vbt, tq, tk):
    b, s, _ = qb.shape
    qk_spec = pl.BlockSpec((1, s, 2 * LANES), lambda bi, p: (bi, 0, p))
    n_blocks = s // tk
    steps = n_blocks * max(1, MLA_STEPS // n_blocks)
    out = pl.pallas_call(
        functools.partial(_mla_kernel, tq=tq, tk=tk),
        grid=(b, B_HEADS // 2),
        in_specs=[qk_spec, qk_spec, pl.BlockSpec((LANES, s), lambda bi, p: (p, bi))],
        out_specs=pl.BlockSpec((1, s, LANES), lambda bi, p: (bi, 0, p)),
        out_shape=jax.ShapeDtypeStruct((b, s, B_WIDTH), F32),
        scratch_shapes=[pltpu.VMEM((steps, 2, tk, tq), F32), pltpu.VMEM((steps, 2, 8, tq), F32)],
        compiler_params=_params(2),
        name="mla",
    )(qb, kb, vbt)
    return out.reshape(b * s, B_WIDTH)


def _mem_kv_kernel(mem_ref, g_ref, wkv_ref, gk_ref, k_ref, v_ref):
    mem = mem_ref[0]
    mm = (mem * _row_rms_scale(mem, mem.shape[-1]) * g_ref[...]).astype(BF16)
    kv = _dot(mm, wkv_ref[...])
    for hd in range(M_HEADS):
        cols = slice(hd * M_HEAD_DIM, (hd + 1) * M_HEAD_DIM)
        kh = kv[:, cols]
        k_ref[0, :, cols] = (kh * _row_rms_scale(kh, M_HEAD_DIM) * gk_ref[...]).astype(k_ref.dtype)
    v_ref[0] = kv[:, M_WIDTH:].astype(v_ref.dtype)


def _mem_kv(mem, w):
    b, n_mem, d = mem.shape
    const = lambda bi: (0, 0)
    full = lambda a: pl.BlockSpec(a.shape, const)
    blk = pl.BlockSpec((1, n_mem, M_WIDTH), lambda bi: (bi, 0, 0))
    return pl.pallas_call(
        _mem_kv_kernel,
        grid=(b,),
        in_specs=[pl.BlockSpec((1, n_mem, d), lambda bi: (bi, 0, 0)),
                  full(w['gmem_kv']), full(w['wkv_m']), full(w['gk_m'])],
        out_specs=[blk, blk],
        out_shape=[jax.ShapeDtypeStruct((b, n_mem, M_WIDTH), BF16)] * 2,
        compiler_params=_params(1),
        name="mem_kv",
    )(mem, w['gmem_kv'], w['wkv_m'], w['gk_m'])


FFN_CHUNK = 1024


def _post_kernel(x_ref, oa_ref, ob_ref, ga_ref, gb_ref, wa_ref, wb_ref,
                 gmem_ref, wq_ref, gq_ref, k_ref, v_ref, wo_ref, gffn_ref, w1_ref, w2_ref, y_ref):
    oa, ob = oa_ref[0], ob_ref[0]
    na = (oa * _row_rms_scale(oa, A_WIDTH) * ga_ref[...]).astype(BF16)
    nb = (ob * _row_rms_scale(ob, B_WIDTH) * gb_ref[...]).astype(BF16)
    x = x_ref[0] + _dot(na, wa_ref[...]) + _dot(nb, wb_ref[...])

    h = (x * _row_rms_scale(x, x.shape[-1]) * gmem_ref[...]).astype(BF16)
    q = _dot(h, wq_ref[...])
    heads = []
    for hd in range(M_HEADS):
        cols = slice(hd * M_HEAD_DIM, (hd + 1) * M_HEAD_DIM)
        qh = q[:, cols]
        qh = (qh * _row_rms_scale(qh, M_HEAD_DIM) * gq_ref[...]).astype(BF16)
        s = _dot_nt(qh, k_ref[0, :, cols])
        pr = jnp.exp(s - jnp.max(s, axis=-1, keepdims=True))
        pr = pr / jnp.sum(pr, axis=-1, keepdims=True)
        heads.append(_dot(pr.astype(BF16), v_ref[0, :, cols]).astype(BF16))
    x = x + _dot(jnp.concatenate(heads, axis=-1), wo_ref[...])

    h = (x * _row_rms_scale(x, x.shape[-1]) * gffn_ref[...]).astype(BF16)
    for c in range(w1_ref.shape[1] // FFN_CHUNK):
        cols = slice(c * FFN_CHUNK, (c + 1) * FFN_CHUNK)
        a = jnp.maximum(_dot(h, w1_ref[:, cols]), 0.0)
        x = x + _dot((a * a).astype(BF16), w2_ref[cols, :])
    y_ref[0] = x


def _post(x3d, oa, ob, km, vm, w, tm):
    b, s, d = x3d.shape
    n_mem = km.shape[1]
    const = lambda bi, i: (0, 0)
    full = lambda a: pl.BlockSpec(a.shape, const, pipeline_mode=pl.Buffered(1))
    tile = lambda width: pl.BlockSpec((1, tm, width), lambda bi, i: (bi, i, 0))
    kvblk = pl.BlockSpec((1, n_mem, M_WIDTH), lambda bi, i: (bi, 0, 0))
    names = ['ga_out', 'gb_out', 'wout_a', 'wout_b', 'gmem', 'wq_m', 'gq_m']
    tail = ['wo_m', 'gffn', 'w1', 'w2']
    return pl.pallas_call(
        _post_kernel,
        grid=(b, s // tm),
        in_specs=[tile(d), tile(A_WIDTH), tile(B_WIDTH)] + [full(w[n]) for n in names] + [kvblk, kvblk]
                 + [full(w[n]) for n in tail],
        out_specs=tile(d),
        out_shape=jax.ShapeDtypeStruct((b, s, d), F32),
        compiler_params=_params(2),
        name="post",
    )(x3d, oa, ob, *[w[n] for n in names], km, vm, *[w[n] for n in tail])


def _rope_tables(seq, rope_dim):
    half = rope_dim // 2
    assert ROPE_SPAN % half == 0
    inv = ROPE_THETA ** (-jnp.arange(0, rope_dim, 2, dtype=F32) / rope_dim)
    ang = jnp.arange(seq, dtype=F32)[:, None] * inv[None, :]
    cos, sin = jnp.cos(ang), jnp.sin(ang)
    lane = np.arange(LANES)
    is_lo = lane < ROPE_SPAN
    is_hi = (lane >= LANES // 2) & (lane < LANES // 2 + ROPE_SPAN)
    idx = (lane % (LANES // 2)) % half
    cos_t = jnp.where(is_lo | is_hi, cos[:, idx], 1.0)
    sin_t = jnp.where(is_lo, -sin[:, idx], jnp.where(is_hi, sin[:, idx], 0.0))
    return jnp.stack([cos_t, sin_t]).astype(F32)


def _layer_weights(l, p):
    row = lambda v: v.reshape(1, -1).astype(F32)
    w_in = p['w_in'][l]
    d = w_in.shape[0]
    c_kr = 3 * A_WIDTH + Q_LORA + KV_LORA
    half_r = B_ROPE_DIM // 2
    zeros = lambda n: jnp.zeros((d, n), F32)
    kr = w_in[:, c_kr:]
    win = jnp.concatenate([
        w_in[:, :A_WIDTH][:, _A_PERM_ALL], w_in[:, A_WIDTH:2 * A_WIDTH][:, _A_PERM_ALL], w_in[:, 2 * A_WIDTH:c_kr],
        kr[:, :half_r], zeros(LANES // 2 - half_r), kr[:, half_r:], zeros(LANES // 2 - half_r)], axis=1)

    def b_lanes(a, with_rope):
        if with_rope:
            a = a[..., _B_PERM]
        else:
            z = jnp.zeros(a.shape[:-1] + (half_r,), a.dtype)
            a = jnp.concatenate([z, a[..., :_B_SPLIT], z, a[..., _B_SPLIT:]], axis=-1)
        return jnp.pad(a, [(0, 0)] * (a.ndim - 1) + [(0, LANES - B_QK_DIM)])

    wukv = p['w_ukv'][l].reshape(KV_LORA, B_HEADS, B_NOPE_DIM + B_V_DIM)
    b_gain = lambda g: jnp.tile(b_lanes(g, True), B_HEADS)
    a_gain = lambda g: jnp.tile(g, A_HEADS)[_A_PERM_ALL]
    w_out = p['w_out'][l]
    return {
        'gmix': row(p['norm_mix_g'][l]),
        'win': win.astype(BF16),
        'gqa': row(a_gain(p['a_q_norm_g'][l]) * (A_HEAD_DIM ** -0.5 * LOG2_E)),
        'gka': row(a_gain(p['a_k_norm_g'][l])),
        'gcq': row(p['b_cq_norm_g'][l]),
        'gckv': row(p['b_ckv_norm_g'][l]),
        'wuq': b_lanes(p['w_uq'][l].reshape(Q_LORA, B_HEADS, B_QK_DIM), True).reshape(Q_LORA, B_QK_PAD).astype(BF16),
        'wuk': b_lanes(wukv[:, :, :B_NOPE_DIM], False).reshape(KV_LORA, B_QK_PAD).astype(BF16),
        'wuv': wukv[:, :, B_NOPE_DIM:].reshape(KV_LORA, B_WIDTH).astype(BF16),
        'gqb': row(b_gain(p['b_q_norm_g'][l]) * (B_QK_DIM ** -0.5 * LOG2_E)),
        'gkb': row(b_gain(p['b_k_norm_g'][l])),
        'ga_out': row(p['a_out_norm_g'][l]),
        'gb_out': row(p['b_out_norm_g'][l]),
        'wout_a': w_out[:A_WIDTH].astype(BF16),
        'wout_b': w_out[A_WIDTH:].astype(BF16),
        'gmem': row(p['norm_mem_g'][l]),
        'gmem_kv': row(p['mem_kv_norm_g'][l]),
        'wq_m': p['m_wq'][l].astype(BF16),
        'wkv_m': p['m_wkv'][l].astype(BF16),
        'gq_m': row(p['m_q_norm_g'][l] * (M_HEAD_DIM ** -0.5)),
        'gk_m': row(p['m_k_norm_g'][l]),
        'wo_m': p['m_wo'][l].astype(BF16),
        'gffn': row(p['norm_ffn_g'][l]),
        'w1': p['w_ff1'][l].astype(BF16),
        'w2': p['w_ff2'][l].astype(BF16),
    }


def _tiles(seq):
    tm = 512 if seq % 512 == 0 else 256
    return dict(tm=tm, tq=256, tk=512)


def _trunk(x, mem, layers):
    b, s, d = x.shape
    assert s % (max(DILATIONS) * DIL_TQ) == 0 and s // max(DILATIONS) >= DIL_WIN
    t = _tiles(s)
    taba = _rope_tables(s, A_ROPE_DIM)
    tabb = _rope_tables(s, B_ROPE_DIM)
    head_ones = jnp.asarray(np.kron(np.eye(2), np.ones((LANES, LANES))), BF16)
    same_head = (_A_PERM[:, None] < A_HEAD_DIM) == (_A_PERM[None, :] < A_HEAD_DIM)
    pair_ones = jnp.asarray(np.kron(np.eye(2), same_head), BF16)
    x2d = x.reshape(b * s, d)
    for w in layers:
        w = dict(w, taba=taba, tabb=tabb, head_ones=head_ones, pair_ones=pair_ones)
        qa, ka, va, qb, kb, vbt = _proj(x2d, s, w, t['tm'])
        as3d = lambda a: a.reshape(b, s, a.shape[-1])
        oa = _dilated(as3d(qa), as3d(ka), as3d(va))
        ob = _mla(as3d(qb), as3d(kb), vbt, t['tq'], t['tk'])
        km, vm = _mem_kv(mem, w)
        x2d = _post(as3d(x2d), as3d(oa), as3d(ob), km, vm, w, t['tm']).reshape(b * s, d)
    return x2d.reshape(b, s, d)


def kernel(x_prompt, x_sample, mem_prompt, mem_sample, norm_mix_g, w_in, a_q_norm_g, a_k_norm_g, b_cq_norm_g, b_ckv_norm_g, w_uq, w_ukv, b_q_norm_g, b_k_norm_g, a_out_norm_g, b_out_norm_g, w_out, norm_mem_g, mem_kv_norm_g, m_wq, m_wkv, m_q_norm_g, m_k_norm_g, m_wo, norm_ffn_g, w_ff1, w_ff2):
    p = dict(norm_mix_g=norm_mix_g, w_in=w_in, a_q_norm_g=a_q_norm_g, a_k_norm_g=a_k_norm_g,
             b_cq_norm_g=b_cq_norm_g, b_ckv_norm_g=b_ckv_norm_g, w_uq=w_uq, w_ukv=w_ukv,
             b_q_norm_g=b_q_norm_g, b_k_norm_g=b_k_norm_g, a_out_norm_g=a_out_norm_g,
             b_out_norm_g=b_out_norm_g, w_out=w_out, norm_mem_g=norm_mem_g, mem_kv_norm_g=mem_kv_norm_g,
             m_wq=m_wq, m_wkv=m_wkv, m_q_norm_g=m_q_norm_g, m_k_norm_g=m_k_norm_g, m_wo=m_wo,
             norm_ffn_g=norm_ffn_g, w_ff1=w_ff1, w_ff2=w_ff2)
    layers = [_layer_weights(l, p) for l in range(w_in.shape[0])]
    return (_trunk(x_prompt, mem_prompt, layers), _trunk(x_sample, mem_sample, layers))
```

```python
import functools

import numpy as np
import jax
import jax.numpy as jnp
from jax import lax
from jax.experimental import pallas as pl
from jax.experimental.pallas import tpu as pltpu

EPS = 1e-6
ROPE_THETA = 500000.0

LANES = 128
A_HEADS = 8
A_HEAD_DIM = 64
A_WIDTH = A_HEADS * A_HEAD_DIM
A_ROPE_DIM = 16
A_HALF_WINDOW = 64
DILATIONS = (16, 4, 1)

B_HEADS = 8
B_NOPE_DIM = 64
B_ROPE_DIM = 32
B_QK_DIM = B_NOPE_DIM + B_ROPE_DIM
B_V_DIM = 64
B_WIDTH = B_HEADS * B_V_DIM
B_QK_PAD = B_HEADS * LANES
Q_LORA = 256
KV_LORA = 128

M_HEADS = 4
M_HEAD_DIM = 128
M_WIDTH = M_HEADS * M_HEAD_DIM

VMEM_LIMIT = 56 * 1024 * 1024

BF16 = jnp.bfloat16
F32 = jnp.float32
NEG_BIG = -1e30
LOG2_E = float(np.log2(np.e))


def _a_pair_perm():
    heads = [np.arange(A_HEAD_DIM), A_HEAD_DIM + np.arange(A_HEAD_DIM)]
    r = A_ROPE_DIM // 2
    rest = (A_HEAD_DIM - A_ROPE_DIM) // 2
    first = [h[:r] for h in heads] + [h[A_ROPE_DIM:A_ROPE_DIM + rest] for h in heads]
    second = [h[r:A_ROPE_DIM] for h in heads] + [h[A_ROPE_DIM + rest:] for h in heads]
    return np.concatenate(first + second)


ROPE_SPAN = 16
_A_PERM = _a_pair_perm()
_A_PERM_ALL = np.concatenate([c * LANES + _A_PERM for c in range(A_WIDTH // LANES)])
_B_SPLIT = LANES // 2 - B_ROPE_DIM // 2
_B_PERM = np.concatenate([np.arange(B_NOPE_DIM, B_NOPE_DIM + B_ROPE_DIM // 2), np.arange(0, _B_SPLIT),
                          np.arange(B_NOPE_DIM + B_ROPE_DIM // 2, B_QK_DIM), np.arange(_B_SPLIT, B_NOPE_DIM)])


def _lane_mask(member):
    lane = lax.broadcasted_iota(jnp.int32, (1, LANES), 1)
    edges = np.flatnonzero(np.diff(np.concatenate([[0], member.astype(np.int8), [0]])))
    mask = None
    for lo, hi in zip(edges[::2], edges[1::2]):
        run = (lane >= int(lo)) & (lane < int(hi))
        mask = run if mask is None else (mask | run)
    return mask


def _params(n_axes):
    return pltpu.CompilerParams(dimension_semantics=("arbitrary",) * n_axes,
                                vmem_limit_bytes=VMEM_LIMIT)


def _row_rms_scale(x, width):
    return lax.rsqrt(jnp.sum(x * x, axis=-1, keepdims=True) * (1.0 / width) + EPS)


def _dot(a, b):
    return jnp.dot(a, b, preferred_element_type=F32)


def _dot_nt(a, b):
    return lax.dot_general(a, b, (((1,), (1,)), ((), ())), preferred_element_type=F32)


def _rope_chunk(xc, cos, sin):
    return xc * cos + pltpu.roll(xc, LANES // 2, axis=1) * sin


def _proj_kernel(x_ref, gmix_ref, win_ref, gqa_ref, gka_ref, gcq_ref, gckv_ref,
                 wuq_ref, wuk_ref, wuv_ref, gqb_ref, gkb_ref, aones_ref, ones_ref, taba_ref, tabb_ref,
                 qa_ref, ka_ref, va_ref, qb_ref, kb_ref, vbt_ref):
    x = x_ref[...]
    h = (x * _row_rms_scale(x, x.shape[-1]) * gmix_ref[...]).astype(BF16)
    proj = _dot(h, win_ref[...])

    cos_a, sin_a = taba_ref[0], taba_ref[1]
    cos_b, sin_b = tabb_ref[0], tabb_ref[1]

    def a_heads(col0, g_ref, out_ref):
        for c2 in range(A_WIDTH // (2 * LANES)):
            xs = proj[:, col0 + 2 * c2 * LANES: col0 + 2 * (c2 + 1) * LANES]
            ssq = _dot((xs * xs).astype(BF16), aones_ref[...])
            for i in range(2):
                c = 2 * c2 + i
                lanes = slice(i * LANES, (i + 1) * LANES)
                rs = lax.rsqrt(ssq[:, lanes] * (1.0 / A_HEAD_DIM) + EPS)
                y = xs[:, lanes] * rs * g_ref[:, c * LANES:(c + 1) * LANES]
                y = _rope_chunk(y, cos_a, sin_a)
                out_ref[:, c * LANES:(c + 1) * LANES] = y.astype(out_ref.dtype)

    a_heads(0, gqa_ref, qa_ref)
    a_heads(A_WIDTH, gka_ref, ka_ref)
    va_ref[...] = proj[:, 2 * A_WIDTH:3 * A_WIDTH].astype(va_ref.dtype)

    c0 = 3 * A_WIDTH
    cq = proj[:, c0:c0 + Q_LORA]
    ckv = proj[:, c0 + Q_LORA:c0 + Q_LORA + KV_LORA]
    kr = proj[:, c0 + Q_LORA + KV_LORA:c0 + Q_LORA + 2 * KV_LORA]
    cqn = (cq * _row_rms_scale(cq, Q_LORA) * gcq_ref[...]).astype(BF16)
    ckvn = (ckv * _row_rms_scale(ckv, KV_LORA) * gckv_ref[...]).astype(BF16)
    qb = _dot(cqn, wuq_ref[...])
    kn = _dot(ckvn, wuk_ref[...])
    vbt_ref[...] = _dot(ckvn, wuv_ref[...]).T.astype(vbt_ref.dtype)

    def b_heads(get_chunk, g_ref, out_ref):
        for c2 in range(B_HEADS // 2):
            xs = [get_chunk(2 * c2 + i) for i in range(2)]
            ssq = _dot(jnp.concatenate([xc * xc for xc in xs], axis=1).astype(BF16), ones_ref[...])
            for i, xc in enumerate(xs):
                c = 2 * c2 + i
                rs = lax.rsqrt(ssq[:, i * LANES:(i + 1) * LANES] * (1.0 / B_QK_DIM) + EPS)
                y = xc * rs * g_ref[:, c * LANES:(c + 1) * LANES]
                y = _rope_chunk(y, cos_b, sin_b)
                out_ref[:, c * LANES:(c + 1) * LANES] = y.astype(out_ref.dtype)

    b_heads(lambda c: qb[:, c * LANES:(c + 1) * LANES], gqb_ref, qb_ref)
    b_heads(lambda c: kn[:, c * LANES:(c + 1) * LANES] + kr, gkb_ref, kb_ref)


def _proj(x2d, seq, w, tm):
    n, d = x2d.shape
    nblk_seq = seq // tm
    row = lambda i: (i, 0)
    const = lambda i: (0, 0)
    tab = lambda i: (0, i % nblk_seq, 0)
    full = lambda a: pl.BlockSpec(a.shape, const)
    widths = [A_WIDTH, A_WIDTH, A_WIDTH, B_QK_PAD, B_QK_PAD]
    return pl.pallas_call(
        _proj_kernel,
        grid=(n // tm,),
        in_specs=[pl.BlockSpec((tm, d), row), full(w['gmix']), full(w['win']), full(w['gqa']), full(w['gka']),
                  full(w['gcq']), full(w['gckv']), full(w['wuq']), full(w['wuk']), full(w['wuv']),
                  full(w['gqb']), full(w['gkb']), full(w['pair_ones']), full(w['head_ones']),
                  pl.BlockSpec((2, tm, LANES), tab), pl.BlockSpec((2, tm, LANES), tab)],
        out_specs=[pl.BlockSpec((tm, wd), row) for wd in widths]
                  + [pl.BlockSpec((B_WIDTH, tm), lambda i: (0, i))],
        out_shape=[jax.ShapeDtypeStruct((n, wd), F32 if wd == A_WIDTH else BF16) for wd in widths]
                  + [jax.ShapeDtypeStruct((B_WIDTH, n), BF16)],
        compiler_params=_params(1),
        name="proj",
    )(x2d, w['gmix'], w['win'], w['gqa'], w['gka'], w['gcq'], w['gckv'], w['wuq'], w['wuk'], w['wuv'],
      w['gqb'], w['gkb'], w['pair_ones'], w['head_ones'], w['taba'], w['tabb'])


DIL_TQ = 128
DIL_WIN = DIL_TQ + 2 * A_HALF_WINDOW
DIL_SLOTS = 4


def _strided_rows(start, size, stride):
    return pl.ds(start, size) if stride == 1 else pl.ds(start, size, stride=stride)


def _dilated_kernel(q_ref, k_ref, v_ref, o_ref, lse_scr, s_scr, bias_scr):
    seq = q_ref.shape[1]
    lane = lax.broadcasted_iota(jnp.int32, (1, LANES), 1)
    low_head = lane < A_HEAD_DIM
    q_first = _lane_mask(_A_PERM < A_HEAD_DIM)
    rel = (lax.broadcasted_iota(jnp.int32, (2 * DIL_TQ, DIL_WIN), 1)
           - lax.broadcasted_iota(jnp.int32, (2 * DIL_TQ, DIL_WIN), 0) % DIL_TQ)
    for place in range(3):
        bias_scr[place] = jnp.where(jnp.abs(rel - place * A_HALF_WINDOW) <= A_HALF_WINDOW, 0.0, NEG_BIG)

    for dil in DILATIONS:
        sub_len = seq // dil
        n_blk = sub_len // DIL_TQ
        n_total = dil * n_blk

        def coords(idx, dil=dil, sub_len=sub_len, n_blk=n_blk):
            r = idx // n_blk
            q0 = (idx % n_blk) * DIL_TQ
            k0 = jnp.clip(q0 - A_HALF_WINDOW, 0, sub_len - DIL_WIN)
            rows_q = _strided_rows(r + dil * q0, DIL_TQ, dil)
            rows_k = _strided_rows(r + dil * k0, DIL_WIN, dil)
            return rows_q, rows_k, (q0 - k0) // A_HALF_WINDOW

        def scores_into(slot, idx, coords=coords):
            rows_q, rows_k, place = coords(idx)
            qp = q_ref[0, rows_q, :]
            kw = k_ref[0, rows_k, :].astype(BF16)
            q2 = jnp.concatenate([jnp.where(q_first, qp, 0.0), jnp.where(q_first, 0.0, qp)], axis=0).astype(BF16)
            s_scr[slot] = _dot_nt(q2, kw) + bias_scr[place]

        def finish(slot, idx, dil=dil, coords=coords):
            rows_q, rows_k, _ = coords(idx)
            vw = v_ref[0, rows_k, :].astype(BF16)
            m2 = jnp.max(s_scr[slot], axis=-1, keepdims=True)
            pr = jnp.exp2(s_scr[slot] - m2)
            den2 = jnp.sum(pr, axis=-1, keepdims=True)
            o2 = _dot(pr.astype(BF16), vw)
            pick = lambda a: jnp.where(low_head, a[:DIL_TQ], a[DIL_TQ:])
            den = pick(den2)
            o = pick(o2) / den
            lse = pick(m2) + jnp.log2(den)
            if dil != DILATIONS[0]:
                o_prev, lse_prev = o_ref[0, rows_q, :], lse_scr[rows_q, :]
                top = jnp.maximum(lse_prev, lse)
                w_prev, w_new = jnp.exp2(lse_prev - top), jnp.exp2(lse - top)
                tot = w_prev + w_new
                o = (w_prev * o_prev + w_new * o) / tot
                lse = top + jnp.log2(tot)
            o_ref[0, rows_q, :] = o
            if dil != DILATIONS[-1]:
                lse_scr[rows_q, :] = lse

        assert n_total % DIL_SLOTS == 0
        for slot in range(DIL_SLOTS - 1):
            scores_into(slot, slot)

        def step(i, carry, scores_into=scores_into, finish=finish, n_total=n_total):
            for slot in range(DIL_SLOTS):
                idx = DIL_SLOTS * i + slot
                ahead = idx + DIL_SLOTS - 1
                scores_into((slot + DIL_SLOTS - 1) % DIL_SLOTS, jnp.where(ahead >= n_total, ahead - n_total, ahead))
                finish(slot, idx)
            return carry

        lax.fori_loop(0, n_total // DIL_SLOTS, step, 0)


def _dilated(qa, ka, va):
    b, s, w = qa.shape
    spec = pl.BlockSpec((1, s, LANES), lambda bi, p: (bi, 0, p))
    out = pl.pallas_call(
        _dilated_kernel,
        grid=(b, w // LANES),
        in_specs=[spec, spec, spec],
        out_specs=spec,
        out_shape=jax.ShapeDtypeStruct((b, s, w), F32),
        scratch_shapes=[pltpu.VMEM((s, LANES), F32),
                        pltpu.VMEM((DIL_SLOTS, 2 * DIL_TQ, DIL_WIN), F32),
                        pltpu.VMEM((3, 2 * DIL_TQ, DIL_WIN), F32)],
        compiler_params=_params(2),
        name="dilated",
    )(qa, ka, va)
    return out.reshape(b * s, w)


MLA_DEN_ROWS = 16
MLA_STEPS = 32
MLA_SLOTS = 16
MLA_AHEAD = 2


def _mla_kernel(q_ref, k_ref, vt_ref, o_ref, s_scr, smax_scr, *, tq, tk, steps):
    seq = k_ref.shape[1]
    heads = range(2)
    n_blocks = seq // tk
    n_tiles = seq // tq
    slots = s_scr.shape[0]
    tiles_per_iter = steps // n_blocks
    assert steps % n_blocks == 0 and n_tiles % tiles_per_iter == 0 and steps % slots == 0 and MLA_AHEAD < slots

    def scores_into(slot, tile, blk, hh):
        q0 = tile * tq if isinstance(tile, int) else pl.multiple_of(tile * tq, tq)
        cols = slice(hh * LANES, (hh + 1) * LANES)
        st = _dot_nt(k_ref[0, blk * tk:(blk + 1) * tk, cols], q_ref[0, pl.ds(q0, tq), cols])
        s_scr[slot, hh] = st
        smax_scr[slot, hh] = jnp.max(st.reshape(tk // 8, 8, tq), axis=0)

    def fresh():
        return [(jnp.full((1, tq), NEG_BIG, F32), jnp.zeros((B_V_DIM + MLA_DEN_ROWS, tq), F32)) for _ in heads]

    for u in range(MLA_AHEAD):
        for hh in heads:
            scores_into(u, u // n_blocks, u % n_blocks, hh)

    def iteration(it, carry):
        tile0 = it * tiles_per_iter
        state = fresh()
        for u in range(steps):
            tile, blk = tile0 + u // n_blocks, u % n_blocks
            v = u + MLA_AHEAD
            ahead_tile = jnp.minimum(tile0 + v // n_blocks, n_tiles - 1)
            for hh in heads:
                scores_into(v % slots, ahead_tile, v % n_blocks, hh)
                m, acc = state[hh]
                m_new = jnp.maximum(m, jnp.max(smax_scr[u % slots, hh], axis=0, keepdims=True))
                alpha = jnp.exp2(m - m_new)
                pt = jnp.exp2(s_scr[u % slots, hh] - m_new).astype(BF16)
                vt1 = jnp.concatenate([vt_ref[hh * B_V_DIM:(hh + 1) * B_V_DIM, blk * tk:(blk + 1) * tk],
                                       jnp.ones((MLA_DEN_ROWS, tk), BF16)], axis=0)
                acc = alpha * acc + _dot(vt1, pt)
                state[hh] = (m_new, acc)
            if blk == n_blocks - 1:
                ot = jnp.concatenate([acc[:B_V_DIM] / acc[B_V_DIM:B_V_DIM + 1] for _, acc in state], axis=0)
                o_ref[0, pl.ds(pl.multiple_of(tile * tq, tq), tq), :] = ot.T
                state = fresh()
        return carry

    lax.fori_loop(0, n_tiles // tiles_per_iter, iteration, 0)


def _mla(qb, kb, vbt, tq, tk):
    b, s, _ = qb.shape
    qk_spec = pl.BlockSpec((1, s, 2 * LANES), lambda bi, p: (bi, 0, p))
    n_blocks = s // tk
    steps = n_blocks * max(1, MLA_STEPS // n_blocks)
    slots = min(steps, MLA_SLOTS)
    out = pl.pallas_call(
        functools.partial(_mla_kernel, tq=tq, tk=tk, steps=steps),
        grid=(b, B_HEADS // 2),
        in_specs=[qk_spec, qk_spec, pl.BlockSpec((LANES, s), lambda bi, p: (p, bi))],
        out_specs=pl.BlockSpec((1, s, LANES), lambda bi, p: (bi, 0, p)),
        out_shape=jax.ShapeDtypeStruct((b, s, B_WIDTH), F32),
        scratch_shapes=[pltpu.VMEM((slots, 2, tk, tq), F32), pltpu.VMEM((slots, 2, 8, tq), F32)],
        compiler_params=_params(2),
        name="mla",
    )(qb, kb, vbt)
    return out.reshape(b * s, B_WIDTH)


def _mem_kv_kernel(mem_ref, g_ref, wkv_ref, gk_ref, k_ref, v_ref):
    mem = mem_ref[0]
    mm = (mem * _row_rms_scale(mem, mem.shape[-1]) * g_ref[...]).astype(BF16)
    kv = _dot(mm, wkv_ref[...])
    for hd in range(M_HEADS):
        cols = slice(hd * M_HEAD_DIM, (hd + 1) * M_HEAD_DIM)
        kh = kv[:, cols]
        k_ref[0, :, cols] = (kh * _row_rms_scale(kh, M_HEAD_DIM) * gk_ref[...]).astype(k_ref.dtype)
    v_ref[0] = kv[:, M_WIDTH:].astype(v_ref.dtype)


def _mem_kv(mem, w):
    b, n_mem, d = mem.shape
    const = lambda bi: (0, 0)
    full = lambda a: pl.BlockSpec(a.shape, const)
    blk = pl.BlockSpec((1, n_mem, M_WIDTH), lambda bi: (bi, 0, 0))
    return pl.pallas_call(
        _mem_kv_kernel,
        grid=(b,),
        in_specs=[pl.BlockSpec((1, n_mem, d), lambda bi: (bi, 0, 0)),
                  full(w['gmem_kv']), full(w['wkv_m']), full(w['gk_m'])],
        out_specs=[blk, blk],
        out_shape=[jax.ShapeDtypeStruct((b, n_mem, M_WIDTH), BF16)] * 2,
        compiler_params=_params(1),
        name="mem_kv",
    )(mem, w['gmem_kv'], w['wkv_m'], w['gk_m'])


FFN_CHUNK = 1024


def _post_kernel(x_ref, oa_ref, ob_ref, ga_ref, gb_ref, wa_ref, wb_ref,
                 gmem_ref, wq_ref, gq_ref, k_ref, v_ref, wo_ref, gffn_ref, w1_ref, w2_ref, y_ref):
    oa, ob = oa_ref[0], ob_ref[0]
    na = (oa * _row_rms_scale(oa, A_WIDTH) * ga_ref[...]).astype(BF16)
    nb = (ob * _row_rms_scale(ob, B_WIDTH) * gb_ref[...]).astype(BF16)
    x = x_ref[0] + _dot(na, wa_ref[...]) + _dot(nb, wb_ref[...])

    h = (x * _row_rms_scale(x, x.shape[-1]) * gmem_ref[...]).astype(BF16)
    q = _dot(h, wq_ref[...])
    heads = []
    for hd in range(M_HEADS):
        cols = slice(hd * M_HEAD_DIM, (hd + 1) * M_HEAD_DIM)
        qh = q[:, cols]
        qh = (qh * _row_rms_scale(qh, M_HEAD_DIM) * gq_ref[...]).astype(BF16)
        s = _dot_nt(qh, k_ref[0, :, cols])
        pr = jnp.exp(s - jnp.max(s, axis=-1, keepdims=True))
        pr = pr / jnp.sum(pr, axis=-1, keepdims=True)
        heads.append(_dot(pr.astype(BF16), v_ref[0, :, cols]).astype(BF16))
    x = x + _dot(jnp.concatenate(heads, axis=-1), wo_ref[...])

    h = (x * _row_rms_scale(x, x.shape[-1]) * gffn_ref[...]).astype(BF16)
    for c in range(w1_ref.shape[1] // FFN_CHUNK):
        cols = slice(c * FFN_CHUNK, (c + 1) * FFN_CHUNK)
        a = jnp.maximum(_dot(h, w1_ref[:, cols]), 0.0)
        x = x + _dot((a * a).astype(BF16), w2_ref[cols, :])
    y_ref[0] = x


def _post(x3d, oa, ob, km, vm, w, tm):
    b, s, d = x3d.shape
    n_mem = km.shape[1]
    const = lambda bi, i: (0, 0)
    full = lambda a: pl.BlockSpec(a.shape, const, pipeline_mode=pl.Buffered(1))
    tile = lambda width: pl.BlockSpec((1, tm, width), lambda bi, i: (bi, i, 0))
    kvblk = pl.BlockSpec((1, n_mem, M_WIDTH), lambda bi, i: (bi, 0, 0))
    names = ['ga_out', 'gb_out', 'wout_a', 'wout_b', 'gmem', 'wq_m', 'gq_m']
    tail = ['wo_m', 'gffn', 'w1', 'w2']
    return pl.pallas_call(
        _post_kernel,
        grid=(b, s // tm),
        in_specs=[tile(d), tile(A_WIDTH), tile(B_WIDTH)] + [full(w[n]) for n in names] + [kvblk, kvblk]
                 + [full(w[n]) for n in tail],
        out_specs=tile(d),
        out_shape=jax.ShapeDtypeStruct((b, s, d), F32),
        compiler_params=_params(2),
        name="post",
    )(x3d, oa, ob, *[w[n] for n in names], km, vm, *[w[n] for n in tail])


def _rope_tables(seq, rope_dim):
    half = rope_dim // 2
    assert ROPE_SPAN % half == 0
    inv = ROPE_THETA ** (-jnp.arange(0, rope_dim, 2, dtype=F32) / rope_dim)
    ang = jnp.arange(seq, dtype=F32)[:, None] * inv[None, :]
    cos, sin = jnp.cos(ang), jnp.sin(ang)
    lane = np.arange(LANES)
    is_lo = lane < ROPE_SPAN
    is_hi = (lane >= LANES // 2) & (lane < LANES // 2 + ROPE_SPAN)
    idx = (lane % (LANES // 2)) % half
    cos_t = jnp.where(is_lo | is_hi, cos[:, idx], 1.0)
    sin_t = jnp.where(is_lo, -sin[:, idx], jnp.where(is_hi, sin[:, idx], 0.0))
    return jnp.stack([cos_t, sin_t]).astype(F32)


def _layer_weights(l, p):
    row = lambda v: v.reshape(1, -1).astype(F32)
    w_in = p['w_in'][l]
    d = w_in.shape[0]
    c_kr = 3 * A_WIDTH + Q_LORA + KV_LORA
    half_r = B_ROPE_DIM // 2
    zeros = lambda n: jnp.zeros((d, n), F32)
    kr = w_in[:, c_kr:]
    win = jnp.concatenate([
        w_in[:, :A_WIDTH][:, _A_PERM_ALL], w_in[:, A_WIDTH:2 * A_WIDTH][:, _A_PERM_ALL], w_in[:, 2 * A_WIDTH:c_kr],
        kr[:, :half_r], zeros(LANES // 2 - half_r), kr[:, half_r:], zeros(LANES // 2 - half_r)], axis=1)

    def b_lanes(a, with_rope):
        if with_rope:
            a = a[..., _B_PERM]
        else:
            z = jnp.zeros(a.shape[:-1] + (half_r,), a.dtype)
            a = jnp.concatenate([z, a[..., :_B_SPLIT], z, a[..., _B_SPLIT:]], axis=-1)
        return jnp.pad(a, [(0, 0)] * (a.ndim - 1) + [(0, LANES - B_QK_DIM)])

    wukv = p['w_ukv'][l].reshape(KV_LORA, B_HEADS, B_NOPE_DIM + B_V_DIM)
    b_gain = lambda g: jnp.tile(b_lanes(g, True), B_HEADS)
    a_gain = lambda g: jnp.tile(g, A_HEADS)[_A_PERM_ALL]
    w_out = p['w_out'][l]
    return {
        'gmix': row(p['norm_mix_g'][l]),
        'win': win.astype(BF16),
        'gqa': row(a_gain(p['a_q_norm_g'][l]) * (A_HEAD_DIM ** -0.5 * LOG2_E)),
        'gka': row(a_gain(p['a_k_norm_g'][l])),
        'gcq': row(p['b_cq_norm_g'][l]),
        'gckv': row(p['b_ckv_norm_g'][l]),
        'wuq': b_lanes(p['w_uq'][l].reshape(Q_LORA, B_HEADS, B_QK_DIM), True).reshape(Q_LORA, B_QK_PAD).astype(BF16),
        'wuk': b_lanes(wukv[:, :, :B_NOPE_DIM], False).reshape(KV_LORA, B_QK_PAD).astype(BF16),
        'wuv': wukv[:, :, B_NOPE_DIM:].reshape(KV_LORA, B_WIDTH).astype(BF16),
        'gqb': row(b_gain(p['b_q_norm_g'][l]) * (B_QK_DIM ** -0.5 * LOG2_E)),
        'gkb': row(b_gain(p['b_k_norm_g'][l])),
        'ga_out': row(p['a_out_norm_g'][l]),
        'gb_out': row(p['b_out_norm_g'][l]),
        'wout_a': w_out[:A_WIDTH].astype(BF16),
        'wout_b': w_out[A_WIDTH:].astype(BF16),
        'gmem': row(p['norm_mem_g'][l]),
        'gmem_kv': row(p['mem_kv_norm_g'][l]),
        'wq_m': p['m_wq'][l].astype(BF16),
        'wkv_m': p['m_wkv'][l].astype(BF16),
        'gq_m': row(p['m_q_norm_g'][l] * (M_HEAD_DIM ** -0.5)),
        'gk_m': row(p['m_k_norm_g'][l]),
        'wo_m': p['m_wo'][l].astype(BF16),
        'gffn': row(p['norm_ffn_g'][l]),
        'w1': p['w_ff1'][l].astype(BF16),
        'w2': p['w_ff2'][l].astype(BF16),
    }


def _tiles(seq):
    tm = 512 if seq % 512 == 0 else 256
    return dict(tm=tm, tq=256, tk=512)


def _trunk(x, mem, layers):
    b, s, d = x.shape
    assert s % (max(DILATIONS) * DIL_TQ) == 0 and s // max(DILATIONS) >= DIL_WIN
    t = _tiles(s)
    taba = _rope_tables(s, A_ROPE_DIM)
    tabb = _rope_tables(s, B_ROPE_DIM)
    head_ones = jnp.asarray(np.kron(np.eye(2), np.ones((LANES, LANES))), BF16)
    same_head = (_A_PERM[:, None] < A_HEAD_DIM) == (_A_PERM[None, :] < A_HEAD_DIM)
    pair_ones = jnp.asarray(np.kron(np.eye(2), same_head), BF16)
    x2d = x.reshape(b * s, d)
    for w in layers:
        w = dict(w, taba=taba, tabb=tabb, head_ones=head_ones, pair_ones=pair_ones)
        qa, ka, va, qb, kb, vbt = _proj(x2d, s, w, t['tm'])
        as3d = lambda a: a.reshape(b, s, a.shape[-1])
        oa = _dilated(as3d(qa), as3d(ka), as3d(va))
        ob = _mla(as3d(qb), as3d(kb), vbt, t['tq'], t['tk'])
        km, vm = _mem_kv(mem, w)
        x2d = _post(as3d(x2d), as3d(oa), as3d(ob), km, vm, w, t['tm']).reshape(b * s, d)
    return x2d.reshape(b, s, d)


def kernel(x_prompt, x_sample, mem_prompt, mem_sample, norm_mix_g, w_in, a_q_norm_g, a_k_norm_g, b_cq_norm_g, b_ckv_norm_g, w_uq, w_ukv, b_q_norm_g, b_k_norm_g, a_out_norm_g, b_out_norm_g, w_out, norm_mem_g, mem_kv_norm_g, m_wq, m_wkv, m_q_norm_g, m_k_norm_g, m_wo, norm_ffn_g, w_ff1, w_ff2):
    p = dict(norm_mix_g=norm_mix_g, w_in=w_in, a_q_norm_g=a_q_norm_g, a_k_norm_g=a_k_norm_g,
             b_cq_norm_g=b_cq_norm_g, b_ckv_norm_g=b_ckv_norm_g, w_uq=w_uq, w_ukv=w_ukv,
             b_q_norm_g=b_q_norm_g, b_k_norm_g=b_k_norm_g, a_out_norm_g=a_out_norm_g,
             b_out_norm_g=b_out_norm_g, w_out=w_out, norm_mem_g=norm_mem_g, mem_kv_norm_g=mem_kv_norm_g,
             m_wq=m_wq, m_wkv=m_wkv, m_q_norm_g=m_q_norm_g, m_k_norm_g=m_k_norm_g, m_wo=m_wo,
             norm_ffn_g=norm_ffn_g, w_ff1=w_ff1, w_ff2=w_ff2)
    layers = [_layer_weights(l, p) for l in range(w_in.shape[0])]
    return (_trunk(x_prompt, mem_prompt, layers), _trunk(x_sample, mem_sample, layers))
```

```python
import functools

import numpy as np
import jax
import jax.numpy as jnp
from jax import lax
from jax.experimental import pallas as pl
from jax.experimental.pallas import tpu as pltpu

EPS = 1e-6
ROPE_THETA = 500000.0

LANES = 128
A_HEADS = 8
A_HEAD_DIM = 64
A_WIDTH = A_HEADS * A_HEAD_DIM
A_ROPE_DIM = 16
A_HALF_WINDOW = 64
DILATIONS = (16, 4, 1)

B_HEADS = 8
B_NOPE_DIM = 64
B_ROPE_DIM = 32
B_QK_DIM = B_NOPE_DIM + B_ROPE_DIM
B_V_DIM = 64
B_WIDTH = B_HEADS * B_V_DIM
B_QK_PAD = B_HEADS * LANES
Q_LORA = 256
KV_LORA = 128

M_HEADS = 4
M_HEAD_DIM = 128
M_WIDTH = M_HEADS * M_HEAD_DIM

VMEM_LIMIT = 56 * 1024 * 1024

BF16 = jnp.bfloat16
F32 = jnp.float32
NEG_BIG = -1e30
LOG2_E = float(np.log2(np.e))


def _a_pair_perm():
    heads = [np.arange(A_HEAD_DIM), A_HEAD_DIM + np.arange(A_HEAD_DIM)]
    r = A_ROPE_DIM // 2
    rest = (A_HEAD_DIM - A_ROPE_DIM) // 2
    first = [h[:r] for h in heads] + [h[A_ROPE_DIM:A_ROPE_DIM + rest] for h in heads]
    second = [h[r:A_ROPE_DIM] for h in heads] + [h[A_ROPE_DIM + rest:] for h in heads]
    return np.concatenate(first + second)


ROPE_SPAN = 16
_A_PERM = _a_pair_perm()
_A_PERM_ALL = np.concatenate([c * LANES + _A_PERM for c in range(A_WIDTH // LANES)])
_B_SPLIT = LANES // 2 - B_ROPE_DIM // 2
_B_PERM = np.concatenate([np.arange(B_NOPE_DIM, B_NOPE_DIM + B_ROPE_DIM // 2), np.arange(0, _B_SPLIT),
                          np.arange(B_NOPE_DIM + B_ROPE_DIM // 2, B_QK_DIM), np.arange(_B_SPLIT, B_NOPE_DIM)])


def _lane_mask(member):
    lane = lax.broadcasted_iota(jnp.int32, (1, LANES), 1)
    edges = np.flatnonzero(np.diff(np.concatenate([[0], member.astype(np.int8), [0]])))
    mask = None
    for lo, hi in zip(edges[::2], edges[1::2]):
        run = (lane >= int(lo)) & (lane < int(hi))
        mask = run if mask is None else (mask | run)
    return mask


def _params(n_axes):
    return pltpu.CompilerParams(dimension_semantics=("arbitrary",) * n_axes,
                                vmem_limit_bytes=VMEM_LIMIT)


def _row_rms_scale(x, width):
    return lax.rsqrt(jnp.sum(x * x, axis=-1, keepdims=True) * (1.0 / width) + EPS)


def _dot(a, b):
    return jnp.dot(a, b, preferred_element_type=F32)


def _dot_nt(a, b):
    return lax.dot_general(a, b, (((1,), (1,)), ((), ())), preferred_element_type=F32)


def _rope_chunk(xc, cos, sin):
    return xc * cos + pltpu.roll(xc, LANES // 2, axis=1) * sin


def _proj_kernel(x_ref, gmix_ref, win_ref, gqa_ref, gka_ref, gcq_ref, gckv_ref,
                 wuq_ref, wuk_ref, wuv_ref, gqb_ref, gkb_ref, aones_ref, ones_ref, taba_ref, tabb_ref,
                 qa_ref, ka_ref, va_ref, qb_ref, kb_ref, vbt_ref):
    x = x_ref[...]
    h = (x * _row_rms_scale(x, x.shape[-1]) * gmix_ref[...]).astype(BF16)
    proj = _dot(h, win_ref[...])

    cos_a, sin_a = taba_ref[0], taba_ref[1]
    cos_b, sin_b = tabb_ref[0], tabb_ref[1]

    def a_heads(col0, g_ref, out_ref):
        for c2 in range(A_WIDTH // (2 * LANES)):
            xs = proj[:, col0 + 2 * c2 * LANES: col0 + 2 * (c2 + 1) * LANES]
            ssq = _dot((xs * xs).astype(BF16), aones_ref[...])
            for i in range(2):
                c = 2 * c2 + i
                lanes = slice(i * LANES, (i + 1) * LANES)
                rs = lax.rsqrt(ssq[:, lanes] * (1.0 / A_HEAD_DIM) + EPS)
                y = xs[:, lanes] * rs * g_ref[:, c * LANES:(c + 1) * LANES]
                y = _rope_chunk(y, cos_a, sin_a)
                out_ref[:, c * LANES:(c + 1) * LANES] = y.astype(out_ref.dtype)

    a_heads(0, gqa_ref, qa_ref)
    a_heads(A_WIDTH, gka_ref, ka_ref)
    va_ref[...] = proj[:, 2 * A_WIDTH:3 * A_WIDTH].astype(va_ref.dtype)

    c0 = 3 * A_WIDTH
    cq = proj[:, c0:c0 + Q_LORA]
    ckv = proj[:, c0 + Q_LORA:c0 + Q_LORA + KV_LORA]
    kr = proj[:, c0 + Q_LORA + KV_LORA:c0 + Q_LORA + 2 * KV_LORA]
    cqn = (cq * _row_rms_scale(cq, Q_LORA) * gcq_ref[...]).astype(BF16)
    ckvn = (ckv * _row_rms_scale(ckv, KV_LORA) * gckv_ref[...]).astype(BF16)
    qb = _dot(cqn, wuq_ref[...])
    kn = _dot(ckvn, wuk_ref[...])
    vbt_ref[...] = _dot(ckvn, wuv_ref[...]).T.astype(vbt_ref.dtype)

    def b_heads(get_chunk, g_ref, out_ref):
        for c2 in range(B_HEADS // 2):
            xs = [get_chunk(2 * c2 + i) for i in range(2)]
            ssq = _dot(jnp.concatenate([xc * xc for xc in xs], axis=1).astype(BF16), ones_ref[...])
            for i, xc in enumerate(xs):
                c = 2 * c2 + i
                rs = lax.rsqrt(ssq[:, i * LANES:(i + 1) * LANES] * (1.0 / B_QK_DIM) + EPS)
                y = xc * rs * g_ref[:, c * LANES:(c + 1) * LANES]
                y = _rope_chunk(y, cos_b, sin_b)
                out_ref[:, c * LANES:(c + 1) * LANES] = y.astype(out_ref.dtype)

    b_heads(lambda c: qb[:, c * LANES:(c + 1) * LANES], gqb_ref, qb_ref)
    b_heads(lambda c: kn[:, c * LANES:(c + 1) * LANES] + kr, gkb_ref, kb_ref)


def _proj(x2d, seq, w, tm):
    n, d = x2d.shape
    nblk_seq = seq // tm
    row = lambda i: (i, 0)
    const = lambda i: (0, 0)
    tab = lambda i: (0, i % nblk_seq, 0)
    full = lambda a: pl.BlockSpec(a.shape, const)
    widths = [A_WIDTH, A_WIDTH, A_WIDTH, B_QK_PAD, B_QK_PAD]
    return pl.pallas_call(
        _proj_kernel,
        grid=(n // tm,),
        in_specs=[pl.BlockSpec((tm, d), row), full(w['gmix']), full(w['win']), full(w['gqa']), full(w['gka']),
                  full(w['gcq']), full(w['gckv']), full(w['wuq']), full(w['wuk']), full(w['wuv']),
                  full(w['gqb']), full(w['gkb']), full(w['pair_ones']), full(w['head_ones']),
                  pl.BlockSpec((2, tm, LANES), tab), pl.BlockSpec((2, tm, LANES), tab)],
        out_specs=[pl.BlockSpec((tm, wd), row) for wd in widths]
                  + [pl.BlockSpec((B_WIDTH, tm), lambda i: (0, i))],
        out_shape=[jax.ShapeDtypeStruct((n, wd), F32 if wd == A_WIDTH else BF16) for wd in widths]
                  + [jax.ShapeDtypeStruct((B_WIDTH, n), BF16)],
        compiler_params=_params(1),
        name="proj",
    )(x2d, w['gmix'], w['win'], w['gqa'], w['gka'], w['gcq'], w['gckv'], w['wuq'], w['wuk'], w['wuv'],
      w['gqb'], w['gkb'], w['pair_ones'], w['head_ones'], w['taba'], w['tabb'])


DIL_TQ = 128
DIL_WIN = DIL_TQ + 2 * A_HALF_WINDOW
DIL_SLOTS = 4
DIL_STEPS = 8


def _strided_rows(start, size, stride):
    return pl.ds(start, size) if stride == 1 else pl.ds(start, size, stride=stride)


def _dilated_kernel(q_ref, k_ref, v_ref, o_ref, lse_scr, s_scr, bias_scr):
    seq = q_ref.shape[1]
    lane = lax.broadcasted_iota(jnp.int32, (1, LANES), 1)
    low_head = lane < A_HEAD_DIM
    q_first = _lane_mask(_A_PERM < A_HEAD_DIM)
    rel = (lax.broadcasted_iota(jnp.int32, (2 * DIL_TQ, DIL_WIN), 1)
           - lax.broadcasted_iota(jnp.int32, (2 * DIL_TQ, DIL_WIN), 0) % DIL_TQ)
    for place in range(3):
        bias_scr[place] = jnp.where(jnp.abs(rel - place * A_HALF_WINDOW) <= A_HALF_WINDOW, 0.0, NEG_BIG)

    for dil in DILATIONS:
        sub_len = seq // dil
        n_blk = sub_len // DIL_TQ
        n_total = dil * n_blk

        def coords(idx, dil=dil, sub_len=sub_len, n_blk=n_blk):
            r = idx // n_blk
            q0 = (idx % n_blk) * DIL_TQ
            k0 = jnp.clip(q0 - A_HALF_WINDOW, 0, sub_len - DIL_WIN)
            rows_q = _strided_rows(r + dil * q0, DIL_TQ, dil)
            rows_k = _strided_rows(r + dil * k0, DIL_WIN, dil)
            return rows_q, rows_k, (q0 - k0) // A_HALF_WINDOW

        def scores_into(slot, idx, coords=coords):
            rows_q, rows_k, place = coords(idx)
            qp = q_ref[0, rows_q, :]
            kw = k_ref[0, rows_k, :].astype(BF16)
            q2 = jnp.concatenate([jnp.where(q_first, qp, 0.0), jnp.where(q_first, 0.0, qp)], axis=0).astype(BF16)
            s_scr[slot] = _dot_nt(q2, kw) + bias_scr[place]

        def finish(slot, idx, dil=dil, coords=coords):
            rows_q, rows_k, _ = coords(idx)
            vw = v_ref[0, rows_k, :].astype(BF16)
            m2 = jnp.max(s_scr[slot], axis=-1, keepdims=True)
            pr = jnp.exp2(s_scr[slot] - m2)
            den2 = jnp.sum(pr, axis=-1, keepdims=True)
            o2 = _dot(pr.astype(BF16), vw)
            pick = lambda a: jnp.where(low_head, a[:DIL_TQ], a[DIL_TQ:])
            den = pick(den2)
            o = pick(o2) / den
            lse = pick(m2) + jnp.log2(den)
            if dil != DILATIONS[0]:
                o_prev, lse_prev = o_ref[0, rows_q, :], lse_scr[rows_q, :]
                top = jnp.maximum(lse_prev, lse)
                w_prev, w_new = jnp.exp2(lse_prev - top), jnp.exp2(lse - top)
                tot = w_prev + w_new
                o = (w_prev * o_prev + w_new * o) / tot
                lse = top + jnp.log2(tot)
            o_ref[0, rows_q, :] = o
            if dil != DILATIONS[-1]:
                lse_scr[rows_q, :] = lse

        assert n_total % DIL_STEPS == 0 and DIL_STEPS % DIL_SLOTS == 0
        for slot in range(DIL_SLOTS - 1):
            scores_into(slot, slot)

        def step(i, carry, scores_into=scores_into, finish=finish, n_total=n_total):
            for u in range(DIL_STEPS):
                idx = DIL_STEPS * i + u
                ahead = idx + DIL_SLOTS - 1
                scores_into((u + DIL_SLOTS - 1) % DIL_SLOTS, jnp.where(ahead >= n_total, ahead - n_total, ahead))
                finish(u % DIL_SLOTS, idx)
            return carry

        lax.fori_loop(0, n_total // DIL_STEPS, step, 0)


def _dilated(qa, ka, va):
    b, s, w = qa.shape
    spec = pl.BlockSpec((1, s, LANES), lambda bi, p: (bi, 0, p))
    out = pl.pallas_call(
        _dilated_kernel,
        grid=(b, w // LANES),
        in_specs=[spec, spec, spec],
        out_specs=spec,
        out_shape=jax.ShapeDtypeStruct((b, s, w), F32),
        scratch_shapes=[pltpu.VMEM((s, LANES), F32),
                        pltpu.VMEM((DIL_SLOTS, 2 * DIL_TQ, DIL_WIN), F32),
                        pltpu.VMEM((3, 2 * DIL_TQ, DIL_WIN), F32)],
        compiler_params=_params(2),
        name="dilated",
    )(qa, ka, va)
    return out.reshape(b * s, w)


MLA_DEN_ROWS = 16
MLA_STEPS = 32
MLA_SLOTS = 16
MLA_AHEAD = 2


def _mla_kernel(q_ref, k_ref, vt_ref, o_ref, s_scr, smax_scr, *, tq, tk, steps):
    seq = k_ref.shape[1]
    heads = range(2)
    n_blocks = seq // tk
    n_tiles = seq // tq
    slots = s_scr.shape[0]
    tiles_per_iter = steps // n_blocks
    assert steps % n_blocks == 0 and n_tiles % tiles_per_iter == 0 and steps % slots == 0 and MLA_AHEAD < slots

    def scores_into(slot, tile, blk, hh):
        q0 = tile * tq if isinstance(tile, int) else pl.multiple_of(tile * tq, tq)
        cols = slice(hh * LANES, (hh + 1) * LANES)
        st = _dot_nt(k_ref[0, blk * tk:(blk + 1) * tk, cols], q_ref[0, pl.ds(q0, tq), cols])
        s_scr[slot, hh] = st
        smax_scr[slot, hh] = jnp.max(st.reshape(tk // 8, 8, tq), axis=0)

    def fresh():
        return [(jnp.full((1, tq), NEG_BIG, F32), jnp.zeros((B_V_DIM + MLA_DEN_ROWS, tq), F32)) for _ in heads]

    for u in range(MLA_AHEAD):
        for hh in heads:
            scores_into(u, u // n_blocks, u % n_blocks, hh)

    def iteration(it, carry):
        tile0 = it * tiles_per_iter
        state = fresh()
        for u in range(steps):
            tile, blk = tile0 + u // n_blocks, u % n_blocks
            v = u + MLA_AHEAD
            ahead_tile = jnp.minimum(tile0 + v // n_blocks, n_tiles - 1)
            for hh in heads:
                scores_into(v % slots, ahead_tile, v % n_blocks, hh)
                m, acc = state[hh]
                m_new = jnp.maximum(m, jnp.max(smax_scr[u % slots, hh], axis=0, keepdims=True))
                alpha = jnp.exp2(m - m_new)
                pt = jnp.exp2(s_scr[u % slots, hh] - m_new).astype(BF16)
                vt1 = jnp.concatenate([vt_ref[hh * B_V_DIM:(hh + 1) * B_V_DIM, blk * tk:(blk + 1) * tk],
                                       jnp.ones((MLA_DEN_ROWS, tk), BF16)], axis=0)
                acc = alpha * acc + _dot(vt1, pt)
                state[hh] = (m_new, acc)
            if blk == n_blocks - 1:
                ot = jnp.concatenate([acc[:B_V_DIM] / acc[B_V_DIM:B_V_DIM + 1] for _, acc in state], axis=0)
                o_ref[0, pl.ds(pl.multiple_of(tile * tq, tq), tq), :] = ot.T
                state = fresh()
        return carry

    lax.fori_loop(0, n_tiles // tiles_per_iter, iteration, 0)


def _mla(qb, kb, vbt, tq, tk):
    b, s, _ = qb.shape
    qk_spec = pl.BlockSpec((1, s, 2 * LANES), lambda bi, p: (bi, 0, p))
    n_blocks = s // tk
    steps = n_blocks * max(1, MLA_STEPS // n_blocks)
    slots = min(steps, MLA_SLOTS)
    out = pl.pallas_call(
        functools.partial(_mla_kernel, tq=tq, tk=tk, steps=steps),
        grid=(b, B_HEADS // 2),
        in_specs=[qk_spec, qk_spec, pl.BlockSpec((LANES, s), lambda bi, p: (p, bi))],
        out_specs=pl.BlockSpec((1, s, LANES), lambda bi, p: (bi, 0, p)),
        out_shape=jax.ShapeDtypeStruct((b, s, B_WIDTH), F32),
        scratch_shapes=[pltpu.VMEM((slots, 2, tk, tq), F32), pltpu.VMEM((slots, 2, 8, tq), F32)],
        compiler_params=_params(2),
        name="mla",
    )(qb, kb, vbt)
    return out.reshape(b * s, B_WIDTH)


def _mem_kv_kernel(mem_ref, g_ref, wkv_ref, gk_ref, k_ref, v_ref):
    mem = mem_ref[0]
    mm = (mem * _row_rms_scale(mem, mem.shape[-1]) * g_ref[...]).astype(BF16)
    kv = _dot(mm, wkv_ref[...])
    for hd in range(M_HEADS):
        cols = slice(hd * M_HEAD_DIM, (hd + 1) * M_HEAD_DIM)
        kh = kv[:, cols]
        k_ref[0, :, cols] = (kh * _row_rms_scale(kh, M_HEAD_DIM) * gk_ref[...]).astype(k_ref.dtype)
    v_ref[0] = kv[:, M_WIDTH:].astype(v_ref.dtype)


def _mem_kv(mem, w):
    b, n_mem, d = mem.shape
    const = lambda bi: (0, 0)
    full = lambda a: pl.BlockSpec(a.shape, const)
    blk = pl.BlockSpec((1, n_mem, M_WIDTH), lambda bi: (bi, 0, 0))
    return pl.pallas_call(
        _mem_kv_kernel,
        grid=(b,),
        in_specs=[pl.BlockSpec((1, n_mem, d), lambda bi: (bi, 0, 0)),
                  full(w['gmem_kv']), full(w['wkv_m']), full(w['gk_m'])],
        out_specs=[blk, blk],
        out_shape=[jax.ShapeDtypeStruct((b, n_mem, M_WIDTH), BF16)] * 2,
        compiler_params=_params(1),
        name="mem_kv",
    )(mem, w['gmem_kv'], w['wkv_m'], w['gk_m'])


FFN_CHUNK = 1024


def _post_kernel(x_ref, oa_ref, ob_ref, ga_ref, gb_ref, wa_ref, wb_ref,
                 gmem_ref, wq_ref, gq_ref, k_ref, v_ref, wo_ref, gffn_ref, w1_ref, w2_ref, y_ref):
    oa, ob = oa_ref[0], ob_ref[0]
    na = (oa * _row_rms_scale(oa, A_WIDTH) * ga_ref[...]).astype(BF16)
    nb = (ob * _row_rms_scale(ob, B_WIDTH) * gb_ref[...]).astype(BF16)
    x = x_ref[0] + _dot(na, wa_ref[...]) + _dot(nb, wb_ref[...])

    h = (x * _row_rms_scale(x, x.shape[-1]) * gmem_ref[...]).astype(BF16)
    q = _dot(h, wq_ref[...])
    heads = []
    for hd in range(M_HEADS):
        cols = slice(hd * M_HEAD_DIM, (hd + 1) * M_HEAD_DIM)
        qh = q[:, cols]
        qh = (qh * _row_rms_scale(qh, M_HEAD_DIM) * gq_ref[...]).astype(BF16)
        s = _dot_nt(qh, k_ref[0, :, cols])
        pr = jnp.exp(s - jnp.max(s, axis=-1, keepdims=True))
        pr = pr / jnp.sum(pr, axis=-1, keepdims=True)
        heads.append(_dot(pr.astype(BF16), v_ref[0, :, cols]).astype(BF16))
    x = x + _dot(jnp.concatenate(heads, axis=-1), wo_ref[...])

    h = (x * _row_rms_scale(x, x.shape[-1]) * gffn_ref[...]).astype(BF16)
    for c in range(w1_ref.shape[1] // FFN_CHUNK):
        cols = slice(c * FFN_CHUNK, (c + 1) * FFN_CHUNK)
        a = jnp.maximum(_dot(h, w1_ref[:, cols]), 0.0)
        x = x + _dot((a * a).astype(BF16), w2_ref[cols, :])
    y_ref[0] = x


def _post(x3d, oa, ob, km, vm, w, tm):
    b, s, d = x3d.shape
    n_mem = km.shape[1]
    const = lambda bi, i: (0, 0)
    full = lambda a: pl.BlockSpec(a.shape, const, pipeline_mode=pl.Buffered(1))
    tile = lambda width: pl.BlockSpec((1, tm, width), lambda bi, i: (bi, i, 0))
    kvblk = pl.BlockSpec((1, n_mem, M_WIDTH), lambda bi, i: (bi, 0, 0))
    names = ['ga_out', 'gb_out', 'wout_a', 'wout_b', 'gmem', 'wq_m', 'gq_m']
    tail = ['wo_m', 'gffn', 'w1', 'w2']
    return pl.pallas_call(
        _post_kernel,
        grid=(b, s // tm),
        in_specs=[tile(d), tile(A_WIDTH), tile(B_WIDTH)] + [full(w[n]) for n in names] + [kvblk, kvblk]
                 + [full(w[n]) for n in tail],
        out_specs=tile(d),
        out_shape=jax.ShapeDtypeStruct((b, s, d), F32),
        compiler_params=_params(2),
        name="post",
    )(x3d, oa, ob, *[w[n] for n in names], km, vm, *[w[n] for n in tail])


def _rope_tables(seq, rope_dim):
    half = rope_dim // 2
    assert ROPE_SPAN % half == 0
    inv = ROPE_THETA ** (-jnp.arange(0, rope_dim, 2, dtype=F32) / rope_dim)
    ang = jnp.arange(seq, dtype=F32)[:, None] * inv[None, :]
    cos, sin = jnp.cos(ang), jnp.sin(ang)
    lane = np.arange(LANES)
    is_lo = lane < ROPE_SPAN
    is_hi = (lane >= LANES // 2) & (lane < LANES // 2 + ROPE_SPAN)
    idx = (lane % (LANES // 2)) % half
    cos_t = jnp.where(is_lo | is_hi, cos[:, idx], 1.0)
    sin_t = jnp.where(is_lo, -sin[:, idx], jnp.where(is_hi, sin[:, idx], 0.0))
    return jnp.stack([cos_t, sin_t]).astype(F32)


def _layer_weights(l, p):
    row = lambda v: v.reshape(1, -1).astype(F32)
    w_in = p['w_in'][l]
    d = w_in.shape[0]
    c_kr = 3 * A_WIDTH + Q_LORA + KV_LORA
    half_r = B_ROPE_DIM // 2
    zeros = lambda n: jnp.zeros((d, n), F32)
    kr = w_in[:, c_kr:]
    win = jnp.concatenate([
        w_in[:, :A_WIDTH][:, _A_PERM_ALL], w_in[:, A_WIDTH:2 * A_WIDTH][:, _A_PERM_ALL], w_in[:, 2 * A_WIDTH:c_kr],
        kr[:, :half_r], zeros(LANES // 2 - half_r), kr[:, half_r:], zeros(LANES // 2 - half_r)], axis=1)

    def b_lanes(a, with_rope):
        if with_rope:
            a = a[..., _B_PERM]
        else:
            z = jnp.zeros(a.shape[:-1] + (half_r,), a.dtype)
            a = jnp.concatenate([z, a[..., :_B_SPLIT], z, a[..., _B_SPLIT:]], axis=-1)
        return jnp.pad(a, [(0, 0)] * (a.ndim - 1) + [(0, LANES - B_QK_DIM)])

    wukv = p['w_ukv'][l].reshape(KV_LORA, B_HEADS, B_NOPE_DIM + B_V_DIM)
    b_gain = lambda g: jnp.tile(b_lanes(g, True), B_HEADS)
    a_gain = lambda g: jnp.tile(g, A_HEADS)[_A_PERM_ALL]
    w_out = p['w_out'][l]
    return {
        'gmix': row(p['norm_mix_g'][l]),
        'win': win.astype(BF16),
        'gqa': row(a_gain(p['a_q_norm_g'][l]) * (A_HEAD_DIM ** -0.5 * LOG2_E)),
        'gka': row(a_gain(p['a_k_norm_g'][l])),
        'gcq': row(p['b_cq_norm_g'][l]),
        'gckv': row(p['b_ckv_norm_g'][l]),
        'wuq': b_lanes(p['w_uq'][l].reshape(Q_LORA, B_HEADS, B_QK_DIM), True).reshape(Q_LORA, B_QK_PAD).astype(BF16),
        'wuk': b_lanes(wukv[:, :, :B_NOPE_DIM], False).reshape(KV_LORA, B_QK_PAD).astype(BF16),
        'wuv': wukv[:, :, B_NOPE_DIM:].reshape(KV_LORA, B_WIDTH).astype(BF16),
        'gqb': row(b_gain(p['b_q_norm_g'][l]) * (B_QK_DIM ** -0.5 * LOG2_E)),
        'gkb': row(b_gain(p['b_k_norm_g'][l])),
        'ga_out': row(p['a_out_norm_g'][l]),
        'gb_out': row(p['b_out_norm_g'][l]),
        'wout_a': w_out[:A_WIDTH].astype(BF16),
        'wout_b': w_out[A_WIDTH:].astype(BF16),
        'gmem': row(p['norm_mem_g'][l]),
        'gmem_kv': row(p['mem_kv_norm_g'][l]),
        'wq_m': p['m_wq'][l].astype(BF16),
        'wkv_m': p['m_wkv'][l].astype(BF16),
        'gq_m': row(p['m_q_norm_g'][l] * (M_HEAD_DIM ** -0.5)),
        'gk_m': row(p['m_k_norm_g'][l]),
        'wo_m': p['m_wo'][l].astype(BF16),
        'gffn': row(p['norm_ffn_g'][l]),
        'w1': p['w_ff1'][l].astype(BF16),
        'w2': p['w_ff2'][l].astype(BF16),
    }


def _tiles(seq):
    tm = 512 if seq % 512 == 0 else 256
    return dict(tm=tm, tq=256, tk=512)


def _trunk(x, mem, layers):
    b, s, d = x.shape
    assert s % (max(DILATIONS) * DIL_TQ) == 0 and s // max(DILATIONS) >= DIL_WIN
    t = _tiles(s)
    taba = _rope_tables(s, A_ROPE_DIM)
    tabb = _rope_tables(s, B_ROPE_DIM)
    head_ones = jnp.asarray(np.kron(np.eye(2), np.ones((LANES, LANES))), BF16)
    same_head = (_A_PERM[:, None] < A_HEAD_DIM) == (_A_PERM[None, :] < A_HEAD_DIM)
    pair_ones = jnp.asarray(np.kron(np.eye(2), same_head), BF16)
    x2d = x.reshape(b * s, d)
    for w in layers:
        w = dict(w, taba=taba, tabb=tabb, head_ones=head_ones, pair_ones=pair_ones)
        qa, ka, va, qb, kb, vbt = _proj(x2d, s, w, t['tm'])
        as3d = lambda a: a.reshape(b, s, a.shape[-1])
        oa = _dilated(as3d(qa), as3d(ka), as3d(va))
        ob = _mla(as3d(qb), as3d(kb), vbt, t['tq'], t['tk'])
        km, vm = _mem_kv(mem, w)
        x2d = _post(as3d(x2d), as3d(oa), as3d(ob), km, vm, w, t['tm']).reshape(b * s, d)
    return x2d.reshape(b, s, d)


def kernel(x_prompt, x_sample, mem_prompt, mem_sample, norm_mix_g, w_in, a_q_norm_g, a_k_norm_g, b_cq_norm_g, b_ckv_norm_g, w_uq, w_ukv, b_q_norm_g, b_k_norm_g, a_out_norm_g, b_out_norm_g, w_out, norm_mem_g, mem_kv_norm_g, m_wq, m_wkv, m_q_norm_g, m_k_norm_g, m_wo, norm_ffn_g, w_ff1, w_ff2):
    p = dict(norm_mix_g=norm_mix_g, w_in=w_in, a_q_norm_g=a_q_norm_g, a_k_norm_g=a_k_norm_g,
             b_cq_norm_g=b_cq_norm_g, b_ckv_norm_g=b_ckv_norm_g, w_uq=w_uq, w_ukv=w_ukv,
             b_q_norm_g=b_q_norm_g, b_k_norm_g=b_k_norm_g, a_out_norm_g=a_out_norm_g,
             b_out_norm_g=b_out_norm_g, w_out=w_out, norm_mem_g=norm_mem_g, mem_kv_norm_g=mem_kv_norm_g,
             m_wq=m_wq, m_wkv=m_wkv, m_q_norm_g=m_q_norm_g, m_k_norm_g=m_k_norm_g, m_wo=m_wo,
             norm_ffn_g=norm_ffn_g, w_ff1=w_ff1, w_ff2=w_ff2)
    layers = [_layer_weights(l, p) for l in range(w_in.shape[0])]
    return (_trunk(x_prompt, mem_prompt, layers), _trunk(x_sample, mem_sample, layers))
```

```python
import functools

import numpy as np
import jax
import jax.numpy as jnp
from jax import lax
from jax.experimental import pallas as pl
from jax.experimental.pallas import tpu as pltpu

EPS = 1e-6
ROPE_THETA = 500000.0

LANES = 128
SUBLANES = 8
A_HEADS = 8
A_HEAD_DIM = 64
A_WIDTH = A_HEADS * A_HEAD_DIM
A_ROPE_DIM = 16
A_HALF_WINDOW = 64
DILATIONS = (16, 4, 1)

B_HEADS = 8
B_NOPE_DIM = 64
B_ROPE_DIM = 32
B_QK_DIM = B_NOPE_DIM + B_ROPE_DIM
B_V_DIM = 64
B_WIDTH = B_HEADS * B_V_DIM
B_QK_PAD = B_HEADS * LANES
Q_LORA = 256
KV_LORA = 128

M_HEADS = 4
M_HEAD_DIM = 128
M_WIDTH = M_HEADS * M_HEAD_DIM

VMEM_LIMIT = 56 * 1024 * 1024

BF16 = jnp.bfloat16
F32 = jnp.float32
NEG_BIG = -1e30
LOG2_E = float(np.log2(np.e))


def _a_pair_perm():
    heads = [np.arange(A_HEAD_DIM), A_HEAD_DIM + np.arange(A_HEAD_DIM)]
    r = A_ROPE_DIM // 2
    rest = (A_HEAD_DIM - A_ROPE_DIM) // 2
    first = [h[:r] for h in heads] + [h[A_ROPE_DIM:A_ROPE_DIM + rest] for h in heads]
    second = [h[r:A_ROPE_DIM] for h in heads] + [h[A_ROPE_DIM + rest:] for h in heads]
    return np.concatenate(first + second)


ROPE_SPAN = 16
_A_PERM = _a_pair_perm()
_A_PERM_ALL = np.concatenate([c * LANES + _A_PERM for c in range(A_WIDTH // LANES)])
_B_SPLIT = LANES // 2 - B_ROPE_DIM // 2
_B_PERM = np.concatenate([np.arange(B_NOPE_DIM, B_NOPE_DIM + B_ROPE_DIM // 2), np.arange(0, _B_SPLIT),
                          np.arange(B_NOPE_DIM + B_ROPE_DIM // 2, B_QK_DIM), np.arange(_B_SPLIT, B_NOPE_DIM)])


def _lane_mask(member):
    lane = lax.broadcasted_iota(jnp.int32, (1, LANES), 1)
    edges = np.flatnonzero(np.diff(np.concatenate([[0], member.astype(np.int8), [0]])))
    mask = None
    for lo, hi in zip(edges[::2], edges[1::2]):
        run = (lane >= int(lo)) & (lane < int(hi))
        mask = run if mask is None else (mask | run)
    return mask


def _params(n_axes):
    return pltpu.CompilerParams(dimension_semantics=("arbitrary",) * n_axes,
                                vmem_limit_bytes=VMEM_LIMIT)


def _row_rms_scale(x, width):
    return lax.rsqrt(jnp.sum(x * x, axis=-1, keepdims=True) * (1.0 / width) + EPS)


def _dot(a, b):
    return jnp.dot(a, b, preferred_element_type=F32)


def _dot_nt(a, b):
    return lax.dot_general(a, b, (((1,), (1,)), ((), ())), preferred_element_type=F32)


def _rope_chunk(xc, cos, sin):
    return xc * cos + pltpu.roll(xc, LANES // 2, axis=1) * sin


def _proj_kernel(x_ref, gmix_ref, win_ref, gqa_ref, gka_ref, gcq_ref, gckv_ref,
                 wuq_ref, wuk_ref, wuv_ref, gqb_ref, gkb_ref, aones_ref, ones_ref, taba_ref, tabb_ref,
                 qa_ref, ka_ref, va_ref, qb_ref, kb_ref, vbt_ref):
    x = x_ref[...]
    h = (x * _row_rms_scale(x, x.shape[-1]) * gmix_ref[...]).astype(BF16)
    proj = _dot(h, win_ref[...])

    cos_a, sin_a = taba_ref[0], taba_ref[1]
    cos_b, sin_b = tabb_ref[0], tabb_ref[1]

    def a_heads(col0, g_ref, out_ref):
        for c2 in range(A_WIDTH // (2 * LANES)):
            xs = proj[:, col0 + 2 * c2 * LANES: col0 + 2 * (c2 + 1) * LANES]
            ssq = _dot((xs * xs).astype(BF16), aones_ref[...])
            for i in range(2):
                c = 2 * c2 + i
                lanes = slice(i * LANES, (i + 1) * LANES)
                rs = lax.rsqrt(ssq[:, lanes] * (1.0 / A_HEAD_DIM) + EPS)
                y = xs[:, lanes] * rs * g_ref[:, c * LANES:(c + 1) * LANES]
                y = _rope_chunk(y, cos_a, sin_a)
                out_ref[:, c * LANES:(c + 1) * LANES] = y.astype(out_ref.dtype)

    a_heads(0, gqa_ref, qa_ref)
    a_heads(A_WIDTH, gka_ref, ka_ref)
    va_ref[...] = proj[:, 2 * A_WIDTH:3 * A_WIDTH].astype(va_ref.dtype)

    c0 = 3 * A_WIDTH
    cq = proj[:, c0:c0 + Q_LORA]
    ckv = proj[:, c0 + Q_LORA:c0 + Q_LORA + KV_LORA]
    kr = proj[:, c0 + Q_LORA + KV_LORA:c0 + Q_LORA + 2 * KV_LORA]
    cqn = (cq * _row_rms_scale(cq, Q_LORA) * gcq_ref[...]).astype(BF16)
    ckvn = (ckv * _row_rms_scale(ckv, KV_LORA) * gckv_ref[...]).astype(BF16)
    qb = _dot(cqn, wuq_ref[...])
    kn = _dot(ckvn, wuk_ref[...])
    vbt_ref[...] = _dot(ckvn, wuv_ref[...]).T.astype(vbt_ref.dtype)

    def b_heads(get_chunk, g_ref, out_ref):
        for c2 in range(B_HEADS // 2):
            xs = [get_chunk(2 * c2 + i) for i in range(2)]
            ssq = _dot(jnp.concatenate([xc * xc for xc in xs], axis=1).astype(BF16), ones_ref[...])
            for i, xc in enumerate(xs):
                c = 2 * c2 + i
                rs = lax.rsqrt(ssq[:, i * LANES:(i + 1) * LANES] * (1.0 / B_QK_DIM) + EPS)
                y = xc * rs * g_ref[:, c * LANES:(c + 1) * LANES]
                y = _rope_chunk(y, cos_b, sin_b)
                out_ref[:, c * LANES:(c + 1) * LANES] = y.astype(out_ref.dtype)

    b_heads(lambda c: qb[:, c * LANES:(c + 1) * LANES], gqb_ref, qb_ref)
    b_heads(lambda c: kn[:, c * LANES:(c + 1) * LANES] + kr, gkb_ref, kb_ref)


def _proj(x2d, seq, w, tm):
    n, d = x2d.shape
    nblk_seq = seq // tm
    row = lambda i: (i, 0)
    const = lambda i: (0, 0)
    tab = lambda i: (0, i % nblk_seq, 0)
    full = lambda a: pl.BlockSpec(a.shape, const)
    widths = [A_WIDTH, A_WIDTH, A_WIDTH, B_QK_PAD, B_QK_PAD]
    return pl.pallas_call(
        _proj_kernel,
        grid=(n // tm,),
        in_specs=[pl.BlockSpec((tm, d), row), full(w['gmix']), full(w['win']), full(w['gqa']), full(w['gka']),
                  full(w['gcq']), full(w['gckv']), full(w['wuq']), full(w['wuk']), full(w['wuv']),
                  full(w['gqb']), full(w['gkb']), full(w['pair_ones']), full(w['head_ones']),
                  pl.BlockSpec((2, tm, LANES), tab), pl.BlockSpec((2, tm, LANES), tab)],
        out_specs=[pl.BlockSpec((tm, wd), row) for wd in widths]
                  + [pl.BlockSpec((B_WIDTH, tm), lambda i: (0, i))],
        out_shape=[jax.ShapeDtypeStruct((n, wd), F32 if wd == A_WIDTH else BF16) for wd in widths]
                  + [jax.ShapeDtypeStruct((B_WIDTH, n), BF16)],
        compiler_params=_params(1),
        name="proj",
    )(x2d, w['gmix'], w['win'], w['gqa'], w['gka'], w['gcq'], w['gckv'], w['wuq'], w['wuk'], w['wuv'],
      w['gqb'], w['gkb'], w['pair_ones'], w['head_ones'], w['taba'], w['tabb'])


DIL_TQ = 128
DIL_WIN = DIL_TQ + 2 * A_HALF_WINDOW
DIL_SLOTS = 4
DIL_STEPS = 8


def _strided_rows(start, size, stride):
    return pl.ds(start, size) if stride == 1 else pl.ds(start, size, stride=stride)


def _dilated_kernel(q_ref, k_ref, v_ref, o_ref, lse_scr, s_scr, bias_scr):
    seq = q_ref.shape[1]
    lane = lax.broadcasted_iota(jnp.int32, (1, LANES), 1)
    low_head = lane < A_HEAD_DIM
    q_first = _lane_mask(_A_PERM < A_HEAD_DIM)
    rel = (lax.broadcasted_iota(jnp.int32, (2 * DIL_TQ, DIL_WIN), 1)
           - lax.broadcasted_iota(jnp.int32, (2 * DIL_TQ, DIL_WIN), 0) % DIL_TQ)
    for place in range(3):
        bias_scr[place] = jnp.where(jnp.abs(rel - place * A_HALF_WINDOW) <= A_HALF_WINDOW, 0.0, NEG_BIG)

    for dil in DILATIONS:
        sub_len = seq // dil
        n_blk = sub_len // DIL_TQ
        n_total = dil * n_blk

        def coords(idx, dil=dil, sub_len=sub_len, n_blk=n_blk):
            r = idx // n_blk
            q0 = (idx % n_blk) * DIL_TQ
            k0 = jnp.clip(q0 - A_HALF_WINDOW, 0, sub_len - DIL_WIN)
            rows_q = _strided_rows(r + dil * q0, DIL_TQ, dil)
            rows_k = _strided_rows(r + dil * k0, DIL_WIN, dil)
            return rows_q, rows_k, (q0 - k0) // A_HALF_WINDOW

        def scores_into(slot, idx, coords=coords):
            rows_q, rows_k, place = coords(idx)
            qp = q_ref[0, rows_q, :]
            kw = k_ref[0, rows_k, :].astype(BF16)
            q2 = jnp.concatenate([jnp.where(q_first, qp, 0.0), jnp.where(q_first, 0.0, qp)], axis=0).astype(BF16)
            s_scr[slot] = _dot_nt(q2, kw) + bias_scr[place]

        def finish(slot, idx, dil=dil, coords=coords):
            rows_q, rows_k, _ = coords(idx)
            vw = v_ref[0, rows_k, :].astype(BF16)
            m2 = jnp.max(s_scr[slot], axis=-1, keepdims=True)
            pr = jnp.exp2(s_scr[slot] - m2)
            den2 = jnp.sum(pr, axis=-1, keepdims=True)
            o2 = _dot(pr.astype(BF16), vw)
            pick = lambda a: jnp.where(low_head, a[:DIL_TQ], a[DIL_TQ:])
            den = pick(den2)
            o = pick(o2) / den
            lse = pick(m2) + jnp.log2(den)
            if dil != DILATIONS[0]:
                o_prev, lse_prev = o_ref[0, rows_q, :], lse_scr[rows_q, :]
                top = jnp.maximum(lse_prev, lse)
                w_prev, w_new = jnp.exp2(lse_prev - top), jnp.exp2(lse - top)
                tot = w_prev + w_new
                o = (w_prev * o_prev + w_new * o) / tot
                lse = top + jnp.log2(tot)
            o_ref[0, rows_q, :] = o
            if dil != DILATIONS[-1]:
                lse_scr[rows_q, :] = lse

        assert n_total % DIL_STEPS == 0 and DIL_STEPS % DIL_SLOTS == 0
        for slot in range(DIL_SLOTS - 1):
            scores_into(slot, slot)

        def step(i, carry, scores_into=scores_into, finish=finish, n_total=n_total):
            for u in range(DIL_STEPS):
                idx = DIL_STEPS * i + u
                ahead = idx + DIL_SLOTS - 1
                scores_into((u + DIL_SLOTS - 1) % DIL_SLOTS, jnp.where(ahead >= n_total, ahead - n_total, ahead))
                finish(u % DIL_SLOTS, idx)
            return carry

        lax.fori_loop(0, n_total // DIL_STEPS, step, 0)


def _dilated(qa, ka, va):
    b, s, w = qa.shape
    spec = pl.BlockSpec((1, s, LANES), lambda bi, p: (bi, 0, p))
    out = pl.pallas_call(
        _dilated_kernel,
        grid=(b, w // LANES),
        in_specs=[spec, spec, spec],
        out_specs=spec,
        out_shape=jax.ShapeDtypeStruct((b, s, w), F32),
        scratch_shapes=[pltpu.VMEM((s, LANES), F32),
                        pltpu.VMEM((DIL_SLOTS, 2 * DIL_TQ, DIL_WIN), F32),
                        pltpu.VMEM((3, 2 * DIL_TQ, DIL_WIN), F32)],
        compiler_params=_params(2),
        name="dilated",
    )(qa, ka, va)
    return out.reshape(b * s, w)


MLA_DEN_ROWS = 16
MLA_STEPS = 32
MLA_SLOTS = 16
MLA_AHEAD = 2


def _mla_kernel(q_ref, k_ref, vt_ref, o_ref, s_scr, smax_scr, *, tq, tk, steps):
    seq = k_ref.shape[1]
    heads = range(2)
    n_blocks = seq // tk
    n_tiles = seq // tq
    slots = s_scr.shape[0]
    tiles_per_iter = steps // n_blocks
    assert steps % n_blocks == 0 and n_tiles % tiles_per_iter == 0 and steps % slots == 0 and MLA_AHEAD < slots

    def scores_into(slot, tile, blk, hh):
        q0 = tile * tq if isinstance(tile, int) else pl.multiple_of(tile * tq, tq)
        cols = slice(hh * LANES, (hh + 1) * LANES)
        st = _dot_nt(k_ref[0, blk * tk:(blk + 1) * tk, cols], q_ref[0, pl.ds(q0, tq), cols])
        s_scr[slot, hh] = st
        smax_scr[slot, hh] = jnp.max(st.reshape(tk // SUBLANES, SUBLANES, tq), axis=0)

    def fresh():
        return [(jnp.full((1, tq), NEG_BIG, F32), jnp.zeros((B_V_DIM + MLA_DEN_ROWS, tq), F32)) for _ in heads]

    for u in range(MLA_AHEAD):
        for hh in heads:
            scores_into(u, u // n_blocks, u % n_blocks, hh)

    def iteration(it, carry):
        tile0 = it * tiles_per_iter
        state = fresh()
        for u in range(steps):
            tile, blk = tile0 + u // n_blocks, u % n_blocks
            v = u + MLA_AHEAD
            ahead_tile = jnp.minimum(tile0 + v // n_blocks, n_tiles - 1)
            for hh in heads:
                scores_into(v % slots, ahead_tile, v % n_blocks, hh)
                m, acc = state[hh]
                m_new = jnp.maximum(m, jnp.max(smax_scr[u % slots, hh], axis=0, keepdims=True))
                alpha = jnp.exp2(m - m_new)
                pt = jnp.exp2(s_scr[u % slots, hh] - m_new).astype(BF16)
                vt1 = jnp.concatenate([vt_ref[hh * B_V_DIM:(hh + 1) * B_V_DIM, blk * tk:(blk + 1) * tk],
                                       jnp.ones((MLA_DEN_ROWS, tk), BF16)], axis=0)
                acc = alpha * acc + _dot(vt1, pt)
                state[hh] = (m_new, acc)
            if blk == n_blocks - 1:
                ot = jnp.concatenate([acc[:B_V_DIM] / acc[B_V_DIM:B_V_DIM + 1] for _, acc in state], axis=0)
                o_ref[0, pl.ds(pl.multiple_of(tile * tq, tq), tq), :] = ot.T
                state = fresh()
        return carry

    lax.fori_loop(0, n_tiles // tiles_per_iter, iteration, 0)


def _mla(qb, kb, vbt, tq, tk):
    b, s, _ = qb.shape
    qk_spec = pl.BlockSpec((1, s, 2 * LANES), lambda bi, p: (bi, 0, p))
    n_blocks = s // tk
    steps = n_blocks * max(1, MLA_STEPS // n_blocks)
    slots = min(steps, MLA_SLOTS)
    out = pl.pallas_call(
        functools.partial(_mla_kernel, tq=tq, tk=tk, steps=steps),
        grid=(b, B_HEADS // 2),
        in_specs=[qk_spec, qk_spec, pl.BlockSpec((LANES, s), lambda bi, p: (p, bi))],
        out_specs=pl.BlockSpec((1, s, LANES), lambda bi, p: (bi, 0, p)),
        out_shape=jax.ShapeDtypeStruct((b, s, B_WIDTH), F32),
        scratch_shapes=[pltpu.VMEM((slots, 2, tk, tq), F32), pltpu.VMEM((slots, 2, SUBLANES, tq), F32)],
        compiler_params=_params(2),
        name="mla",
    )(qb, kb, vbt)
    return out.reshape(b * s, B_WIDTH)


def _mem_kv_kernel(mem_ref, g_ref, wkv_ref, gk_ref, k_ref, v_ref):
    mem = mem_ref[0]
    mm = (mem * _row_rms_scale(mem, mem.shape[-1]) * g_ref[...]).astype(BF16)
    kv = _dot(mm, wkv_ref[...])
    for hd in range(M_HEADS):
        cols = slice(hd * M_HEAD_DIM, (hd + 1) * M_HEAD_DIM)
        kh = kv[:, cols]
        k_ref[0, :, cols] = (kh * _row_rms_scale(kh, M_HEAD_DIM) * gk_ref[...]).astype(k_ref.dtype)
    v_ref[0] = kv[:, M_WIDTH:].astype(v_ref.dtype)


def _mem_kv(mem, w):
    b, n_mem, d = mem.shape
    const = lambda bi: (0, 0)
    full = lambda a: pl.BlockSpec(a.shape, const)
    blk = pl.BlockSpec((1, n_mem, M_WIDTH), lambda bi: (bi, 0, 0))
    return pl.pallas_call(
        _mem_kv_kernel,
        grid=(b,),
        in_specs=[pl.BlockSpec((1, n_mem, d), lambda bi: (bi, 0, 0)),
                  full(w['gmem_kv']), full(w['wkv_m']), full(w['gk_m'])],
        out_specs=[blk, blk],
        out_shape=[jax.ShapeDtypeStruct((b, n_mem, M_WIDTH), BF16)] * 2,
        compiler_params=_params(1),
        name="mem_kv",
    )(mem, w['gmem_kv'], w['wkv_m'], w['gk_m'])


FFN_CHUNK = 1024


def _post_kernel(x_ref, oa_ref, ob_ref, ga_ref, gb_ref, wa_ref, wb_ref,
                 gmem_ref, wq_ref, gq_ref, k_ref, v_ref, wo_ref, gffn_ref, w1_ref, w2_ref, y_ref):
    oa, ob = oa_ref[0], ob_ref[0]
    na = (oa * _row_rms_scale(oa, A_WIDTH) * ga_ref[...]).astype(BF16)
    nb = (ob * _row_rms_scale(ob, B_WIDTH) * gb_ref[...]).astype(BF16)
    x = x_ref[0] + _dot(na, wa_ref[...]) + _dot(nb, wb_ref[...])

    h = (x * _row_rms_scale(x, x.shape[-1]) * gmem_ref[...]).astype(BF16)
    q = _dot(h, wq_ref[...])
    heads = []
    for hd in range(M_HEADS):
        cols = slice(hd * M_HEAD_DIM, (hd + 1) * M_HEAD_DIM)
        qh = q[:, cols]
        qh = (qh * _row_rms_scale(qh, M_HEAD_DIM) * gq_ref[...]).astype(BF16)
        s = _dot_nt(qh, k_ref[0, :, cols])
        pr = jnp.exp(s - jnp.max(s, axis=-1, keepdims=True))
        pr = pr / jnp.sum(pr, axis=-1, keepdims=True)
        heads.append(_dot(pr.astype(BF16), v_ref[0, :, cols]).astype(BF16))
    x = x + _dot(jnp.concatenate(heads, axis=-1), wo_ref[...])

    h = (x * _row_rms_scale(x, x.shape[-1]) * gffn_ref[...]).astype(BF16)
    for c in range(w1_ref.shape[1] // FFN_CHUNK):
        cols = slice(c * FFN_CHUNK, (c + 1) * FFN_CHUNK)
        a = jnp.maximum(_dot(h, w1_ref[:, cols]), 0.0)
        x = x + _dot((a * a).astype(BF16), w2_ref[cols, :])
    y_ref[0] = x


def _post(x3d, oa, ob, km, vm, w, tm):
    b, s, d = x3d.shape
    n_mem = km.shape[1]
    const = lambda bi, i: (0, 0)
    full = lambda a: pl.BlockSpec(a.shape, const, pipeline_mode=pl.Buffered(1))
    tile = lambda width: pl.BlockSpec((1, tm, width), lambda bi, i: (bi, i, 0))
    kvblk = pl.BlockSpec((1, n_mem, M_WIDTH), lambda bi, i: (bi, 0, 0))
    names = ['ga_out', 'gb_out', 'wout_a', 'wout_b', 'gmem', 'wq_m', 'gq_m']
    tail = ['wo_m', 'gffn', 'w1', 'w2']
    return pl.pallas_call(
        _post_kernel,
        grid=(b, s // tm),
        in_specs=[tile(d), tile(A_WIDTH), tile(B_WIDTH)] + [full(w[n]) for n in names] + [kvblk, kvblk]
                 + [full(w[n]) for n in tail],
        out_specs=tile(d),
        out_shape=jax.ShapeDtypeStruct((b, s, d), F32),
        compiler_params=_params(2),
        name="post",
    )(x3d, oa, ob, *[w[n] for n in names], km, vm, *[w[n] for n in tail])


def _rope_tables(seq, rope_dim):
    half = rope_dim // 2
    assert ROPE_SPAN % half == 0
    inv = ROPE_THETA ** (-jnp.arange(0, rope_dim, 2, dtype=F32) / rope_dim)
    ang = jnp.arange(seq, dtype=F32)[:, None] * inv[None, :]
    cos, sin = jnp.cos(ang), jnp.sin(ang)
    lane = np.arange(LANES)
    is_lo = lane < ROPE_SPAN
    is_hi = (lane >= LANES // 2) & (lane < LANES // 2 + ROPE_SPAN)
    idx = (lane % (LANES // 2)) % half
    cos_t = jnp.where(is_lo | is_hi, cos[:, idx], 1.0)
    sin_t = jnp.where(is_lo, -sin[:, idx], jnp.where(is_hi, sin[:, idx], 0.0))
    return jnp.stack([cos_t, sin_t]).astype(F32)


def _layer_weights(p):
    n_layers, d, _ = p['w_in'].shape
    row = lambda v: v.reshape(n_layers, 1, -1).astype(F32)
    w_in = p['w_in']
    c_kr = 3 * A_WIDTH + Q_LORA + KV_LORA
    half_r = B_ROPE_DIM // 2
    zeros = lambda n: jnp.zeros((n_layers, d, n), F32)
    kr = w_in[..., c_kr:]
    win = jnp.concatenate([
        w_in[..., :A_WIDTH][..., _A_PERM_ALL], w_in[..., A_WIDTH:2 * A_WIDTH][..., _A_PERM_ALL],
        w_in[..., 2 * A_WIDTH:c_kr],
        kr[..., :half_r], zeros(LANES // 2 - half_r), kr[..., half_r:], zeros(LANES // 2 - half_r)], axis=-1)

    def b_lanes(a, with_rope):
        if with_rope:
            a = a[..., _B_PERM]
        else:
            z = jnp.zeros(a.shape[:-1] + (half_r,), a.dtype)
            a = jnp.concatenate([z, a[..., :_B_SPLIT], z, a[..., _B_SPLIT:]], axis=-1)
        return jnp.pad(a, [(0, 0)] * (a.ndim - 1) + [(0, LANES - B_QK_DIM)])

    wuq = p['w_uq'].reshape(n_layers, Q_LORA, B_HEADS, B_QK_DIM)
    wukv = p['w_ukv'].reshape(n_layers, KV_LORA, B_HEADS, B_NOPE_DIM + B_V_DIM)
    b_gain = lambda g: jnp.tile(b_lanes(g, True), (1, B_HEADS))
    a_gain = lambda g: jnp.tile(g, (1, A_HEADS))[..., _A_PERM_ALL]
    stacked = {
        'gmix': row(p['norm_mix_g']),
        'win': win.astype(BF16),
        'gqa': row(a_gain(p['a_q_norm_g']) * (A_HEAD_DIM ** -0.5 * LOG2_E)),
        'gka': row(a_gain(p['a_k_norm_g'])),
        'gcq': row(p['b_cq_norm_g']),
        'gckv': row(p['b_ckv_norm_g']),
        'wuq': b_lanes(wuq, True).reshape(n_layers, Q_LORA, B_QK_PAD).astype(BF16),
        'wuk': b_lanes(wukv[..., :B_NOPE_DIM], False).reshape(n_layers, KV_LORA, B_QK_PAD).astype(BF16),
        'wuv': wukv[..., B_NOPE_DIM:].reshape(n_layers, KV_LORA, B_WIDTH).astype(BF16),
        'gqb': row(b_gain(p['b_q_norm_g']) * (B_QK_DIM ** -0.5 * LOG2_E)),
        'gkb': row(b_gain(p['b_k_norm_g'])),
        'ga_out': row(p['a_out_norm_g']),
        'gb_out': row(p['b_out_norm_g']),
        'wout_a': p['w_out'][:, :A_WIDTH].astype(BF16),
        'wout_b': p['w_out'][:, A_WIDTH:].astype(BF16),
        'gmem': row(p['norm_mem_g']),
        'gmem_kv': row(p['mem_kv_norm_g']),
        'wq_m': p['m_wq'].astype(BF16),
        'wkv_m': p['m_wkv'].astype(BF16),
        'gq_m': row(p['m_q_norm_g'] * (M_HEAD_DIM ** -0.5)),
        'gk_m': row(p['m_k_norm_g']),
        'wo_m': p['m_wo'].astype(BF16),
        'gffn': row(p['norm_ffn_g']),
        'w1': p['w_ff1'].astype(BF16),
        'w2': p['w_ff2'].astype(BF16),
    }
    return [{name: a[l] for name, a in stacked.items()} for l in range(n_layers)]


def _tiles(seq):
    tm = 512 if seq % 512 == 0 else 256
    return dict(tm=tm, tq=256, tk=512)


def _trunk(x, mem, layers):
    b, s, d = x.shape
    assert s % (max(DILATIONS) * DIL_TQ) == 0 and s // max(DILATIONS) >= DIL_WIN
    t = _tiles(s)
    taba = _rope_tables(s, A_ROPE_DIM)
    tabb = _rope_tables(s, B_ROPE_DIM)
    head_ones = jnp.asarray(np.kron(np.eye(2), np.ones((LANES, LANES))), BF16)
    same_head = (_A_PERM[:, None] < A_HEAD_DIM) == (_A_PERM[None, :] < A_HEAD_DIM)
    pair_ones = jnp.asarray(np.kron(np.eye(2), same_head), BF16)
    x2d = x.reshape(b * s, d)
    for w in layers:
        w = dict(w, taba=taba, tabb=tabb, head_ones=head_ones, pair_ones=pair_ones)
        qa, ka, va, qb, kb, vbt = _proj(x2d, s, w, t['tm'])
        as3d = lambda a: a.reshape(b, s, a.shape[-1])
        oa = _dilated(as3d(qa), as3d(ka), as3d(va))
        ob = _mla(as3d(qb), as3d(kb), vbt, t['tq'], t['tk'])
        km, vm = _mem_kv(mem, w)
        x2d = _post(as3d(x2d), as3d(oa), as3d(ob), km, vm, w, t['tm']).reshape(b * s, d)
    return x2d.reshape(b, s, d)


def kernel(x_prompt, x_sample, mem_prompt, mem_sample, norm_mix_g, w_in, a_q_norm_g, a_k_norm_g, b_cq_norm_g, b_ckv_norm_g, w_uq, w_ukv, b_q_norm_g, b_k_norm_g, a_out_norm_g, b_out_norm_g, w_out, norm_mem_g, mem_kv_norm_g, m_wq, m_wkv, m_q_norm_g, m_k_norm_g, m_wo, norm_ffn_g, w_ff1, w_ff2):
    p = dict(norm_mix_g=norm_mix_g, w_in=w_in, a_q_norm_g=a_q_norm_g, a_k_norm_g=a_k_norm_g,
             b_cq_norm_g=b_cq_norm_g, b_ckv_norm_g=b_ckv_norm_g, w_uq=w_uq, w_ukv=w_ukv,
             b_q_norm_g=b_q_norm_g, b_k_norm_g=b_k_norm_g, a_out_norm_g=a_out_norm_g,
             b_out_norm_g=b_out_norm_g, w_out=w_out, norm_mem_g=norm_mem_g, mem_kv_norm_g=mem_kv_norm_g,
             m_wq=m_wq, m_wkv=m_wkv, m_q_norm_g=m_q_norm_g, m_k_norm_g=m_k_norm_g, m_wo=m_wo,
             norm_ffn_g=norm_ffn_g, w_ff1=w_ff1, w_ff2=w_ff2)
    layers = _layer_weights(p)
    return (_trunk(x_prompt, mem_prompt, layers), _trunk(x_sample, mem_sample, layers))
```

```python
import functools

import numpy as np
import jax
import jax.numpy as jnp
from jax import lax
from jax.experimental import pallas as pl
from jax.experimental.pallas import tpu as pltpu

EPS = 1e-6
ROPE_THETA = 500000.0

LANES = 128
SUBLANES = 8
A_HEADS = 8
A_HEAD_DIM = 64
A_WIDTH = A_HEADS * A_HEAD_DIM
A_ROPE_DIM = 16
A_HALF_WINDOW = 64
DILATIONS = (16, 4, 1)

B_HEADS = 8
B_NOPE_DIM = 64
B_ROPE_DIM = 32
B_QK_DIM = B_NOPE_DIM + B_ROPE_DIM
B_V_DIM = 64
B_WIDTH = B_HEADS * B_V_DIM
B_QK_PAD = B_HEADS * LANES
Q_LORA = 256
KV_LORA = 128

M_HEADS = 4
M_HEAD_DIM = 128
M_WIDTH = M_HEADS * M_HEAD_DIM

VMEM_LIMIT = 56 * 1024 * 1024

BF16 = jnp.bfloat16
F32 = jnp.float32
NEG_BIG = -1e30
LOG2_E = float(np.log2(np.e))


def _a_pair_perm():
    heads = [np.arange(A_HEAD_DIM), A_HEAD_DIM + np.arange(A_HEAD_DIM)]
    r = A_ROPE_DIM // 2
    rest = (A_HEAD_DIM - A_ROPE_DIM) // 2
    first = [h[:r] for h in heads] + [h[A_ROPE_DIM:A_ROPE_DIM + rest] for h in heads]
    second = [h[r:A_ROPE_DIM] for h in heads] + [h[A_ROPE_DIM + rest:] for h in heads]
    return np.concatenate(first + second)


ROPE_SPAN = 16
_A_PERM = _a_pair_perm()
_A_PERM_ALL = np.concatenate([c * LANES + _A_PERM for c in range(A_WIDTH // LANES)])
_B_SPLIT = LANES // 2 - B_ROPE_DIM // 2
_B_PERM = np.concatenate([np.arange(B_NOPE_DIM, B_NOPE_DIM + B_ROPE_DIM // 2), np.arange(0, _B_SPLIT),
                          np.arange(B_NOPE_DIM + B_ROPE_DIM // 2, B_QK_DIM), np.arange(_B_SPLIT, B_NOPE_DIM)])


def _lane_mask(member):
    lane = lax.broadcasted_iota(jnp.int32, (1, LANES), 1)
    edges = np.flatnonzero(np.diff(np.concatenate([[0], member.astype(np.int8), [0]])))
    mask = None
    for lo, hi in zip(edges[::2], edges[1::2]):
        run = (lane >= int(lo)) & (lane < int(hi))
        mask = run if mask is None else (mask | run)
    return mask


def _params(n_axes):
    return pltpu.CompilerParams(dimension_semantics=("arbitrary",) * n_axes,
                                vmem_limit_bytes=VMEM_LIMIT)


def _row_rms_scale(x, width):
    return lax.rsqrt(jnp.sum(x * x, axis=-1, keepdims=True) * (1.0 / width) + EPS)


def _dot(a, b):
    return jnp.dot(a, b, preferred_element_type=F32)


def _dot_nt(a, b):
    return lax.dot_general(a, b, (((1,), (1,)), ((), ())), preferred_element_type=F32)


def _rope_chunk(xc, cos, sin):
    return xc * cos + pltpu.roll(xc, LANES // 2, axis=1) * sin


def _proj_kernel(x_ref, gmix_ref, win_ref, gqa_ref, gka_ref, gcq_ref, gckv_ref,
                 wuq_ref, wuk_ref, wuv_ref, gqb_ref, gkb_ref, aones_ref, ones_ref, taba_ref, tabb_ref,
                 qa_ref, ka_ref, va_ref, qb_ref, kb_ref, vbt_ref):
    x = x_ref[...]
    h = (x * _row_rms_scale(x, x.shape[-1]) * gmix_ref[...]).astype(BF16)
    proj = _dot(h, win_ref[...])

    cos_a, sin_a = taba_ref[0], taba_ref[1]
    cos_b, sin_b = tabb_ref[0], tabb_ref[1]

    def a_heads(col0, g_ref, out_ref):
        for c2 in range(A_WIDTH // (2 * LANES)):
            xs = proj[:, col0 + 2 * c2 * LANES: col0 + 2 * (c2 + 1) * LANES]
            ssq = _dot((xs * xs).astype(BF16), aones_ref[...])
            for i in range(2):
                c = 2 * c2 + i
                lanes = slice(i * LANES, (i + 1) * LANES)
                rs = lax.rsqrt(ssq[:, lanes] * (1.0 / A_HEAD_DIM) + EPS)
                y = xs[:, lanes] * rs * g_ref[:, c * LANES:(c + 1) * LANES]
                y = _rope_chunk(y, cos_a, sin_a)
                out_ref[:, c * LANES:(c + 1) * LANES] = y.astype(out_ref.dtype)

    a_heads(0, gqa_ref, qa_ref)
    a_heads(A_WIDTH, gka_ref, ka_ref)
    va_ref[...] = proj[:, 2 * A_WIDTH:3 * A_WIDTH].astype(va_ref.dtype)

    c0 = 3 * A_WIDTH
    cq = proj[:, c0:c0 + Q_LORA]
    ckv = proj[:, c0 + Q_LORA:c0 + Q_LORA + KV_LORA]
    kr = proj[:, c0 + Q_LORA + KV_LORA:c0 + Q_LORA + 2 * KV_LORA]
    cqn = (cq * _row_rms_scale(cq, Q_LORA) * gcq_ref[...]).astype(BF16)
    ckvn = (ckv * _row_rms_scale(ckv, KV_LORA) * gckv_ref[...]).astype(BF16)
    qb = _dot(cqn, wuq_ref[...])
    kn = _dot(ckvn, wuk_ref[...])
    vbt_ref[...] = _dot(ckvn, wuv_ref[...]).T.astype(vbt_ref.dtype)

    def b_heads(get_chunk, g_ref, out_ref):
        for c2 in range(B_HEADS // 2):
            xs = [get_chunk(2 * c2 + i) for i in range(2)]
            ssq = _dot(jnp.concatenate([xc * xc for xc in xs], axis=1).astype(BF16), ones_ref[...])
            for i, xc in enumerate(xs):
                c = 2 * c2 + i
                rs = lax.rsqrt(ssq[:, i * LANES:(i + 1) * LANES] * (1.0 / B_QK_DIM) + EPS)
                y = xc * rs * g_ref[:, c * LANES:(c + 1) * LANES]
                y = _rope_chunk(y, cos_b, sin_b)
                out_ref[:, c * LANES:(c + 1) * LANES] = y.astype(out_ref.dtype)

    b_heads(lambda c: qb[:, c * LANES:(c + 1) * LANES], gqb_ref, qb_ref)
    b_heads(lambda c: kn[:, c * LANES:(c + 1) * LANES] + kr, gkb_ref, kb_ref)


def _proj(x2d, seq, w, tm):
    n, d = x2d.shape
    nblk_seq = seq // tm
    row = lambda i: (i, 0)
    const = lambda i: (0, 0)
    tab = lambda i: (0, i % nblk_seq, 0)
    full = lambda a: pl.BlockSpec(a.shape, const)
    widths = [A_WIDTH, A_WIDTH, A_WIDTH, B_QK_PAD, B_QK_PAD]
    return pl.pallas_call(
        _proj_kernel,
        grid=(n // tm,),
        in_specs=[pl.BlockSpec((tm, d), row), full(w['gmix']), full(w['win']), full(w['gqa']), full(w['gka']),
                  full(w['gcq']), full(w['gckv']), full(w['wuq']), full(w['wuk']), full(w['wuv']),
                  full(w['gqb']), full(w['gkb']), full(w['pair_ones']), full(w['head_ones']),
                  pl.BlockSpec((2, tm, LANES), tab), pl.BlockSpec((2, tm, LANES), tab)],
        out_specs=[pl.BlockSpec((tm, wd), row) for wd in widths]
                  + [pl.BlockSpec((B_WIDTH, tm), lambda i: (0, i))],
        out_shape=[jax.ShapeDtypeStruct((n, wd), F32 if wd == A_WIDTH else BF16) for wd in widths]
                  + [jax.ShapeDtypeStruct((B_WIDTH, n), BF16)],
        compiler_params=_params(1),
        name="proj",
    )(x2d, w['gmix'], w['win'], w['gqa'], w['gka'], w['gcq'], w['gckv'], w['wuq'], w['wuk'], w['wuv'],
      w['gqb'], w['gkb'], w['pair_ones'], w['head_ones'], w['taba'], w['tabb'])


DIL_TQ = 128
DIL_WIN = DIL_TQ + 2 * A_HALF_WINDOW
DIL_SLOTS = 4
DIL_STEPS = 8


def _strided_rows(start, size, stride):
    return pl.ds(start, size) if stride == 1 else pl.ds(start, size, stride=stride)


def _dilated_kernel(q_ref, k_ref, v_ref, o_ref, m_scr, den_scr, s_scr, bias_scr):
    seq = q_ref.shape[1]
    lane = lax.broadcasted_iota(jnp.int32, (1, LANES), 1)
    low_head = lane < A_HEAD_DIM
    q_first = _lane_mask(_A_PERM < A_HEAD_DIM)
    rel = (lax.broadcasted_iota(jnp.int32, (2 * DIL_TQ, DIL_WIN), 1)
           - lax.broadcasted_iota(jnp.int32, (2 * DIL_TQ, DIL_WIN), 0) % DIL_TQ)
    for place in range(3):
        bias_scr[place] = jnp.where(jnp.abs(rel - place * A_HALF_WINDOW) <= A_HALF_WINDOW, 0.0, NEG_BIG)

    for dil in DILATIONS:
        sub_len = seq // dil
        n_blk = sub_len // DIL_TQ
        n_total = dil * n_blk

        def coords(idx, dil=dil, sub_len=sub_len, n_blk=n_blk):
            r = idx // n_blk
            q0 = (idx % n_blk) * DIL_TQ
            k0 = jnp.clip(q0 - A_HALF_WINDOW, 0, sub_len - DIL_WIN)
            rows_q = _strided_rows(r + dil * q0, DIL_TQ, dil)
            rows_k = _strided_rows(r + dil * k0, DIL_WIN, dil)
            return rows_q, rows_k, (q0 - k0) // A_HALF_WINDOW

        def scores_into(slot, idx, coords=coords):
            rows_q, rows_k, place = coords(idx)
            qp = q_ref[0, rows_q, :]
            kw = k_ref[0, rows_k, :].astype(BF16)
            q2 = jnp.concatenate([jnp.where(q_first, qp, 0.0), jnp.where(q_first, 0.0, qp)], axis=0).astype(BF16)
            s_scr[slot] = _dot_nt(q2, kw) + bias_scr[place]

        def finish(slot, idx, dil=dil, coords=coords):
            rows_q, rows_k, _ = coords(idx)
            vw = v_ref[0, rows_k, :].astype(BF16)
            m2 = jnp.max(s_scr[slot], axis=-1, keepdims=True)
            pr = jnp.exp2(s_scr[slot] - m2)
            den2 = jnp.sum(pr, axis=-1, keepdims=True)
            o2 = _dot(pr.astype(BF16), vw)
            pick = lambda a: jnp.where(low_head, a[:DIL_TQ], a[DIL_TQ:])
            o, m, den = pick(o2), pick(m2), pick(den2)
            if dil != DILATIONS[0]:
                o_prev, m_prev, den_prev = o_ref[0, rows_q, :], m_scr[rows_q, :], den_scr[rows_q, :]
                top = jnp.maximum(m_prev, m)
                w_prev, w_new = jnp.exp2(m_prev - top), jnp.exp2(m - top)
                o = w_prev * o_prev + w_new * o
                den = w_prev * den_prev + w_new * den
                m = top
            if dil != DILATIONS[-1]:
                o_ref[0, rows_q, :] = o
                m_scr[rows_q, :] = m
                den_scr[rows_q, :] = den
            else:
                o_ref[0, rows_q, :] = o / den

        assert n_total % DIL_STEPS == 0 and DIL_STEPS % DIL_SLOTS == 0
        for slot in range(DIL_SLOTS - 1):
            scores_into(slot, slot)

        def step(i, carry, scores_into=scores_into, finish=finish, n_total=n_total):
            for u in range(DIL_STEPS):
                idx = DIL_STEPS * i + u
                ahead = idx + DIL_SLOTS - 1
                scores_into((u + DIL_SLOTS - 1) % DIL_SLOTS, jnp.where(ahead >= n_total, ahead - n_total, ahead))
                finish(u % DIL_SLOTS, idx)
            return carry

        lax.fori_loop(0, n_total // DIL_STEPS, step, 0)


def _dilated(qa, ka, va):
    b, s, w = qa.shape
    spec = pl.BlockSpec((1, s, LANES), lambda bi, p: (bi, 0, p))
    out = pl.pallas_call(
        _dilated_kernel,
        grid=(b, w // LANES),
        in_specs=[spec, spec, spec],
        out_specs=spec,
        out_shape=jax.ShapeDtypeStruct((b, s, w), F32),
        scratch_shapes=[pltpu.VMEM((s, LANES), F32), pltpu.VMEM((s, LANES), F32),
                        pltpu.VMEM((DIL_SLOTS, 2 * DIL_TQ, DIL_WIN), F32),
                        pltpu.VMEM((3, 2 * DIL_TQ, DIL_WIN), F32)],
        compiler_params=_params(2),
        name="dilated",
    )(qa, ka, va)
    return out.reshape(b * s, w)


MLA_DEN_ROWS = 16
MLA_STEPS = 32
MLA_SLOTS = 16
MLA_AHEAD = 2


def _mla_kernel(q_ref, k_ref, vt_ref, o_ref, s_scr, smax_scr, *, tq, tk, steps):
    seq = k_ref.shape[1]
    heads = range(2)
    n_blocks = seq // tk
    n_tiles = seq // tq
    slots = s_scr.shape[0]
    tiles_per_iter = steps // n_blocks
    assert steps % n_blocks == 0 and n_tiles % tiles_per_iter == 0 and steps % slots == 0 and MLA_AHEAD < slots

    def scores_into(slot, tile, blk, hh):
        q0 = tile * tq if isinstance(tile, int) else pl.multiple_of(tile * tq, tq)
        cols = slice(hh * LANES, (hh + 1) * LANES)
        st = _dot_nt(k_ref[0, blk * tk:(blk + 1) * tk, cols], q_ref[0, pl.ds(q0, tq), cols])
        s_scr[slot, hh] = st
        smax_scr[slot, hh] = jnp.max(st.reshape(tk // SUBLANES, SUBLANES, tq), axis=0)

    def fresh():
        return [(jnp.full((1, tq), NEG_BIG, F32), jnp.zeros((B_V_DIM + MLA_DEN_ROWS, tq), F32)) for _ in heads]

    for u in range(MLA_AHEAD):
        for hh in heads:
            scores_into(u, u // n_blocks, u % n_blocks, hh)

    def iteration(it, carry):
        tile0 = it * tiles_per_iter
        state = fresh()
        for u in range(steps):
            tile, blk = tile0 + u // n_blocks, u % n_blocks
            v = u + MLA_AHEAD
            ahead_tile = jnp.minimum(tile0 + v // n_blocks, n_tiles - 1)
            for hh in heads:
                scores_into(v % slots, ahead_tile, v % n_blocks, hh)
                m, acc = state[hh]
                m_new = jnp.maximum(m, jnp.max(smax_scr[u % slots, hh], axis=0, keepdims=True))
                alpha = jnp.exp2(m - m_new)
                pt = jnp.exp2(s_scr[u % slots, hh] - m_new).astype(BF16)
                vt1 = jnp.concatenate([vt_ref[hh * B_V_DIM:(hh + 1) * B_V_DIM, blk * tk:(blk + 1) * tk],
                                       jnp.ones((MLA_DEN_ROWS, tk), BF16)], axis=0)
                acc = alpha * acc + _dot(vt1, pt)
                state[hh] = (m_new, acc)
            if blk == n_blocks - 1:
                ot = jnp.concatenate([acc[:B_V_DIM] / acc[B_V_DIM:B_V_DIM + 1] for _, acc in state], axis=0)
                o_ref[0, pl.ds(pl.multiple_of(tile * tq, tq), tq), :] = ot.T
                state = fresh()
        return carry

    lax.fori_loop(0, n_tiles // tiles_per_iter, iteration, 0)


def _mla(qb, kb, vbt, tq, tk):
    b, s, _ = qb.shape
    qk_spec = pl.BlockSpec((1, s, 2 * LANES), lambda bi, p: (bi, 0, p))
    n_blocks = s // tk
    steps = n_blocks * max(1, MLA_STEPS // n_blocks)
    slots = min(steps, MLA_SLOTS)
    out = pl.pallas_call(
        functools.partial(_mla_kernel, tq=tq, tk=tk, steps=steps),
        grid=(b, B_HEADS // 2),
        in_specs=[qk_spec, qk_spec, pl.BlockSpec((LANES, s), lambda bi, p: (p, bi))],
        out_specs=pl.BlockSpec((1, s, LANES), lambda bi, p: (bi, 0, p)),
        out_shape=jax.ShapeDtypeStruct((b, s, B_WIDTH), F32),
        scratch_shapes=[pltpu.VMEM((slots, 2, tk, tq), F32), pltpu.VMEM((slots, 2, SUBLANES, tq), F32)],
        compiler_params=_params(2),
        name="mla",
    )(qb, kb, vbt)
    return out.reshape(b * s, B_WIDTH)


def _mem_kv_kernel(mem_ref, g_ref, wkv_ref, gk_ref, k_ref, v_ref):
    mem = mem_ref[0]
    mm = (mem * _row_rms_scale(mem, mem.shape[-1]) * g_ref[...]).astype(BF16)
    kv = _dot(mm, wkv_ref[...])
    for hd in range(M_HEADS):
        cols = slice(hd * M_HEAD_DIM, (hd + 1) * M_HEAD_DIM)
        kh = kv[:, cols]
        k_ref[0, :, cols] = (kh * _row_rms_scale(kh, M_HEAD_DIM) * gk_ref[...]).astype(k_ref.dtype)
    v_ref[0] = kv[:, M_WIDTH:].astype(v_ref.dtype)


def _mem_kv(mem, w):
    b, n_mem, d = mem.shape
    const = lambda bi: (0, 0)
    full = lambda a: pl.BlockSpec(a.shape, const)
    blk = pl.BlockSpec((1, n_mem, M_WIDTH), lambda bi: (bi, 0, 0))
    return pl.pallas_call(
        _mem_kv_kernel,
        grid=(b,),
        in_specs=[pl.BlockSpec((1, n_mem, d), lambda bi: (bi, 0, 0)),
                  full(w['gmem_kv']), full(w['wkv_m']), full(w['gk_m'])],
        out_specs=[blk, blk],
        out_shape=[jax.ShapeDtypeStruct((b, n_mem, M_WIDTH), BF16)] * 2,
        compiler_params=_params(1),
        name="mem_kv",
    )(mem, w['gmem_kv'], w['wkv_m'], w['gk_m'])


FFN_CHUNK = 1024


def _post_kernel(x_ref, oa_ref, ob_ref, ga_ref, gb_ref, wa_ref, wb_ref,
                 gmem_ref, wq_ref, gq_ref, k_ref, v_ref, wo_ref, gffn_ref, w1_ref, w2_ref, y_ref):
    oa, ob = oa_ref[0], ob_ref[0]
    na = (oa * _row_rms_scale(oa, A_WIDTH) * ga_ref[...]).astype(BF16)
    nb = (ob * _row_rms_scale(ob, B_WIDTH) * gb_ref[...]).astype(BF16)
    x = x_ref[0] + _dot(na, wa_ref[...]) + _dot(nb, wb_ref[...])

    h = (x * _row_rms_scale(x, x.shape[-1]) * gmem_ref[...]).astype(BF16)
    q = _dot(h, wq_ref[...])
    heads = []
    for hd in range(M_HEADS):
        cols = slice(hd * M_HEAD_DIM, (hd + 1) * M_HEAD_DIM)
        qh = q[:, cols]
        qh = (qh * _row_rms_scale(qh, M_HEAD_DIM) * gq_ref[...]).astype(BF16)
        s = _dot_nt(qh, k_ref[0, :, cols])
        pr = jnp.exp(s - jnp.max(s, axis=-1, keepdims=True))
        pr = pr / jnp.sum(pr, axis=-1, keepdims=True)
        heads.append(_dot(pr.astype(BF16), v_ref[0, :, cols]).astype(BF16))
    x = x + _dot(jnp.concatenate(heads, axis=-1), wo_ref[...])

    h = (x * _row_rms_scale(x, x.shape[-1]) * gffn_ref[...]).astype(BF16)
    for c in range(w1_ref.shape[1] // FFN_CHUNK):
        cols = slice(c * FFN_CHUNK, (c + 1) * FFN_CHUNK)
        a = jnp.maximum(_dot(h, w1_ref[:, cols]), 0.0)
        x = x + _dot((a * a).astype(BF16), w2_ref[cols, :])
    y_ref[0] = x


def _post(x3d, oa, ob, km, vm, w, tm):
    b, s, d = x3d.shape
    n_mem = km.shape[1]
    const = lambda bi, i: (0, 0)
    full = lambda a: pl.BlockSpec(a.shape, const, pipeline_mode=pl.Buffered(1))
    tile = lambda width: pl.BlockSpec((1, tm, width), lambda bi, i: (bi, i, 0))
    kvblk = pl.BlockSpec((1, n_mem, M_WIDTH), lambda bi, i: (bi, 0, 0))
    names = ['ga_out', 'gb_out', 'wout_a', 'wout_b', 'gmem', 'wq_m', 'gq_m']
    tail = ['wo_m', 'gffn', 'w1', 'w2']
    return pl.pallas_call(
        _post_kernel,
        grid=(b, s // tm),
        in_specs=[tile(d), tile(A_WIDTH), tile(B_WIDTH)] + [full(w[n]) for n in names] + [kvblk, kvblk]
                 + [full(w[n]) for n in tail],
        out_specs=tile(d),
        out_shape=jax.ShapeDtypeStruct((b, s, d), F32),
        compiler_params=_params(2),
        name="post",
    )(x3d, oa, ob, *[w[n] for n in names], km, vm, *[w[n] for n in tail])


def _rope_tables(seq, rope_dim):
    half = rope_dim // 2
    assert ROPE_SPAN % half == 0
    inv = ROPE_THETA ** (-jnp.arange(0, rope_dim, 2, dtype=F32) / rope_dim)
    ang = jnp.arange(seq, dtype=F32)[:, None] * inv[None, :]
    cos, sin = jnp.cos(ang), jnp.sin(ang)
    lane = np.arange(LANES)
    is_lo = lane < ROPE_SPAN
    is_hi = (lane >= LANES // 2) & (lane < LANES // 2 + ROPE_SPAN)
    idx = (lane % (LANES // 2)) % half
    cos_t = jnp.where(is_lo | is_hi, cos[:, idx], 1.0)
    sin_t = jnp.where(is_lo, -sin[:, idx], jnp.where(is_hi, sin[:, idx], 0.0))
    return jnp.stack([cos_t, sin_t]).astype(F32)


def _layer_weights(p):
    n_layers, d, _ = p['w_in'].shape
    row = lambda v: v.reshape(n_layers, 1, -1).astype(F32)
    w_in = p['w_in']
    c_kr = 3 * A_WIDTH + Q_LORA + KV_LORA
    half_r = B_ROPE_DIM // 2
    zeros = lambda n: jnp.zeros((n_layers, d, n), F32)
    kr = w_in[..., c_kr:]
    win = jnp.concatenate([
        w_in[..., :A_WIDTH][..., _A_PERM_ALL], w_in[..., A_WIDTH:2 * A_WIDTH][..., _A_PERM_ALL],
        w_in[..., 2 * A_WIDTH:c_kr],
        kr[..., :half_r], zeros(LANES // 2 - half_r), kr[..., half_r:], zeros(LANES // 2 - half_r)], axis=-1)

    def b_lanes(a, with_rope):
        if with_rope:
            a = a[..., _B_PERM]
        else:
            z = jnp.zeros(a.shape[:-1] + (half_r,), a.dtype)
            a = jnp.concatenate([z, a[..., :_B_SPLIT], z, a[..., _B_SPLIT:]], axis=-1)
        return jnp.pad(a, [(0, 0)] * (a.ndim - 1) + [(0, LANES - B_QK_DIM)])

    wuq = p['w_uq'].reshape(n_layers, Q_LORA, B_HEADS, B_QK_DIM)
    wukv = p['w_ukv'].reshape(n_layers, KV_LORA, B_HEADS, B_NOPE_DIM + B_V_DIM)
    b_gain = lambda g: jnp.tile(b_lanes(g, True), (1, B_HEADS))
    a_gain = lambda g: jnp.tile(g, (1, A_HEADS))[..., _A_PERM_ALL]
    stacked = {
        'gmix': row(p['norm_mix_g']),
        'win': win.astype(BF16),
        'gqa': row(a_gain(p['a_q_norm_g']) * (A_HEAD_DIM ** -0.5 * LOG2_E)),
        'gka': row(a_gain(p['a_k_norm_g'])),
        'gcq': row(p['b_cq_norm_g']),
        'gckv': row(p['b_ckv_norm_g']),
        'wuq': b_lanes(wuq, True).reshape(n_layers, Q_LORA, B_QK_PAD).astype(BF16),
        'wuk': b_lanes(wukv[..., :B_NOPE_DIM], False).reshape(n_layers, KV_LORA, B_QK_PAD).astype(BF16),
        'wuv': wukv[..., B_NOPE_DIM:].reshape(n_layers, KV_LORA, B_WIDTH).astype(BF16),
        'gqb': row(b_gain(p['b_q_norm_g']) * (B_QK_DIM ** -0.5 * LOG2_E)),
        'gkb': row(b_gain(p['b_k_norm_g'])),
        'ga_out': row(p['a_out_norm_g']),
        'gb_out': row(p['b_out_norm_g']),
        'wout_a': p['w_out'][:, :A_WIDTH].astype(BF16),
        'wout_b': p['w_out'][:, A_WIDTH:].astype(BF16),
        'gmem': row(p['norm_mem_g']),
        'gmem_kv': row(p['mem_kv_norm_g']),
        'wq_m': p['m_wq'].astype(BF16),
        'wkv_m': p['m_wkv'].astype(BF16),
        'gq_m': row(p['m_q_norm_g'] * (M_HEAD_DIM ** -0.5)),
        'gk_m': row(p['m_k_norm_g']),
        'wo_m': p['m_wo'].astype(BF16),
        'gffn': row(p['norm_ffn_g']),
        'w1': p['w_ff1'].astype(BF16),
        'w2': p['w_ff2'].astype(BF16),
    }
    return [{name: a[l] for name, a in stacked.items()} for l in range(n_layers)]


def _tiles(seq):
    tm = 512 if seq % 512 == 0 else 256
    return dict(tm=tm, tq=256, tk=512)


def _trunk(x, mem, layers):
    b, s, d = x.shape
    assert s % (max(DILATIONS) * DIL_TQ) == 0 and s // max(DILATIONS) >= DIL_WIN
    t = _tiles(s)
    taba = _rope_tables(s, A_ROPE_DIM)
    tabb = _rope_tables(s, B_ROPE_DIM)
    head_ones = jnp.asarray(np.kron(np.eye(2), np.ones((LANES, LANES))), BF16)
    same_head = (_A_PERM[:, None] < A_HEAD_DIM) == (_A_PERM[None, :] < A_HEAD_DIM)
    pair_ones = jnp.asarray(np.kron(np.eye(2), same_head), BF16)
    x2d = x.reshape(b * s, d)
    for w in layers:
        w = dict(w, taba=taba, tabb=tabb, head_ones=head_ones, pair_ones=pair_ones)
        qa, ka, va, qb, kb, vbt = _proj(x2d, s, w, t['tm'])
        as3d = lambda a: a.reshape(b, s, a.shape[-1])
        oa = _dilated(as3d(qa), as3d(ka), as3d(va))
        ob = _mla(as3d(qb), as3d(kb), vbt, t['tq'], t['tk'])
        km, vm = _mem_kv(mem, w)
        x2d = _post(as3d(x2d), as3d(oa), as3d(ob), km, vm, w, t['tm']).reshape(b * s, d)
    return x2d.reshape(b, s, d)


def kernel(x_prompt, x_sample, mem_prompt, mem_sample, norm_mix_g, w_in, a_q_norm_g, a_k_norm_g, b_cq_norm_g, b_ckv_norm_g, w_uq, w_ukv, b_q_norm_g, b_k_norm_g, a_out_norm_g, b_out_norm_g, w_out, norm_mem_g, mem_kv_norm_g, m_wq, m_wkv, m_q_norm_g, m_k_norm_g, m_wo, norm_ffn_g, w_ff1, w_ff2):
    p = dict(norm_mix_g=norm_mix_g, w_in=w_in, a_q_norm_g=a_q_norm_g, a_k_norm_g=a_k_norm_g,
             b_cq_norm_g=b_cq_norm_g, b_ckv_norm_g=b_ckv_norm_g, w_uq=w_uq, w_ukv=w_ukv,
             b_q_norm_g=b_q_norm_g, b_k_norm_g=b_k_norm_g, a_out_norm_g=a_out_norm_g,
             b_out_norm_g=b_out_norm_g, w_out=w_out, norm_mem_g=norm_mem_g, mem_kv_norm_g=mem_kv_norm_g,
             m_wq=m_wq, m_wkv=m_wkv, m_q_norm_g=m_q_norm_g, m_k_norm_g=m_k_norm_g, m_wo=m_wo,
             norm_ffn_g=norm_ffn_g, w_ff1=w_ff1, w_ff2=w_ff2)
    layers = _layer_weights(p)
    return (_trunk(x_prompt, mem_prompt, layers), _trunk(x_sample, mem_sample, layers))
```

```python
import functools

import numpy as np
import jax
import jax.numpy as jnp
from jax import lax
from jax.experimental import pallas as pl
from jax.experimental.pallas import tpu as pltpu

EPS = 1e-6
ROPE_THETA = 500000.0

LANES = 128
SUBLANES = 8
A_HEADS = 8
A_HEAD_DIM = 64
A_WIDTH = A_HEADS * A_HEAD_DIM
A_ROPE_DIM = 16
A_HALF_WINDOW = 64
DILATIONS = (16, 4, 1)

B_HEADS = 8
B_NOPE_DIM = 64
B_ROPE_DIM = 32
B_QK_DIM = B_NOPE_DIM + B_ROPE_DIM
B_V_DIM = 64
B_WIDTH = B_HEADS * B_V_DIM
B_QK_PAD = B_HEADS * LANES
Q_LORA = 256
KV_LORA = 128

M_HEADS = 4
M_HEAD_DIM = 128
M_WIDTH = M_HEADS * M_HEAD_DIM

VMEM_LIMIT = 56 * 1024 * 1024

BF16 = jnp.bfloat16
F32 = jnp.float32
NEG_BIG = -1e30
LOG2_E = float(np.log2(np.e))


def _a_pair_perm():
    heads = [np.arange(A_HEAD_DIM), A_HEAD_DIM + np.arange(A_HEAD_DIM)]
    r = A_ROPE_DIM // 2
    rest = (A_HEAD_DIM - A_ROPE_DIM) // 2
    first = [h[:r] for h in heads] + [h[A_ROPE_DIM:A_ROPE_DIM + rest] for h in heads]
    second = [h[r:A_ROPE_DIM] for h in heads] + [h[A_ROPE_DIM + rest:] for h in heads]
    return np.concatenate(first + second)


ROPE_SPAN = 16
_A_PERM = _a_pair_perm()
_A_PERM_ALL = np.concatenate([c * LANES + _A_PERM for c in range(A_WIDTH // LANES)])
_B_SPLIT = LANES // 2 - B_ROPE_DIM // 2
_B_PERM = np.concatenate([np.arange(B_NOPE_DIM, B_NOPE_DIM + B_ROPE_DIM // 2), np.arange(0, _B_SPLIT),
                          np.arange(B_NOPE_DIM + B_ROPE_DIM // 2, B_QK_DIM), np.arange(_B_SPLIT, B_NOPE_DIM)])


def _lane_mask(member):
    lane = lax.broadcasted_iota(jnp.int32, (1, LANES), 1)
    edges = np.flatnonzero(np.diff(np.concatenate([[0], member.astype(np.int8), [0]])))
    mask = None
    for lo, hi in zip(edges[::2], edges[1::2]):
        run = (lane >= int(lo)) & (lane < int(hi))
        mask = run if mask is None else (mask | run)
    return mask


def _params(n_axes):
    return pltpu.CompilerParams(dimension_semantics=("arbitrary",) * n_axes,
                                vmem_limit_bytes=VMEM_LIMIT)


def _row_rms_scale(x, width):
    return lax.rsqrt(jnp.sum(x * x, axis=-1, keepdims=True) * (1.0 / width) + EPS)


def _dot(a, b):
    return jnp.dot(a, b, preferred_element_type=F32)


def _dot_nt(a, b):
    return lax.dot_general(a, b, (((1,), (1,)), ((), ())), preferred_element_type=F32)


def _rope_chunk(xc, cos, sin):
    return xc * cos + pltpu.roll(xc, LANES // 2, axis=1) * sin


def _proj_kernel(x_ref, gmix_ref, win_ref, gqa_ref, gka_ref, gcq_ref, gckv_ref,
                 wuq_ref, wuk_ref, wuv_ref, gqb_ref, gkb_ref, aones_ref, ones_ref, taba_ref, tabb_ref,
                 qa_ref, ka_ref, va_ref, qb_ref, kb_ref, vbt_ref):
    x = x_ref[...]
    h = (x * _row_rms_scale(x, x.shape[-1]) * gmix_ref[...]).astype(BF16)
    proj = _dot(h, win_ref[...])

    cos_a, sin_a = taba_ref[0], taba_ref[1]
    cos_b, sin_b = tabb_ref[0], tabb_ref[1]

    def a_heads(col0, g_ref, out_ref):
        for c2 in range(A_WIDTH // (2 * LANES)):
            xs = proj[:, col0 + 2 * c2 * LANES: col0 + 2 * (c2 + 1) * LANES]
            ssq = _dot((xs * xs).astype(BF16), aones_ref[...])
            for i in range(2):
                c = 2 * c2 + i
                lanes = slice(i * LANES, (i + 1) * LANES)
                rs = lax.rsqrt(ssq[:, lanes] * (1.0 / A_HEAD_DIM) + EPS)
                y = xs[:, lanes] * rs * g_ref[:, c * LANES:(c + 1) * LANES]
                y = _rope_chunk(y, cos_a, sin_a)
                out_ref[:, c * LANES:(c + 1) * LANES] = y.astype(out_ref.dtype)

    a_heads(0, gqa_ref, qa_ref)
    a_heads(A_WIDTH, gka_ref, ka_ref)
    va_ref[...] = proj[:, 2 * A_WIDTH:3 * A_WIDTH].astype(va_ref.dtype)

    c0 = 3 * A_WIDTH
    cq = proj[:, c0:c0 + Q_LORA]
    ckv = proj[:, c0 + Q_LORA:c0 + Q_LORA + KV_LORA]
    kr = proj[:, c0 + Q_LORA + KV_LORA:c0 + Q_LORA + 2 * KV_LORA]
    cqn = (cq * _row_rms_scale(cq, Q_LORA) * gcq_ref[...]).astype(BF16)
    ckvn = (ckv * _row_rms_scale(ckv, KV_LORA) * gckv_ref[...]).astype(BF16)
    qb = _dot(cqn, wuq_ref[...])
    kn = _dot(ckvn, wuk_ref[...])
    vbt_ref[...] = _dot(ckvn, wuv_ref[...]).T.astype(vbt_ref.dtype)

    def b_heads(get_chunk, g_ref, out_ref):
        for c2 in range(B_HEADS // 2):
            xs = [get_chunk(2 * c2 + i) for i in range(2)]
            ssq = _dot(jnp.concatenate([xc * xc for xc in xs], axis=1).astype(BF16), ones_ref[...])
            for i, xc in enumerate(xs):
                c = 2 * c2 + i
                rs = lax.rsqrt(ssq[:, i * LANES:(i + 1) * LANES] * (1.0 / B_QK_DIM) + EPS)
                y = xc * rs * g_ref[:, c * LANES:(c + 1) * LANES]
                y = _rope_chunk(y, cos_b, sin_b)
                out_ref[:, c * LANES:(c + 1) * LANES] = y.astype(out_ref.dtype)

    b_heads(lambda c: qb[:, c * LANES:(c + 1) * LANES], gqb_ref, qb_ref)
    b_heads(lambda c: kn[:, c * LANES:(c + 1) * LANES] + kr, gkb_ref, kb_ref)


def _proj(x2d, seq, w, tm):
    n, d = x2d.shape
    nblk_seq = seq // tm
    row = lambda i: (i, 0)
    const = lambda i: (0, 0)
    tab = lambda i: (0, i % nblk_seq, 0)
    full = lambda a: pl.BlockSpec(a.shape, const)
    widths = [A_WIDTH, A_WIDTH, A_WIDTH, B_QK_PAD, B_QK_PAD]
    return pl.pallas_call(
        _proj_kernel,
        grid=(n // tm,),
        in_specs=[pl.BlockSpec((tm, d), row), full(w['gmix']), full(w['win']), full(w['gqa']), full(w['gka']),
                  full(w['gcq']), full(w['gckv']), full(w['wuq']), full(w['wuk']), full(w['wuv']),
                  full(w['gqb']), full(w['gkb']), full(w['pair_ones']), full(w['head_ones']),
                  pl.BlockSpec((2, tm, LANES), tab), pl.BlockSpec((2, tm, LANES), tab)],
        out_specs=[pl.BlockSpec((tm, wd), row) for wd in widths]
                  + [pl.BlockSpec((B_WIDTH, tm), lambda i: (0, i))],
        out_shape=[jax.ShapeDtypeStruct((n, wd), F32 if wd == A_WIDTH else BF16) for wd in widths]
                  + [jax.ShapeDtypeStruct((B_WIDTH, n), BF16)],
        compiler_params=_params(1),
        name="proj",
    )(x2d, w['gmix'], w['win'], w['gqa'], w['gka'], w['gcq'], w['gckv'], w['wuq'], w['wuk'], w['wuv'],
      w['gqb'], w['gkb'], w['pair_ones'], w['head_ones'], w['taba'], w['tabb'])


DIL_TQ = 128
DIL_WIN = DIL_TQ + 2 * A_HALF_WINDOW
DIL_SLOTS = 4
DIL_STEPS = 16


def _strided_rows(start, size, stride):
    return pl.ds(start, size) if stride == 1 else pl.ds(start, size, stride=stride)


def _dilated_kernel(q_ref, k_ref, v_ref, o_ref, m_scr, den_scr, s_scr, bias_scr):
    seq = q_ref.shape[1]
    lane = lax.broadcasted_iota(jnp.int32, (1, LANES), 1)
    low_head = lane < A_HEAD_DIM
    q_first = _lane_mask(_A_PERM < A_HEAD_DIM)
    rel = (lax.broadcasted_iota(jnp.int32, (2 * DIL_TQ, DIL_WIN), 1)
           - lax.broadcasted_iota(jnp.int32, (2 * DIL_TQ, DIL_WIN), 0) % DIL_TQ)
    for place in range(3):
        bias_scr[place] = jnp.where(jnp.abs(rel - place * A_HALF_WINDOW) <= A_HALF_WINDOW, 0.0, NEG_BIG)

    for dil in DILATIONS:
        sub_len = seq // dil
        n_blk = sub_len // DIL_TQ
        n_total = dil * n_blk

        def coords(idx, dil=dil, sub_len=sub_len, n_blk=n_blk):
            r = idx // n_blk
            q0 = (idx % n_blk) * DIL_TQ
            k0 = jnp.clip(q0 - A_HALF_WINDOW, 0, sub_len - DIL_WIN)
            rows_q = _strided_rows(r + dil * q0, DIL_TQ, dil)
            rows_k = _strided_rows(r + dil * k0, DIL_WIN, dil)
            return rows_q, rows_k, (q0 - k0) // A_HALF_WINDOW

        def scores_into(slot, idx, coords=coords):
            rows_q, rows_k, place = coords(idx)
            qp = q_ref[0, rows_q, :]
            kw = k_ref[0, rows_k, :].astype(BF16)
            q2 = jnp.concatenate([jnp.where(q_first, qp, 0.0), jnp.where(q_first, 0.0, qp)], axis=0).astype(BF16)
            s_scr[slot] = _dot_nt(q2, kw) + bias_scr[place]

        def finish(slot, idx, dil=dil, coords=coords):
            rows_q, rows_k, _ = coords(idx)
            vw = v_ref[0, rows_k, :].astype(BF16)
            m2 = jnp.max(s_scr[slot], axis=-1, keepdims=True)
            pr = jnp.exp2(s_scr[slot] - m2)
            den2 = jnp.sum(pr, axis=-1, keepdims=True)
            o2 = _dot(pr.astype(BF16), vw)
            pick = lambda a: jnp.where(low_head, a[:DIL_TQ], a[DIL_TQ:])
            o, m, den = pick(o2), pick(m2), pick(den2)
            if dil != DILATIONS[0]:
                o_prev, m_prev, den_prev = o_ref[0, rows_q, :], m_scr[rows_q, :], den_scr[rows_q, :]
                top = jnp.maximum(m_prev, m)
                w_prev, w_new = jnp.exp2(m_prev - top), jnp.exp2(m - top)
                o = w_prev * o_prev + w_new * o
                den = w_prev * den_prev + w_new * den
                m = top
            if dil != DILATIONS[-1]:
                o_ref[0, rows_q, :] = o
                m_scr[rows_q, :] = m
                den_scr[rows_q, :] = den
            else:
                o_ref[0, rows_q, :] = o / den

        assert n_total % DIL_STEPS == 0 and DIL_STEPS % DIL_SLOTS == 0
        for slot in range(DIL_SLOTS - 1):
            scores_into(slot, slot)

        def step(i, carry, scores_into=scores_into, finish=finish, n_total=n_total):
            for u in range(DIL_STEPS):
                idx = DIL_STEPS * i + u
                ahead = idx + DIL_SLOTS - 1
                scores_into((u + DIL_SLOTS - 1) % DIL_SLOTS, jnp.where(ahead >= n_total, ahead - n_total, ahead))
                finish(u % DIL_SLOTS, idx)
            return carry

        lax.fori_loop(0, n_total // DIL_STEPS, step, 0)


def _dilated(qa, ka, va):
    b, s, w = qa.shape
    spec = pl.BlockSpec((1, s, LANES), lambda bi, p: (bi, 0, p))
    out = pl.pallas_call(
        _dilated_kernel,
        grid=(b, w // LANES),
        in_specs=[spec, spec, spec],
        out_specs=spec,
        out_shape=jax.ShapeDtypeStruct((b, s, w), F32),
        scratch_shapes=[pltpu.VMEM((s, LANES), F32), pltpu.VMEM((s, LANES), F32),
                        pltpu.VMEM((DIL_SLOTS, 2 * DIL_TQ, DIL_WIN), F32),
                        pltpu.VMEM((3, 2 * DIL_TQ, DIL_WIN), F32)],
        compiler_params=_params(2),
        name="dilated",
    )(qa, ka, va)
    return out.reshape(b * s, w)


MLA_DEN_ROWS = 16
MLA_STEPS = 64
MLA_SLOTS = 16
MLA_AHEAD = 2


def _mla_kernel(q_ref, k_ref, vt_ref, o_ref, s_scr, smax_scr, *, tq, tk, steps):
    seq = k_ref.shape[1]
    heads = range(2)
    n_blocks = seq // tk
    n_tiles = seq // tq
    slots = s_scr.shape[0]
    tiles_per_iter = steps // n_blocks
    assert steps % n_blocks == 0 and n_tiles % tiles_per_iter == 0 and steps % slots == 0 and MLA_AHEAD < slots

    def scores_into(slot, tile, blk, hh):
        q0 = tile * tq if isinstance(tile, int) else pl.multiple_of(tile * tq, tq)
        cols = slice(hh * LANES, (hh + 1) * LANES)
        st = _dot_nt(k_ref[0, blk * tk:(blk + 1) * tk, cols], q_ref[0, pl.ds(q0, tq), cols])
        s_scr[slot, hh] = st
        smax_scr[slot, hh] = jnp.max(st.reshape(tk // SUBLANES, SUBLANES, tq), axis=0)

    def fresh():
        return [(jnp.full((1, tq), NEG_BIG, F32), jnp.zeros((B_V_DIM + MLA_DEN_ROWS, tq), F32)) for _ in heads]

    for u in range(MLA_AHEAD):
        for hh in heads:
            scores_into(u, u // n_blocks, u % n_blocks, hh)

    def iteration(it, carry):
        tile0 = it * tiles_per_iter
        state = fresh()
        for u in range(steps):
            tile, blk = tile0 + u // n_blocks, u % n_blocks
            v = u + MLA_AHEAD
            ahead_tile = jnp.minimum(tile0 + v // n_blocks, n_tiles - 1)
            for hh in heads:
                scores_into(v % slots, ahead_tile, v % n_blocks, hh)
                m, acc = state[hh]
                m_new = jnp.maximum(m, jnp.max(smax_scr[u % slots, hh], axis=0, keepdims=True))
                alpha = jnp.exp2(m - m_new)
                pt = jnp.exp2(s_scr[u % slots, hh] - m_new).astype(BF16)
                vt1 = jnp.concatenate([vt_ref[hh * B_V_DIM:(hh + 1) * B_V_DIM, blk * tk:(blk + 1) * tk],
                                       jnp.ones((MLA_DEN_ROWS, tk), BF16)], axis=0)
                acc = alpha * acc + _dot(vt1, pt)
                state[hh] = (m_new, acc)
            if blk == n_blocks - 1:
                ot = jnp.concatenate([acc[:B_V_DIM] / acc[B_V_DIM:B_V_DIM + 1] for _, acc in state], axis=0)
                o_ref[0, pl.ds(pl.multiple_of(tile * tq, tq), tq), :] = ot.T
                state = fresh()
        return carry

    lax.fori_loop(0, n_tiles // tiles_per_iter, iteration, 0)


def _mla(qb, kb, vbt, tq, tk):
    b, s, _ = qb.shape
    qk_spec = pl.BlockSpec((1, s, 2 * LANES), lambda bi, p: (bi, 0, p))
    n_blocks = s // tk
    steps = n_blocks * max(1, MLA_STEPS // n_blocks)
    slots = min(steps, MLA_SLOTS)
    out = pl.pallas_call(
        functools.partial(_mla_kernel, tq=tq, tk=tk, steps=steps),
        grid=(b, B_HEADS // 2),
        in_specs=[qk_spec, qk_spec, pl.BlockSpec((LANES, s), lambda bi, p: (p, bi))],
        out_specs=pl.BlockSpec((1, s, LANES), lambda bi, p: (bi, 0, p)),
        out_shape=jax.ShapeDtypeStruct((b, s, B_WIDTH), F32),
        scratch_shapes=[pltpu.VMEM((slots, 2, tk, tq), F32), pltpu.VMEM((slots, 2, SUBLANES, tq), F32)],
        compiler_params=_params(2),
        name="mla",
    )(qb, kb, vbt)
    return out.reshape(b * s, B_WIDTH)


def _mem_kv_kernel(mem_ref, g_ref, wkv_ref, gk_ref, k_ref, v_ref):
    mem = mem_ref[0]
    mm = (mem * _row_rms_scale(mem, mem.shape[-1]) * g_ref[...]).astype(BF16)
    kv = _dot(mm, wkv_ref[...])
    for hd in range(M_HEADS):
        cols = slice(hd * M_HEAD_DIM, (hd + 1) * M_HEAD_DIM)
        kh = kv[:, cols]
        k_ref[0, :, cols] = (kh * _row_rms_scale(kh, M_HEAD_DIM) * gk_ref[...]).astype(k_ref.dtype)
    v_ref[0] = kv[:, M_WIDTH:].astype(v_ref.dtype)


def _mem_kv(mem, w):
    b, n_mem, d = mem.shape
    const = lambda bi: (0, 0)
    full = lambda a: pl.BlockSpec(a.shape, const)
    blk = pl.BlockSpec((1, n_mem, M_WIDTH), lambda bi: (bi, 0, 0))
    return pl.pallas_call(
        _mem_kv_kernel,
        grid=(b,),
        in_specs=[pl.BlockSpec((1, n_mem, d), lambda bi: (bi, 0, 0)),
                  full(w['gmem_kv']), full(w['wkv_m']), full(w['gk_m'])],
        out_specs=[blk, blk],
        out_shape=[jax.ShapeDtypeStruct((b, n_mem, M_WIDTH), BF16)] * 2,
        compiler_params=_params(1),
        name="mem_kv",
    )(mem, w['gmem_kv'], w['wkv_m'], w['gk_m'])


FFN_CHUNK = 1024


def _post_kernel(x_ref, oa_ref, ob_ref, ga_ref, gb_ref, wa_ref, wb_ref,
                 gmem_ref, wq_ref, gq_ref, k_ref, v_ref, wo_ref, gffn_ref, w1_ref, w2_ref, y_ref):
    oa, ob = oa_ref[0], ob_ref[0]
    na = (oa * _row_rms_scale(oa, A_WIDTH) * ga_ref[...]).astype(BF16)
    nb = (ob * _row_rms_scale(ob, B_WIDTH) * gb_ref[...]).astype(BF16)
    x = x_ref[0] + _dot(na, wa_ref[...]) + _dot(nb, wb_ref[...])

    h = (x * _row_rms_scale(x, x.shape[-1]) * gmem_ref[...]).astype(BF16)
    q = _dot(h, wq_ref[...])
    heads = []
    for hd in range(M_HEADS):
        cols = slice(hd * M_HEAD_DIM, (hd + 1) * M_HEAD_DIM)
        qh = q[:, cols]
        qh = (qh * _row_rms_scale(qh, M_HEAD_DIM) * gq_ref[...]).astype(BF16)
        s = _dot_nt(qh, k_ref[0, :, cols])
        pr = jnp.exp(s - jnp.max(s, axis=-1, keepdims=True))
        pr = pr / jnp.sum(pr, axis=-1, keepdims=True)
        heads.append(_dot(pr.astype(BF16), v_ref[0, :, cols]).astype(BF16))
    x = x + _dot(jnp.concatenate(heads, axis=-1), wo_ref[...])

    h = (x * _row_rms_scale(x, x.shape[-1]) * gffn_ref[...]).astype(BF16)
    for c in range(w1_ref.shape[1] // FFN_CHUNK):
        cols = slice(c * FFN_CHUNK, (c + 1) * FFN_CHUNK)
        a = jnp.maximum(_dot(h, w1_ref[:, cols]), 0.0)
        x = x + _dot((a * a).astype(BF16), w2_ref[cols, :])
    y_ref[0] = x


def _post(x3d, oa, ob, km, vm, w, tm):
    b, s, d = x3d.shape
    n_mem = km.shape[1]
    const = lambda bi, i: (0, 0)
    full = lambda a: pl.BlockSpec(a.shape, const, pipeline_mode=pl.Buffered(1))
    tile = lambda width: pl.BlockSpec((1, tm, width), lambda bi, i: (bi, i, 0))
    kvblk = pl.BlockSpec((1, n_mem, M_WIDTH), lambda bi, i: (bi, 0, 0))
    names = ['ga_out', 'gb_out', 'wout_a', 'wout_b', 'gmem', 'wq_m', 'gq_m']
    tail = ['wo_m', 'gffn', 'w1', 'w2']
    return pl.pallas_call(
        _post_kernel,
        grid=(b, s // tm),
        in_specs=[tile(d), tile(A_WIDTH), tile(B_WIDTH)] + [full(w[n]) for n in names] + [kvblk, kvblk]
                 + [full(w[n]) for n in tail],
        out_specs=tile(d),
        out_shape=jax.ShapeDtypeStruct((b, s, d), F32),
        compiler_params=_params(2),
        name="post",
    )(x3d, oa, ob, *[w[n] for n in names], km, vm, *[w[n] for n in tail])


def _rope_tables(seq, rope_dim):
    half = rope_dim // 2
    assert ROPE_SPAN % half == 0
    inv = ROPE_THETA ** (-jnp.arange(0, rope_dim, 2, dtype=F32) / rope_dim)
    ang = jnp.arange(seq, dtype=F32)[:, None] * inv[None, :]
    cos, sin = jnp.cos(ang), jnp.sin(ang)
    lane = np.arange(LANES)
    is_lo = lane < ROPE_SPAN
    is_hi = (lane >= LANES // 2) & (lane < LANES // 2 + ROPE_SPAN)
    idx = (lane % (LANES // 2)) % half
    cos_t = jnp.where(is_lo | is_hi, cos[:, idx], 1.0)
    sin_t = jnp.where(is_lo, -sin[:, idx], jnp.where(is_hi, sin[:, idx], 0.0))
    return jnp.stack([cos_t, sin_t]).astype(F32)


def _layer_weights(p):
    n_layers, d, _ = p['w_in'].shape
    row = lambda v: v.reshape(n_layers, 1, -1).astype(F32)
    w_in = p['w_in']
    c_kr = 3 * A_WIDTH + Q_LORA + KV_LORA
    half_r = B_ROPE_DIM // 2
    zeros = lambda n: jnp.zeros((n_layers, d, n), F32)
    kr = w_in[..., c_kr:]
    win = jnp.concatenate([
        w_in[..., :A_WIDTH][..., _A_PERM_ALL], w_in[..., A_WIDTH:2 * A_WIDTH][..., _A_PERM_ALL],
        w_in[..., 2 * A_WIDTH:c_kr],
        kr[..., :half_r], zeros(LANES // 2 - half_r), kr[..., half_r:], zeros(LANES // 2 - half_r)], axis=-1)

    def b_lanes(a, with_rope):
        if with_rope:
            a = a[..., _B_PERM]
        else:
            z = jnp.zeros(a.shape[:-1] + (half_r,), a.dtype)
            a = jnp.concatenate([z, a[..., :_B_SPLIT], z, a[..., _B_SPLIT:]], axis=-1)
        return jnp.pad(a, [(0, 0)] * (a.ndim - 1) + [(0, LANES - B_QK_DIM)])

    wuq = p['w_uq'].reshape(n_layers, Q_LORA, B_HEADS, B_QK_DIM)
    wukv = p['w_ukv'].reshape(n_layers, KV_LORA, B_HEADS, B_NOPE_DIM + B_V_DIM)
    b_gain = lambda g: jnp.tile(b_lanes(g, True), (1, B_HEADS))
    a_gain = lambda g: jnp.tile(g, (1, A_HEADS))[..., _A_PERM_ALL]
    stacked = {
        'gmix': row(p['norm_mix_g']),
        'win': win.astype(BF16),
        'gqa': row(a_gain(p['a_q_norm_g']) * (A_HEAD_DIM ** -0.5 * LOG2_E)),
        'gka': row(a_gain(p['a_k_norm_g'])),
        'gcq': row(p['b_cq_norm_g']),
        'gckv': row(p['b_ckv_norm_g']),
        'wuq': b_lanes(wuq, True).reshape(n_layers, Q_LORA, B_QK_PAD).astype(BF16),
        'wuk': b_lanes(wukv[..., :B_NOPE_DIM], False).reshape(n_layers, KV_LORA, B_QK_PAD).astype(BF16),
        'wuv': wukv[..., B_NOPE_DIM:].reshape(n_layers, KV_LORA, B_WIDTH).astype(BF16),
        'gqb': row(b_gain(p['b_q_norm_g']) * (B_QK_DIM ** -0.5 * LOG2_E)),
        'gkb': row(b_gain(p['b_k_norm_g'])),
        'ga_out': row(p['a_out_norm_g']),
        'gb_out': row(p['b_out_norm_g']),
        'wout_a': p['w_out'][:, :A_WIDTH].astype(BF16),
        'wout_b': p['w_out'][:, A_WIDTH:].astype(BF16),
        'gmem': row(p['norm_mem_g']),
        'gmem_kv': row(p['mem_kv_norm_g']),
        'wq_m': p['m_wq'].astype(BF16),
        'wkv_m': p['m_wkv'].astype(BF16),
        'gq_m': row(p['m_q_norm_g'] * (M_HEAD_DIM ** -0.5)),
        'gk_m': row(p['m_k_norm_g']),
        'wo_m': p['m_wo'].astype(BF16),
        'gffn': row(p['norm_ffn_g']),
        'w1': p['w_ff1'].astype(BF16),
        'w2': p['w_ff2'].astype(BF16),
    }
    return [{name: a[l] for name, a in stacked.items()} for l in range(n_layers)]


def _tiles(seq):
    tm = 512 if seq % 512 == 0 else 256
    return dict(tm=tm, tq=256, tk=512)


def _trunk(x, mem, layers):
    b, s, d = x.shape
    assert s % (max(DILATIONS) * DIL_TQ) == 0 and s // max(DILATIONS) >= DIL_WIN
    t = _tiles(s)
    taba = _rope_tables(s, A_ROPE_DIM)
    tabb = _rope_tables(s, B_ROPE_DIM)
    head_ones = jnp.asarray(np.kron(np.eye(2), np.ones((LANES, LANES))), BF16)
    same_head = (_A_PERM[:, None] < A_HEAD_DIM) == (_A_PERM[None, :] < A_HEAD_DIM)
    pair_ones = jnp.asarray(np.kron(np.eye(2), same_head), BF16)
    x2d = x.reshape(b * s, d)
    for w in layers:
        w = dict(w, taba=taba, tabb=tabb, head_ones=head_ones, pair_ones=pair_ones)
        qa, ka, va, qb, kb, vbt = _proj(x2d, s, w, t['tm'])
        as3d = lambda a: a.reshape(b, s, a.shape[-1])
        oa = _dilated(as3d(qa), as3d(ka), as3d(va))
        ob = _mla(as3d(qb), as3d(kb), vbt, t['tq'], t['tk'])
        km, vm = _mem_kv(mem, w)
        x2d = _post(as3d(x2d), as3d(oa), as3d(ob), km, vm, w, t['tm']).reshape(b * s, d)
    return x2d.reshape(b, s, d)


def kernel(x_prompt, x_sample, mem_prompt, mem_sample, norm_mix_g, w_in, a_q_norm_g, a_k_norm_g, b_cq_norm_g, b_ckv_norm_g, w_uq, w_ukv, b_q_norm_g, b_k_norm_g, a_out_norm_g, b_out_norm_g, w_out, norm_mem_g, mem_kv_norm_g, m_wq, m_wkv, m_q_norm_g, m_k_norm_g, m_wo, norm_ffn_g, w_ff1, w_ff2):
    p = dict(norm_mix_g=norm_mix_g, w_in=w_in, a_q_norm_g=a_q_norm_g, a_k_norm_g=a_k_norm_g,
             b_cq_norm_g=b_cq_norm_g, b_ckv_norm_g=b_ckv_norm_g, w_uq=w_uq, w_ukv=w_ukv,
             b_q_norm_g=b_q_norm_g, b_k_norm_g=b_k_norm_g, a_out_norm_g=a_out_norm_g,
             b_out_norm_g=b_out_norm_g, w_out=w_out, norm_mem_g=norm_mem_g, mem_kv_norm_g=mem_kv_norm_g,
             m_wq=m_wq, m_wkv=m_wkv, m_q_norm_g=m_q_norm_g, m_k_norm_g=m_k_norm_g, m_wo=m_wo,
             norm_ffn_g=norm_ffn_g, w_ff1=w_ff1, w_ff2=w_ff2)
    layers = _layer_weights(p)
    return (_trunk(x_prompt, mem_prompt, layers), _trunk(x_sample, mem_sample, layers))
```

```python
import functools

import numpy as np
import jax
import jax.numpy as jnp
from jax import lax
from jax.experimental import pallas as pl
from jax.experimental.pallas import tpu as pltpu

EPS = 1e-6
ROPE_THETA = 500000.0

LANES = 128
SUBLANES = 8
A_HEADS = 8
A_HEAD_DIM = 64
A_WIDTH = A_HEADS * A_HEAD_DIM
A_ROPE_DIM = 16
A_HALF_WINDOW = 64
DILATIONS = (16, 4, 1)

B_HEADS = 8
B_NOPE_DIM = 64
B_ROPE_DIM = 32
B_QK_DIM = B_NOPE_DIM + B_ROPE_DIM
B_V_DIM = 64
B_WIDTH = B_HEADS * B_V_DIM
B_QK_PAD = B_HEADS * LANES
Q_LORA = 256
KV_LORA = 128

M_HEADS = 4
M_HEAD_DIM = 128
M_WIDTH = M_HEADS * M_HEAD_DIM

VMEM_LIMIT = 56 * 1024 * 1024

BF16 = jnp.bfloat16
F32 = jnp.float32
NEG_BIG = -1e30
LOG2_E = float(np.log2(np.e))


def _a_pair_perm():
    heads = [np.arange(A_HEAD_DIM), A_HEAD_DIM + np.arange(A_HEAD_DIM)]
    r = A_ROPE_DIM // 2
    rest = (A_HEAD_DIM - A_ROPE_DIM) // 2
    first = [h[:r] for h in heads] + [h[A_ROPE_DIM:A_ROPE_DIM + rest] for h in heads]
    second = [h[r:A_ROPE_DIM] for h in heads] + [h[A_ROPE_DIM + rest:] for h in heads]
    return np.concatenate(first + second)


ROPE_SPAN = 16
_A_PERM = _a_pair_perm()
_A_PERM_ALL = np.concatenate([c * LANES + _A_PERM for c in range(A_WIDTH // LANES)])
_B_SPLIT = LANES // 2 - B_ROPE_DIM // 2
_B_PERM = np.concatenate([np.arange(B_NOPE_DIM, B_NOPE_DIM + B_ROPE_DIM // 2), np.arange(0, _B_SPLIT),
                          np.arange(B_NOPE_DIM + B_ROPE_DIM // 2, B_QK_DIM), np.arange(_B_SPLIT, B_NOPE_DIM)])


def _lane_mask(member):
    lane = lax.broadcasted_iota(jnp.int32, (1, LANES), 1)
    edges = np.flatnonzero(np.diff(np.concatenate([[0], member.astype(np.int8), [0]])))
    mask = None
    for lo, hi in zip(edges[::2], edges[1::2]):
        run = (lane >= int(lo)) & (lane < int(hi))
        mask = run if mask is None else (mask | run)
    return mask


def _params(n_axes):
    return pltpu.CompilerParams(dimension_semantics=("arbitrary",) * n_axes,
                                vmem_limit_bytes=VMEM_LIMIT)


def _row_rms_scale(x, width):
    return lax.rsqrt(jnp.sum(x * x, axis=-1, keepdims=True) * (1.0 / width) + EPS)


def _dot(a, b):
    return jnp.dot(a, b, preferred_element_type=F32)


def _dot_nt(a, b):
    return lax.dot_general(a, b, (((1,), (1,)), ((), ())), preferred_element_type=F32)


def _rope_chunk(xc, cos, sin):
    return xc * cos + pltpu.roll(xc, LANES // 2, axis=1) * sin


def _proj_kernel(x_ref, gmix_ref, win_ref, gqa_ref, gka_ref, gcq_ref, gckv_ref,
                 wuq_ref, wuk_ref, wuv_ref, gqb_ref, gkb_ref, aones_ref, ones_ref, taba_ref, tabb_ref,
                 qa_ref, ka_ref, va_ref, qb_ref, kb_ref, vbt_ref):
    x = x_ref[...]
    h = (x * _row_rms_scale(x, x.shape[-1]) * gmix_ref[...]).astype(BF16)
    proj = _dot(h, win_ref[...])

    cos_a, sin_a = taba_ref[0], taba_ref[1]
    cos_b, sin_b = tabb_ref[0], tabb_ref[1]

    def a_heads(col0, g_ref, out_ref):
        for c2 in range(A_WIDTH // (2 * LANES)):
            xs = proj[:, col0 + 2 * c2 * LANES: col0 + 2 * (c2 + 1) * LANES]
            ssq = _dot((xs * xs).astype(BF16), aones_ref[...])
            for i in range(2):
                c = 2 * c2 + i
                lanes = slice(i * LANES, (i + 1) * LANES)
                rs = lax.rsqrt(ssq[:, lanes] * (1.0 / A_HEAD_DIM) + EPS)
                y = xs[:, lanes] * rs * g_ref[:, c * LANES:(c + 1) * LANES]
                y = _rope_chunk(y, cos_a, sin_a)
                out_ref[:, c * LANES:(c + 1) * LANES] = y.astype(out_ref.dtype)

    a_heads(0, gqa_ref, qa_ref)
    a_heads(A_WIDTH, gka_ref, ka_ref)
    va_ref[...] = proj[:, 2 * A_WIDTH:3 * A_WIDTH].astype(va_ref.dtype)

    c0 = 3 * A_WIDTH
    cq = proj[:, c0:c0 + Q_LORA]
    ckv = proj[:, c0 + Q_LORA:c0 + Q_LORA + KV_LORA]
    kr = proj[:, c0 + Q_LORA + KV_LORA:c0 + Q_LORA + 2 * KV_LORA]
    cqn = (cq * _row_rms_scale(cq, Q_LORA) * gcq_ref[...]).astype(BF16)
    ckvn = (ckv * _row_rms_scale(ckv, KV_LORA) * gckv_ref[...]).astype(BF16)
    qb = _dot(cqn, wuq_ref[...])
    kn = _dot(ckvn, wuk_ref[...])
    vbt_ref[...] = _dot(ckvn, wuv_ref[...]).T.astype(vbt_ref.dtype)

    def b_heads(get_chunk, g_ref, out_ref):
        for c2 in range(B_HEADS // 2):
            xs = [get_chunk(2 * c2 + i) for i in range(2)]
            ssq = _dot(jnp.concatenate([xc * xc for xc in xs], axis=1).astype(BF16), ones_ref[...])
            for i, xc in enumerate(xs):
                c = 2 * c2 + i
                rs = lax.rsqrt(ssq[:, i * LANES:(i + 1) * LANES] * (1.0 / B_QK_DIM) + EPS)
                y = xc * rs * g_ref[:, c * LANES:(c + 1) * LANES]
                y = _rope_chunk(y, cos_b, sin_b)
                out_ref[:, c * LANES:(c + 1) * LANES] = y.astype(out_ref.dtype)

    b_heads(lambda c: qb[:, c * LANES:(c + 1) * LANES], gqb_ref, qb_ref)
    b_heads(lambda c: kn[:, c * LANES:(c + 1) * LANES] + kr, gkb_ref, kb_ref)


def _proj(x2d, seq, w, tm):
    n, d = x2d.shape
    nblk_seq = seq // tm
    row = lambda i: (i, 0)
    const = lambda i: (0, 0)
    tab = lambda i: (0, i % nblk_seq, 0)
    full = lambda a: pl.BlockSpec(a.shape, const, pipeline_mode=pl.Buffered(1))
    widths = [A_WIDTH, A_WIDTH, A_WIDTH, B_QK_PAD, B_QK_PAD]
    return pl.pallas_call(
        _proj_kernel,
        grid=(n // tm,),
        in_specs=[pl.BlockSpec((tm, d), row), full(w['gmix']), full(w['win']), full(w['gqa']), full(w['gka']),
                  full(w['gcq']), full(w['gckv']), full(w['wuq']), full(w['wuk']), full(w['wuv']),
                  full(w['gqb']), full(w['gkb']), full(w['pair_ones']), full(w['head_ones']),
                  pl.BlockSpec((2, tm, LANES), tab), pl.BlockSpec((2, tm, LANES), tab)],
        out_specs=[pl.BlockSpec((tm, wd), row) for wd in widths]
                  + [pl.BlockSpec((B_WIDTH, tm), lambda i: (0, i))],
        out_shape=[jax.ShapeDtypeStruct((n, wd), F32 if wd == A_WIDTH else BF16) for wd in widths]
                  + [jax.ShapeDtypeStruct((B_WIDTH, n), BF16)],
        compiler_params=_params(1),
        name="proj",
    )(x2d, w['gmix'], w['win'], w['gqa'], w['gka'], w['gcq'], w['gckv'], w['wuq'], w['wuk'], w['wuv'],
      w['gqb'], w['gkb'], w['pair_ones'], w['head_ones'], w['taba'], w['tabb'])


DIL_TQ = 128
DIL_WIN = DIL_TQ + 2 * A_HALF_WINDOW
DIL_SLOTS = 4
DIL_STEPS = 16


def _strided_rows(start, size, stride):
    return pl.ds(start, size) if stride == 1 else pl.ds(start, size, stride=stride)


def _dilated_kernel(q_ref, k_ref, v_ref, o_ref, m_scr, den_scr, s_scr, bias_scr):
    seq = q_ref.shape[1]
    lane = lax.broadcasted_iota(jnp.int32, (1, LANES), 1)
    low_head = lane < A_HEAD_DIM
    q_first = _lane_mask(_A_PERM < A_HEAD_DIM)
    rel = (lax.broadcasted_iota(jnp.int32, (2 * DIL_TQ, DIL_WIN), 1)
           - lax.broadcasted_iota(jnp.int32, (2 * DIL_TQ, DIL_WIN), 0) % DIL_TQ)
    for place in range(3):
        bias_scr[place] = jnp.where(jnp.abs(rel - place * A_HALF_WINDOW) <= A_HALF_WINDOW, 0.0, NEG_BIG)

    for dil in DILATIONS:
        sub_len = seq // dil
        n_blk = sub_len // DIL_TQ
        n_total = dil * n_blk

        def coords(idx, dil=dil, sub_len=sub_len, n_blk=n_blk):
            r = idx // n_blk
            q0 = (idx % n_blk) * DIL_TQ
            k0 = jnp.clip(q0 - A_HALF_WINDOW, 0, sub_len - DIL_WIN)
            rows_q = _strided_rows(r + dil * q0, DIL_TQ, dil)
            rows_k = _strided_rows(r + dil * k0, DIL_WIN, dil)
            return rows_q, rows_k, (q0 - k0) // A_HALF_WINDOW

        def scores_into(slot, idx, coords=coords):
            rows_q, rows_k, place = coords(idx)
            qp = q_ref[0, rows_q, :]
            kw = k_ref[0, rows_k, :].astype(BF16)
            q2 = jnp.concatenate([jnp.where(q_first, qp, 0.0), jnp.where(q_first, 0.0, qp)], axis=0).astype(BF16)
            s_scr[slot] = _dot_nt(q2, kw) + bias_scr[place]

        def finish(slot, idx, dil=dil, coords=coords):
            rows_q, rows_k, _ = coords(idx)
            vw = v_ref[0, rows_k, :].astype(BF16)
            m2 = jnp.max(s_scr[slot], axis=-1, keepdims=True)
            pr = jnp.exp2(s_scr[slot] - m2)
            den2 = jnp.sum(pr, axis=-1, keepdims=True)
            o2 = _dot(pr.astype(BF16), vw)
            pick = lambda a: jnp.where(low_head, a[:DIL_TQ], a[DIL_TQ:])
            o, m, den = pick(o2), pick(m2), pick(den2)
            if dil != DILATIONS[0]:
                o_prev, m_prev, den_prev = o_ref[0, rows_q, :], m_scr[rows_q, :], den_scr[rows_q, :]
                top = jnp.maximum(m_prev, m)
                w_prev, w_new = jnp.exp2(m_prev - top), jnp.exp2(m - top)
                o = w_prev * o_prev + w_new * o
                den = w_prev * den_prev + w_new * den
                m = top
            if dil != DILATIONS[-1]:
                o_ref[0, rows_q, :] = o
                m_scr[rows_q, :] = m
                den_scr[rows_q, :] = den
            else:
                o_ref[0, rows_q, :] = o / den

        assert n_total % DIL_STEPS == 0 and DIL_STEPS % DIL_SLOTS == 0
        for slot in range(DIL_SLOTS - 1):
            scores_into(slot, slot)

        def step(i, carry, scores_into=scores_into, finish=finish, n_total=n_total):
            for u in range(DIL_STEPS):
                idx = DIL_STEPS * i + u
                ahead = idx + DIL_SLOTS - 1
                scores_into((u + DIL_SLOTS - 1) % DIL_SLOTS, jnp.where(ahead >= n_total, ahead - n_total, ahead))
                finish(u % DIL_SLOTS, idx)
            return carry

        lax.fori_loop(0, n_total // DIL_STEPS, step, 0)


def _dilated(qa, ka, va):
    b, s, w = qa.shape
    spec = pl.BlockSpec((1, s, LANES), lambda bi, p: (bi, 0, p))
    out = pl.pallas_call(
        _dilated_kernel,
        grid=(b, w // LANES),
        in_specs=[spec, spec, spec],
        out_specs=spec,
        out_shape=jax.ShapeDtypeStruct((b, s, w), F32),
        scratch_shapes=[pltpu.VMEM((s, LANES), F32), pltpu.VMEM((s, LANES), F32),
                        pltpu.VMEM((DIL_SLOTS, 2 * DIL_TQ, DIL_WIN), F32),
                        pltpu.VMEM((3, 2 * DIL_TQ, DIL_WIN), F32)],
        compiler_params=_params(2),
        name="dilated",
    )(qa, ka, va)
    return out.reshape(b * s, w)


MLA_DEN_ROWS = 16
MLA_STEPS = 64
MLA_SLOTS = 16
MLA_AHEAD = 2


def _mla_kernel(q_ref, k_ref, vt_ref, o_ref, s_scr, smax_scr, *, tq, tk, steps):
    seq = k_ref.shape[1]
    heads = range(2)
    n_blocks = seq // tk
    n_tiles = seq // tq
    slots = s_scr.shape[0]
    tiles_per_iter = steps // n_blocks
    assert steps % n_blocks == 0 and n_tiles % tiles_per_iter == 0 and steps % slots == 0 and MLA_AHEAD < slots

    def scores_into(slot, tile, blk, hh):
        q0 = tile * tq if isinstance(tile, int) else pl.multiple_of(tile * tq, tq)
        cols = slice(hh * LANES, (hh + 1) * LANES)
        st = _dot_nt(k_ref[0, blk * tk:(blk + 1) * tk, cols], q_ref[0, pl.ds(q0, tq), cols])
        s_scr[slot, hh] = st
        smax_scr[slot, hh] = jnp.max(st.reshape(tk // SUBLANES, SUBLANES, tq), axis=0)

    def fresh():
        return [(jnp.full((1, tq), NEG_BIG, F32), jnp.zeros((B_V_DIM + MLA_DEN_ROWS, tq), F32)) for _ in heads]

    for u in range(MLA_AHEAD):
        for hh in heads:
            scores_into(u, u // n_blocks, u % n_blocks, hh)

    def iteration(it, carry):
        tile0 = it * tiles_per_iter
        state = fresh()
        for u in range(steps):
            tile, blk = tile0 + u // n_blocks, u % n_blocks
            v = u + MLA_AHEAD
            ahead_tile = jnp.minimum(tile0 + v // n_blocks, n_tiles - 1)
            for hh in heads:
                scores_into(v % slots, ahead_tile, v % n_blocks, hh)
                m, acc = state[hh]
                m_new = jnp.maximum(m, jnp.max(smax_scr[u % slots, hh], axis=0, keepdims=True))
                alpha = jnp.exp2(m - m_new)
                pt = jnp.exp2(s_scr[u % slots, hh] - m_new).astype(BF16)
                vt1 = jnp.concatenate([vt_ref[hh * B_V_DIM:(hh + 1) * B_V_DIM, blk * tk:(blk + 1) * tk],
                                       jnp.ones((MLA_DEN_ROWS, tk), BF16)], axis=0)
                acc = alpha * acc + _dot(vt1, pt)
                state[hh] = (m_new, acc)
            if blk == n_blocks - 1:
                ot = jnp.concatenate([acc[:B_V_DIM] / acc[B_V_DIM:B_V_DIM + 1] for _, acc in state], axis=0)
                o_ref[0, pl.ds(pl.multiple_of(tile * tq, tq), tq), :] = ot.T
                state = fresh()
        return carry

    lax.fori_loop(0, n_tiles // tiles_per_iter, iteration, 0)


def _mla(qb, kb, vbt, tq, tk):
    b, s, _ = qb.shape
    qk_spec = pl.BlockSpec((1, s, 2 * LANES), lambda bi, p: (bi, 0, p))
    n_blocks = s // tk
    steps = n_blocks * max(1, MLA_STEPS // n_blocks)
    slots = min(steps, MLA_SLOTS)
    out = pl.pallas_call(
        functools.partial(_mla_kernel, tq=tq, tk=tk, steps=steps),
        grid=(b, B_HEADS // 2),
        in_specs=[qk_spec, qk_spec, pl.BlockSpec((LANES, s), lambda bi, p: (p, bi))],
        out_specs=pl.BlockSpec((1, s, LANES), lambda bi, p: (bi, 0, p)),
        out_shape=jax.ShapeDtypeStruct((b, s, B_WIDTH), F32),
        scratch_shapes=[pltpu.VMEM((slots, 2, tk, tq), F32), pltpu.VMEM((slots, 2, SUBLANES, tq), F32)],
        compiler_params=_params(2),
        name="mla",
    )(qb, kb, vbt)
    return out.reshape(b * s, B_WIDTH)


def _mem_kv_kernel(mem_ref, g_ref, wkv_ref, gk_ref, k_ref, v_ref):
    mem = mem_ref[0]
    mm = (mem * _row_rms_scale(mem, mem.shape[-1]) * g_ref[...]).astype(BF16)
    kv = _dot(mm, wkv_ref[...])
    for hd in range(M_HEADS):
        cols = slice(hd * M_HEAD_DIM, (hd + 1) * M_HEAD_DIM)
        kh = kv[:, cols]
        k_ref[0, :, cols] = (kh * _row_rms_scale(kh, M_HEAD_DIM) * gk_ref[...]).astype(k_ref.dtype)
    v_ref[0] = kv[:, M_WIDTH:].astype(v_ref.dtype)


def _mem_kv(mem, w):
    b, n_mem, d = mem.shape
    const = lambda bi: (0, 0)
    full = lambda a: pl.BlockSpec(a.shape, const)
    blk = pl.BlockSpec((1, n_mem, M_WIDTH), lambda bi: (bi, 0, 0))
    return pl.pallas_call(
        _mem_kv_kernel,
        grid=(b,),
        in_specs=[pl.BlockSpec((1, n_mem, d), lambda bi: (bi, 0, 0)),
                  full(w['gmem_kv']), full(w['wkv_m']), full(w['gk_m'])],
        out_specs=[blk, blk],
        out_shape=[jax.ShapeDtypeStruct((b, n_mem, M_WIDTH), BF16)] * 2,
        compiler_params=_params(1),
        name="mem_kv",
    )(mem, w['gmem_kv'], w['wkv_m'], w['gk_m'])


FFN_CHUNK = 1024


def _post_kernel(x_ref, oa_ref, ob_ref, ga_ref, gb_ref, wa_ref, wb_ref,
                 gmem_ref, wq_ref, gq_ref, k_ref, v_ref, wo_ref, gffn_ref, w1_ref, w2_ref, y_ref):
    oa, ob = oa_ref[0], ob_ref[0]
    na = (oa * _row_rms_scale(oa, A_WIDTH) * ga_ref[...]).astype(BF16)
    nb = (ob * _row_rms_scale(ob, B_WIDTH) * gb_ref[...]).astype(BF16)
    x = x_ref[0] + _dot(na, wa_ref[...]) + _dot(nb, wb_ref[...])

    h = (x * _row_rms_scale(x, x.shape[-1]) * gmem_ref[...]).astype(BF16)
    q = _dot(h, wq_ref[...])
    heads = []
    for hd in range(M_HEADS):
        cols = slice(hd * M_HEAD_DIM, (hd + 1) * M_HEAD_DIM)
        qh = q[:, cols]
        qh = (qh * _row_rms_scale(qh, M_HEAD_DIM) * gq_ref[...]).astype(BF16)
        s = _dot_nt(qh, k_ref[0, :, cols])
        pr = jnp.exp(s - jnp.max(s, axis=-1, keepdims=True))
        pr = pr / jnp.sum(pr, axis=-1, keepdims=True)
        heads.append(_dot(pr.astype(BF16), v_ref[0, :, cols]).astype(BF16))
    x = x + _dot(jnp.concatenate(heads, axis=-1), wo_ref[...])

    h = (x * _row_rms_scale(x, x.shape[-1]) * gffn_ref[...]).astype(BF16)
    for c in range(w1_ref.shape[1] // FFN_CHUNK):
        cols = slice(c * FFN_CHUNK, (c + 1) * FFN_CHUNK)
        a = jnp.maximum(_dot(h, w1_ref[:, cols]), 0.0)
        x = x + _dot((a * a).astype(BF16), w2_ref[cols, :])
    y_ref[0] = x


def _post(x3d, oa, ob, km, vm, w, tm):
    b, s, d = x3d.shape
    n_mem = km.shape[1]
    const = lambda bi, i: (0, 0)
    full = lambda a: pl.BlockSpec(a.shape, const, pipeline_mode=pl.Buffered(1))
    tile = lambda width: pl.BlockSpec((1, tm, width), lambda bi, i: (bi, i, 0))
    kvblk = pl.BlockSpec((1, n_mem, M_WIDTH), lambda bi, i: (bi, 0, 0))
    names = ['ga_out', 'gb_out', 'wout_a', 'wout_b', 'gmem', 'wq_m', 'gq_m']
    tail = ['wo_m', 'gffn', 'w1', 'w2']
    return pl.pallas_call(
        _post_kernel,
        grid=(b, s // tm),
        in_specs=[tile(d), tile(A_WIDTH), tile(B_WIDTH)] + [full(w[n]) for n in names] + [kvblk, kvblk]
                 + [full(w[n]) for n in tail],
        out_specs=tile(d),
        out_shape=jax.ShapeDtypeStruct((b, s, d), F32),
        compiler_params=_params(2),
        name="post",
    )(x3d, oa, ob, *[w[n] for n in names], km, vm, *[w[n] for n in tail])


def _rope_tables(seq, rope_dim):
    half = rope_dim // 2
    assert ROPE_SPAN % half == 0
    inv = ROPE_THETA ** (-jnp.arange(0, rope_dim, 2, dtype=F32) / rope_dim)
    ang = jnp.arange(seq, dtype=F32)[:, None] * inv[None, :]
    cos, sin = jnp.cos(ang), jnp.sin(ang)
    lane = np.arange(LANES)
    is_lo = lane < ROPE_SPAN
    is_hi = (lane >= LANES // 2) & (lane < LANES // 2 + ROPE_SPAN)
    idx = (lane % (LANES // 2)) % half
    cos_t = jnp.where(is_lo | is_hi, cos[:, idx], 1.0)
    sin_t = jnp.where(is_lo, -sin[:, idx], jnp.where(is_hi, sin[:, idx], 0.0))
    return jnp.stack([cos_t, sin_t]).astype(F32)


def _layer_weights(p):
    n_layers, d, _ = p['w_in'].shape
    row = lambda v: v.reshape(n_layers, 1, -1).astype(F32)
    w_in = p['w_in']
    c_kr = 3 * A_WIDTH + Q_LORA + KV_LORA
    half_r = B_ROPE_DIM // 2
    zeros = lambda n: jnp.zeros((n_layers, d, n), F32)
    kr = w_in[..., c_kr:]
    win = jnp.concatenate([
        w_in[..., :A_WIDTH][..., _A_PERM_ALL], w_in[..., A_WIDTH:2 * A_WIDTH][..., _A_PERM_ALL],
        w_in[..., 2 * A_WIDTH:c_kr],
        kr[..., :half_r], zeros(LANES // 2 - half_r), kr[..., half_r:], zeros(LANES // 2 - half_r)], axis=-1)

    def b_lanes(a, with_rope):
        if with_rope:
            a = a[..., _B_PERM]
        else:
            z = jnp.zeros(a.shape[:-1] + (half_r,), a.dtype)
            a = jnp.concatenate([z, a[..., :_B_SPLIT], z, a[..., _B_SPLIT:]], axis=-1)
        return jnp.pad(a, [(0, 0)] * (a.ndim - 1) + [(0, LANES - B_QK_DIM)])

    wuq = p['w_uq'].reshape(n_layers, Q_LORA, B_HEADS, B_QK_DIM)
    wukv = p['w_ukv'].reshape(n_layers, KV_LORA, B_HEADS, B_NOPE_DIM + B_V_DIM)
    b_gain = lambda g: jnp.tile(b_lanes(g, True), (1, B_HEADS))
    a_gain = lambda g: jnp.tile(g, (1, A_HEADS))[..., _A_PERM_ALL]
    stacked = {
        'gmix': row(p['norm_mix_g']),
        'win': win.astype(BF16),
        'gqa': row(a_gain(p['a_q_norm_g']) * (A_HEAD_DIM ** -0.5 * LOG2_E)),
        'gka': row(a_gain(p['a_k_norm_g'])),
        'gcq': row(p['b_cq_norm_g']),
        'gckv': row(p['b_ckv_norm_g']),
        'wuq': b_lanes(wuq, True).reshape(n_layers, Q_LORA, B_QK_PAD).astype(BF16),
        'wuk': b_lanes(wukv[..., :B_NOPE_DIM], False).reshape(n_layers, KV_LORA, B_QK_PAD).astype(BF16),
        'wuv': wukv[..., B_NOPE_DIM:].reshape(n_layers, KV_LORA, B_WIDTH).astype(BF16),
        'gqb': row(b_gain(p['b_q_norm_g']) * (B_QK_DIM ** -0.5 * LOG2_E)),
        'gkb': row(b_gain(p['b_k_norm_g'])),
        'ga_out': row(p['a_out_norm_g']),
        'gb_out': row(p['b_out_norm_g']),
        'wout_a': p['w_out'][:, :A_WIDTH].astype(BF16),
        'wout_b': p['w_out'][:, A_WIDTH:].astype(BF16),
        'gmem': row(p['norm_mem_g']),
        'gmem_kv': row(p['mem_kv_norm_g']),
        'wq_m': p['m_wq'].astype(BF16),
        'wkv_m': p['m_wkv'].astype(BF16),
        'gq_m': row(p['m_q_norm_g'] * (M_HEAD_DIM ** -0.5)),
        'gk_m': row(p['m_k_norm_g']),
        'wo_m': p['m_wo'].astype(BF16),
        'gffn': row(p['norm_ffn_g']),
        'w1': p['w_ff1'].astype(BF16),
        'w2': p['w_ff2'].astype(BF16),
    }
    return [{name: a[l] for name, a in stacked.items()} for l in range(n_layers)]


def _tiles(seq):
    tm = 512 if seq % 512 == 0 else 256
    tm_proj = 1024 if seq % 1024 == 0 else tm
    return dict(tm=tm, tm_proj=tm_proj, tq=256, tk=512)


def _trunk(x, mem, layers):
    b, s, d = x.shape
    assert s % (max(DILATIONS) * DIL_TQ) == 0 and s // max(DILATIONS) >= DIL_WIN
    t = _tiles(s)
    taba = _rope_tables(s, A_ROPE_DIM)
    tabb = _rope_tables(s, B_ROPE_DIM)
    head_ones = jnp.asarray(np.kron(np.eye(2), np.ones((LANES, LANES))), BF16)
    same_head = (_A_PERM[:, None] < A_HEAD_DIM) == (_A_PERM[None, :] < A_HEAD_DIM)
    pair_ones = jnp.asarray(np.kron(np.eye(2), same_head), BF16)
    x2d = x.reshape(b * s, d)
    for w in layers:
        w = dict(w, taba=taba, tabb=tabb, head_ones=head_ones, pair_ones=pair_ones)
        qa, ka, va, qb, kb, vbt = _proj(x2d, s, w, t['tm_proj'])
        as3d = lambda a: a.reshape(b, s, a.shape[-1])
        oa = _dilated(as3d(qa), as3d(ka), as3d(va))
        ob = _mla(as3d(qb), as3d(kb), vbt, t['tq'], t['tk'])
        km, vm = _mem_kv(mem, w)
        x2d = _post(as3d(x2d), as3d(oa), as3d(ob), km, vm, w, t['tm']).reshape(b * s, d)
    return x2d.reshape(b, s, d)


def kernel(x_prompt, x_sample, mem_prompt, mem_sample, norm_mix_g, w_in, a_q_norm_g, a_k_norm_g, b_cq_norm_g, b_ckv_norm_g, w_uq, w_ukv, b_q_norm_g, b_k_norm_g, a_out_norm_g, b_out_norm_g, w_out, norm_mem_g, mem_kv_norm_g, m_wq, m_wkv, m_q_norm_g, m_k_norm_g, m_wo, norm_ffn_g, w_ff1, w_ff2):
    p = dict(norm_mix_g=norm_mix_g, w_in=w_in, a_q_norm_g=a_q_norm_g, a_k_norm_g=a_k_norm_g,
             b_cq_norm_g=b_cq_norm_g, b_ckv_norm_g=b_ckv_norm_g, w_uq=w_uq, w_ukv=w_ukv,
             b_q_norm_g=b_q_norm_g, b_k_norm_g=b_k_norm_g, a_out_norm_g=a_out_norm_g,
             b_out_norm_g=b_out_norm_g, w_out=w_out, norm_mem_g=norm_mem_g, mem_kv_norm_g=mem_kv_norm_g,
             m_wq=m_wq, m_wkv=m_wkv, m_q_norm_g=m_q_norm_g, m_k_norm_g=m_k_norm_g, m_wo=m_wo,
             norm_ffn_g=norm_ffn_g, w_ff1=w_ff1, w_ff2=w_ff2)
    layers = _layer_weights(p)
    return (_trunk(x_prompt, mem_prompt, layers), _trunk(x_sample, mem_sample, layers))
```
